```python
import math
import jax, jax.numpy as jnp
from jax import lax
import numpy as np

D_MODEL = 1024
BATCH = 8
SEQ = 4096
DEPTH = 1

HEAD_DIM = D_MODEL // 16
DIL_GROUPS = ((128, 1), (512, 4), (2048, 16))
N_DIL_GROUPS = len(DIL_GROUPS)
HEADS_PER_GROUP = 4
N_ATTN_HEADS = N_DIL_GROUPS * HEADS_PER_GROUP
ATTN_W = N_ATTN_HEADS * HEAD_DIM
A_OUT = HEADS_PER_GROUP * HEAD_DIM
WINDOW_STEPS = 128
Q_BLOCK = 128
SG_CHUNK = 128
SG_GROUPS = 8
SG_W = SG_GROUPS * HEAD_DIM
MEM_LEN = 256
MEM_HEADS = 4
MEM_W = MEM_HEADS * HEAD_DIM
N_BRANCH = 3
IN_W = 3 * ATTN_W + 2 * SG_W + MEM_W
SPLITS = (ATTN_W, 2 * ATTN_W, 3 * ATTN_W, 3 * ATTN_W + 2 * SG_W)
D_FF = 11 * D_MODEL // 4
CONV_W = 3
N_BUCKETS = 32
MAX_DISTANCE = 2048
LN_EPS = 1e-5
ALPHA = (2 * DEPTH) ** 0.25
BETA = (8 * DEPTH) ** -0.25
ATTN_SCALE = HEAD_DIM ** -0.5
NEG = -1e30

kernel_name = 'gated_hybrid_dilated_gmlp_memxattn_convffn'


def layer_norm(x, g, b):
    xf = x.astype(jnp.float32)
    mu = xf.mean(-1, keepdims=True)
    var = jnp.square(xf - mu).mean(-1, keepdims=True)
    return ((xf - mu) * lax.rsqrt(var + LN_EPS) * g.astype(jnp.float32) + b.astype(jnp.float32)).astype(x.dtype)


def t5_bucket(dist):
    max_exact = N_BUCKETS // 2
    n = jnp.maximum(dist, 1).astype(jnp.float32)
    large = max_exact + (jnp.log(n / max_exact) / math.log(MAX_DISTANCE / max_exact)
                         * (N_BUCKETS - max_exact)).astype(jnp.int32)
    large = jnp.minimum(large, N_BUCKETS - 1)
    return jnp.where(dist < max_exact, dist, large)


def dilated_window_attention(q, k, v, bias_tab, dilation):
    b_, s_, h_, e_ = q.shape
    L = s_ // dilation
    nb = -(-L // Q_BLOCK)
    pad = nb * Q_BLOCK - L

    def split(t):
        t = t.reshape(b_, L, dilation, h_, e_).transpose(0, 2, 3, 1, 4)
        t = jnp.pad(t, ((0, 0), (0, 0), (0, 0), (0, pad), (0, 0)))
        return t.reshape(b_, dilation, h_, nb, Q_BLOCK, e_).astype(jnp.float32)

    def with_prev(t):
        prev = jnp.pad(t, ((0, 0), (0, 0), (0, 0), (1, 0), (0, 0), (0, 0)))[:, :, :, :-1]
        return jnp.concatenate([prev, t], axis=4)

    qb = split(q)
    kb = with_prev(split(k))
    vb = with_prev(split(v))

    qi = jnp.arange(Q_BLOCK)[:, None]
    kj = jnp.arange(2 * Q_BLOCK)[None, :]
    steps = qi + Q_BLOCK - kj
    band = (steps >= 0) & (steps <= WINDOW_STEPS)
    bucket = t5_bucket(jnp.maximum(steps, 0) * dilation)
    bias = bias_tab[bucket].astype(jnp.float32).transpose(2, 0, 1)
    blk = jnp.arange(nb)[:, None, None]
    valid = band[None] & ((blk > 0) | (kj[None] >= Q_BLOCK))

    s = jnp.einsum('bdhnqe,bdhnke->bdhnqk', qb, kb) * ATTN_SCALE + bias[None, None, :, None]
    s = jnp.where(valid[None, None, None], s, NEG)
    m = s.max(-1, keepdims=True)
    p = jnp.exp(s - m)
    den = p.sum(-1, keepdims=True)
    o = jnp.einsum('bdhnqk,bdhnke->bdhnqe', p, vb) / den
    lse = (m + jnp.log(den))[..., 0]

    o = o.reshape(b_, dilation, h_, nb * Q_BLOCK, e_)[:, :, :, :L]
    o = o.transpose(0, 3, 1, 2, 4).reshape(b_, s_, h_, e_)
    lse = lse.reshape(b_, dilation, h_, nb * Q_BLOCK)[..., :L]
    lse = lse.transpose(0, 3, 1, 2).reshape(b_, s_, h_)
    return o, lse


def token_mixing_sublayer(x, mem, rel_bias, w_in, w_mem_kv, sg_ln_g, sg_ln_b, w_spatial, b_spatial,
                          w_proj_a, w_proj_b, w_proj_c, w_gate, b_gate, w_out, ln_g, ln_b):
    b_, s_, _ = x.shape
    proj = x @ w_in
    qa, ka, va, zb, qc = jnp.split(proj, SPLITS, axis=-1)

    hs = (b_, s_, N_DIL_GROUPS, HEADS_PER_GROUP, HEAD_DIM)
    qa, ka, va = qa.reshape(hs), ka.reshape(hs), va.reshape(hs)
    outs, lses = [], []
    for g, (_, dil) in enumerate(DIL_GROUPS):
        o_g, lse_g = dilated_window_attention(
            qa[:, :, g], ka[:, :, g], va[:, :, g],
            rel_bias[:, g * HEADS_PER_GROUP:(g + 1) * HEADS_PER_GROUP], dil)
        outs.append(o_g)
        lses.append(lse_g)
    wts = jax.nn.softmax(jnp.stack(lses, 0), axis=0)
    oa = (wts[..., None] * jnp.stack(outs, 0)).sum(0)
    oa = oa.reshape(b_, s_, A_OUT).astype(x.dtype)

    zb = jax.nn.gelu(zb)
    u, vg = jnp.split(zb, 2, axis=-1)
    vg = layer_norm(vg, sg_ln_g, sg_ln_b).astype(jnp.float32)
    vg = vg.reshape(b_, s_ // SG_CHUNK, SG_CHUNK, SG_GROUPS, HEAD_DIM)
    tril = jnp.tril(jnp.ones((SG_CHUNK, SG_CHUNK), jnp.float32))
    sv = jnp.einsum('gpq,bcqge->bcpge', w_spatial.astype(jnp.float32) * tril, vg)
    sv = sv + b_spatial.astype(jnp.float32).T[None, None, :, :, None]
    ob = (u.astype(jnp.float32) * sv.reshape(b_, s_, SG_W)).astype(x.dtype)

    kv = (mem @ w_mem_kv).reshape(b_, MEM_LEN, 2, MEM_HEADS, HEAD_DIM)
    kc, vc = kv[:, :, 0].astype(jnp.float32), kv[:, :, 1].astype(jnp.float32)
    qc = qc.reshape(b_, s_, MEM_HEADS, HEAD_DIM).astype(jnp.float32)
    pc = jax.nn.softmax(jnp.einsum('bshe,bmhe->bhsm', qc, kc) * ATTN_SCALE, axis=-1)
    oc = jnp.einsum('bhsm,bmhe->bshe', pc, vc).reshape(b_, s_, MEM_W).astype(x.dtype)

    gates = jax.nn.sigmoid((x @ w_gate + b_gate).astype(jnp.float32)).reshape(b_, s_, N_BRANCH, D_MODEL)
    merged = (gates[:, :, 0] * (oa @ w_proj_a).astype(jnp.float32)
              + gates[:, :, 1] * (ob @ w_proj_b).astype(jnp.float32)
              + gates[:, :, 2] * (oc @ w_proj_c).astype(jnp.float32))
    y = merged.astype(x.dtype) @ w_out
    return layer_norm(ALPHA * x + y, ln_g, ln_b)


def conv_ffn_sublayer(x, w_ffn_up, conv_w, conv_b, w_ffn_down, ln_g, ln_b):
    a, bval = jnp.split(x @ w_ffn_up, 2, axis=-1)
    a = lax.conv_general_dilated(a, conv_w[:, None, :], window_strides=(1,),
                                 padding=((CONV_W - 1, 0),),
                                 dimension_numbers=('NWC', 'WIO', 'NWC'),
                                 feature_group_count=D_FF) + conv_b
    y = (jax.nn.gelu(a) * bval) @ w_ffn_down
    return layer_norm(ALPHA * x + y, ln_g, ln_b)


def setup_inputs(seed: int = 0) -> dict:
    key = jax.random.key(seed)
    ks = jax.random.split(key, 24)
    f32 = jnp.float32

    def nrm(k, shape, scale):
        return jax.random.normal(k, shape, f32) * scale

    L = DEPTH
    in_scale = jnp.ones((IN_W,), f32).at[2 * ATTN_W:3 * ATTN_W].set(BETA)
    mem_scale = jnp.ones((2 * MEM_W,), f32).at[MEM_W:].set(BETA)
    return {
        'x': nrm(ks[0], (BATCH, SEQ, D_MODEL), 1.0),
        'mem': nrm(ks[1], (BATCH, MEM_LEN, D_MODEL), 1.0),
        'rel_bias': nrm(ks[2], (N_BUCKETS, N_ATTN_HEADS), 0.5),
        'w_in': nrm(ks[3], (L, D_MODEL, IN_W), D_MODEL ** -0.5) * in_scale,
        'w_mem_kv': nrm(ks[4], (L, D_MODEL, 2 * MEM_W), D_MODEL ** -0.5) * mem_scale,
        'sg_ln_g': 1.0 + nrm(ks[5], (L, SG_W), 0.02),
        'sg_ln_b': nrm(ks[6], (L, SG_W), 0.02),
        'w_spatial': nrm(ks[7], (L, SG_GROUPS, SG_CHUNK, SG_CHUNK), SG_CHUNK ** -0.5),
        'b_spatial': 1.0 + nrm(ks[8], (L, SG_GROUPS, SG_CHUNK), 0.1),
        'w_proj_a': nrm(ks[9], (L, A_OUT, D_MODEL), A_OUT ** -0.5 * BETA),
        'w_proj_b': nrm(ks[10], (L, SG_W, D_MODEL), SG_W ** -0.5 * BETA),
        'w_proj_c': nrm(ks[11], (L, MEM_W, D_MODEL), MEM_W ** -0.5 * BETA),
        'w_gate': nrm(ks[12], (L, D_MODEL, N_BRANCH * D_MODEL), D_MODEL ** -0.5),
        'b_gate': nrm(ks[13], (L, N_BRANCH * D_MODEL), 0.01),
        'w_out': nrm(ks[14], (L, D_MODEL, D_MODEL), D_MODEL ** -0.5 * BETA),
        'ln1_g': 1.0 + nrm(ks[15], (L, D_MODEL), 0.02),
        'ln1_b': nrm(ks[16], (L, D_MODEL), 0.02),
        'w_ffn_up': nrm(ks[17], (L, D_MODEL, 2 * D_FF), D_MODEL ** -0.5),
        'conv_w': nrm(ks[18], (L, CONV_W, D_FF), CONV_W ** -0.5),
        'conv_b': nrm(ks[19], (L, D_FF), 0.01),
        'w_ffn_down': nrm(ks[20], (L, D_FF, D_MODEL), D_FF ** -0.5 * BETA),
        'ln2_g': 1.0 + nrm(ks[21], (L, D_MODEL), 0.02),
        'ln2_b': nrm(ks[22], (L, D_MODEL), 0.02),
    }


def reference(x, mem, rel_bias, w_in, w_mem_kv, sg_ln_g, sg_ln_b, w_spatial, b_spatial,
              w_proj_a, w_proj_b, w_proj_c, w_gate, b_gate, w_out, ln1_g, ln1_b,
              w_ffn_up, conv_w, conv_b, w_ffn_down, ln2_g, ln2_b):
    h = x
    for l in range(DEPTH):
        h = token_mixing_sublayer(h, mem, rel_bias, w_in[l], w_mem_kv[l], sg_ln_g[l], sg_ln_b[l],
                                  w_spatial[l], b_spatial[l], w_proj_a[l], w_proj_b[l], w_proj_c[l],
                                  w_gate[l], b_gate[l], w_out[l], ln1_g[l], ln1_b[l])
        h = conv_ffn_sublayer(h, w_ffn_up[l], conv_w[l], conv_b[l], w_ffn_down[l], ln2_g[l], ln2_b[l])
    return h
```

```python
import functools
import math

import jax
import jax.numpy as jnp
from jax import lax
from jax.experimental import pallas as pl
from jax.experimental.pallas import tpu as pltpu

F32 = jnp.float32
BF16 = jnp.bfloat16

D_MODEL = 1024
HEAD_DIM = 64
DILATIONS = (1, 4, 16)
N_GROUPS = 3
HEADS = 4
HW = HEADS * HEAD_DIM
ATTN_W = N_GROUPS * HW
WINDOW_STEPS = 128
Q_BLOCK = 128
SG_CHUNK = 128
SG_W = 512
MEM_LEN = 256
IN_W = 3 * ATTN_W + 2 * SG_W + HW
D_FF = 2816
N_BUCKETS = 32
MAX_DISTANCE = 2048
LN_EPS = 1e-5
DEPTH = 1
ALPHA = (2 * DEPTH) ** 0.25
ATTN_SCALE = HEAD_DIM ** -0.5
NEG = -1e30

TOKEN_TILE = 512
SUPER = Q_BLOCK * DILATIONS[-1]
FF_CHUNK = 256
CARRY_ROWS = 8
VMEM_LIMIT = 56 * 1024 * 1024

_NT = (((1,), (1,)), ((), ()))


def _gelu(x):
    return 0.5 * x * (1.0 + jnp.tanh(math.sqrt(2.0 / math.pi) * (x + 0.044715 * (x * x * x))))


def _layer_norm(x, g, b):
    mu = jnp.mean(x, axis=-1, keepdims=True)
    xc = x - mu
    var = jnp.mean(xc * xc, axis=-1, keepdims=True)
    return xc * lax.rsqrt(var + LN_EPS) * g + b


def _resident(shape):
    zeros = (0,) * len(shape)
    return pl.BlockSpec(shape, lambda *_: zeros, pipeline_mode=pl.Buffered(1))


def _packed_heads_attention(q, k, v, hm_ref, lane_head, bias_fn):
    out = jnp.zeros(q.shape, F32)
    lse = jnp.zeros(q.shape, F32)
    for h in range(HEADS):
        s = lax.dot_general(q * hm_ref[h], k, _NT, preferred_element_type=F32)
        s = bias_fn(s, h)
        m = jnp.max(s, axis=-1, keepdims=True)
        p = jnp.exp(s - m)
        l = jnp.sum(p, axis=-1, keepdims=True)
        pv = jnp.dot(p.astype(BF16), v, preferred_element_type=F32)
        sel = lane_head == h
        out = jnp.where(sel, pv * (1.0 / l), out)
        lse = jnp.where(sel, m + jnp.log(l), lse)
    return out, lse


def _memkv_kernel(mem_ref, w_ref, o_ref):
    o_ref[0] = jnp.dot(mem_ref[0].astype(BF16), w_ref[...],
                       preferred_element_type=F32).astype(BF16)


def _proj_kernel(x_ref, win_ref, kvm_ref, lng_ref, lnb_ref, wsp_ref, bsp_ref, hm_ref,
                 q1_ref, q2_ref, q3_ref, kv1_ref, kv2_ref, kv3_ref, ob_ref, oc_ref):
    xb = x_ref[...].astype(BF16)
    tm = xb.shape[0]

    def proj(lo, width):
        return jnp.dot(xb, win_ref[:, lo:lo + width], preferred_element_type=F32)

    for g, (q_ref, kv_ref) in enumerate(((q1_ref, kv1_ref), (q2_ref, kv2_ref), (q3_ref, kv3_ref))):
        q_ref[...] = (proj(g * HW, HW) * ATTN_SCALE).astype(BF16)
        kv_ref[:, :HW] = proj(ATTN_W + g * HW, HW).astype(BF16)
        kv_ref[:, HW:] = proj(2 * ATTN_W + g * HW, HW).astype(BF16)

    u = _gelu(proj(3 * ATTN_W, SG_W))
    vn = _layer_norm(_gelu(proj(3 * ATTN_W + SG_W, SG_W)), lng_ref[...], lnb_ref[...]).astype(BF16)
    row = lax.broadcasted_iota(jnp.int32, (2 * SG_CHUNK, SG_CHUNK), 0) & (SG_CHUNK - 1)
    col = lax.broadcasted_iota(jnp.int32, (2 * SG_CHUNK, SG_CHUNK), 1)
    low_half = lax.broadcasted_iota(jnp.int32, (SG_CHUNK, SG_CHUNK), 1) < HEAD_DIM
    for j in range(SG_W // SG_CHUNK):
        w_pair = jnp.where(row >= col, wsp_ref[j], 0.0).astype(BF16)
        for c in range(tm // SG_CHUNK):
            rows = slice(c * SG_CHUNK, (c + 1) * SG_CHUNK)
            cols = slice(j * SG_CHUNK, (j + 1) * SG_CHUNK)
            r = jnp.dot(w_pair, vn[rows, cols], preferred_element_type=F32)
            sv = jnp.where(low_half, r[:SG_CHUNK], r[SG_CHUNK:]) + bsp_ref[j]
            ob_ref[rows, cols] = (u[rows, cols] * sv).astype(BF16)

    qc = (proj(3 * ATTN_W + 2 * SG_W, HW) * ATTN_SCALE).astype(BF16)
    lane_head = lax.broadcasted_iota(jnp.int32, (tm, HW), 1) // HEAD_DIM
    oc, _ = _packed_heads_attention(qc, kvm_ref[0, :, :HW], kvm_ref[0, :, HW:], hm_ref,
                                    lane_head, lambda s, h: s)
    oc_ref[...] = oc.astype(BF16)


def _make_attn_kernel(dil):
    rows = SUPER // dil
    nblk = rows // Q_BLOCK

    def kernel(q_ref, kv_ref, kvp_ref, bias_ref, hm_ref, o_ref, lse_ref):
        lane = lax.broadcasted_iota(jnp.int32, (Q_BLOCK, 2 * Q_BLOCK), 1)
        no_prev = jnp.where(lane < Q_BLOCK, jnp.where(pl.program_id(1) == 0, NEG, 0.0), 0.0)
        lane_head = lane // HEAD_DIM
        for r in range(dil):
            kc = slice(r * 2 * HW, r * 2 * HW + HW)
            vc = slice(r * 2 * HW + HW, (r + 1) * 2 * HW)
            qc = slice(r * HW, (r + 1) * HW)
            for i in range(nblk):
                rq = slice(i * Q_BLOCK, (i + 1) * Q_BLOCK)
                q = q_ref[0, rq, qc]
                if i == 0:
                    k2 = jnp.concatenate([kvp_ref[0, :, kc], kv_ref[0, rq, kc]], axis=0)
                    v2 = jnp.concatenate([kvp_ref[0, :, vc], kv_ref[0, rq, vc]], axis=0)
                    bias_fn = lambda s, h: s + bias_ref[h] + no_prev
                else:
                    rk = slice((i - 1) * Q_BLOCK, (i + 1) * Q_BLOCK)
                    k2 = kv_ref[0, rk, kc]
                    v2 = kv_ref[0, rk, vc]
                    bias_fn = lambda s, h: s + bias_ref[h]
                out, lse = _packed_heads_attention(q, k2, v2, hm_ref, lane_head, bias_fn)
                o_ref[0, rq, qc] = out
                lse_ref[0, rq, qc] = lse

    return kernel, rows, nblk


def _merge_kernel(x_ref, o1_ref, o2_ref, o3_ref, l1_ref, l2_ref, l3_ref, ob_ref, oc_ref,
                  wg_ref, bg_ref, wpa_ref, wpb_ref, wpc_ref, wout_ref, lng_ref, lnb_ref, h_ref):
    x = x_ref[...]
    xb = x.astype(BF16)

    l1, l2, l3 = l1_ref[...], l2_ref[...], l3_ref[...]
    m = jnp.maximum(jnp.maximum(l1, l2), l3)
    e1, e2, e3 = jnp.exp(l1 - m), jnp.exp(l2 - m), jnp.exp(l3 - m)
    oa = (e1 * o1_ref[...] + e2 * o2_ref[...] + e3 * o3_ref[...]) * (1.0 / (e1 + e2 + e3))

    def gate(k):
        z = jnp.dot(xb, wg_ref[:, k * D_MODEL:(k + 1) * D_MODEL], preferred_element_type=F32)
        return jax.nn.sigmoid(z + bg_ref[:, k * D_MODEL:(k + 1) * D_MODEL])

    merged = gate(0) * jnp.dot(oa.astype(BF16), wpa_ref[...], preferred_element_type=F32)
    merged += gate(1) * jnp.dot(ob_ref[...], wpb_ref[...], preferred_element_type=F32)
    merged += gate(2) * jnp.dot(oc_ref[...], wpc_ref[...], preferred_element_type=F32)
    y = jnp.dot(merged.astype(BF16), wout_ref[...], preferred_element_type=F32)
    h_ref[...] = _layer_norm(ALPHA * x + y, lng_ref[...], lnb_ref[...])


def _ffn_kernel(h_ref, wup_ref, cw_ref, cb_ref, wdn_ref, lng_ref, lnb_ref, o_ref,
                a_buf, carry, g_buf, *, tiles_per_seq):
    tm = h_ref.shape[0]
    n_chunks = D_FF // FF_CHUNK

    @pl.when(pl.program_id(0) % tiles_per_seq == 0)
    def _():
        carry[...] = jnp.zeros(carry.shape, F32)

    h = h_ref[...]
    hb = h.astype(BF16)
    for c in range(n_chunks):
        cols = slice(c * FF_CHUNK, (c + 1) * FF_CHUNK)
        a = jnp.dot(hb, wup_ref[:, cols], preferred_element_type=F32)
        gate = jnp.dot(hb, wup_ref[:, D_FF + c * FF_CHUNK:D_FF + (c + 1) * FF_CHUNK],
                       preferred_element_type=F32)
        a_buf[:CARRY_ROWS] = carry[c]
        a_buf[CARRY_ROWS:] = a
        carry[c] = a[tm - CARRY_ROWS:]
        conv = (cw_ref[0:1, cols] * a_buf[CARRY_ROWS - 2:CARRY_ROWS - 2 + tm]
                + cw_ref[1:2, cols] * a_buf[CARRY_ROWS - 1:CARRY_ROWS - 1 + tm]
                + cw_ref[2:3, cols] * a + cb_ref[:, cols])
        g_buf[:, cols] = (_gelu(conv) * gate).astype(BF16)
    y = jnp.dot(g_buf[...], wdn_ref[...], preferred_element_type=F32)
    o_ref[...] = _layer_norm(ALPHA * h + y, lng_ref[...], lnb_ref[...])


def _t5_bucket(dist):
    max_exact = N_BUCKETS // 2
    n = jnp.maximum(dist, 1).astype(jnp.float32)
    large = max_exact + (jnp.log(n / max_exact) / math.log(MAX_DISTANCE / max_exact)
                         * (N_BUCKETS - max_exact)).astype(jnp.int32)
    large = jnp.minimum(large, N_BUCKETS - 1)
    return jnp.where(dist < max_exact, dist, large)


def _band_bias(rel_bias):
    qi = jnp.arange(Q_BLOCK)[:, None]
    kj = jnp.arange(2 * Q_BLOCK)[None, :]
    steps = qi + Q_BLOCK - kj
    band = (steps >= 0) & (steps <= WINDOW_STEPS)
    tables = []
    for g, dil in enumerate(DILATIONS):
        bucket = _t5_bucket(jnp.maximum(steps, 0) * dil)
        bias = rel_bias[:, g * HEADS:(g + 1) * HEADS][bucket].astype(F32).transpose(2, 0, 1)
        tables.append(jnp.where(band[None], bias, NEG))
    return jnp.stack(tables, 0)


def _cparams(*sem):
    return pltpu.CompilerParams(dimension_semantics=sem, vmem_limit_bytes=VMEM_LIMIT)


def _layer(x, mem, band_bias, hmask, w_in, w_mem_kv, sg_ln_g, sg_ln_b, w_spatial, b_spatial,
           w_proj_a, w_proj_b, w_proj_c, w_gate, b_gate, w_out, ln1_g, ln1_b,
           w_ffn_up, conv_w, conv_b, w_ffn_down, ln2_g, ln2_b):
    bsz, seq, _ = x.shape
    n_tok = bsz * seq
    tm = TOKEN_TILE
    assert seq % SUPER == 0 and seq % tm == 0 and tm % SG_CHUNK == 0 and D_FF % FF_CHUNK == 0
    n_tiles = n_tok // tm
    tiles_per_seq = seq // tm
    x2 = x.reshape(n_tok, D_MODEL)
    row = lambda v: v.reshape(1, -1)
    tok = lambda width: pl.BlockSpec((tm, width), lambda i: (i, 0))

    kvm = pl.pallas_call(
        _memkv_kernel,
        grid=(bsz,),
        in_specs=[pl.BlockSpec((1, MEM_LEN, D_MODEL), lambda b: (b, 0, 0)),
                  _resident((D_MODEL, 2 * HW))],
        out_specs=pl.BlockSpec((1, MEM_LEN, 2 * HW), lambda b: (b, 0, 0)),
        out_shape=jax.ShapeDtypeStruct((bsz, MEM_LEN, 2 * HW), BF16),
        compiler_params=_cparams("parallel"),
        name="memkv",
    )(mem, w_mem_kv.astype(BF16))

    wsp = w_spatial.reshape(SG_W // SG_CHUNK, 2 * SG_CHUNK, SG_CHUNK)
    bsp = jnp.repeat(b_spatial.reshape(SG_W // SG_CHUNK, 2, SG_CHUNK).transpose(0, 2, 1),
                     HEAD_DIM, axis=-1)
    q1, q2, q3, kv1, kv2, kv3, ob, oc = pl.pallas_call(
        _proj_kernel,
        grid=(n_tiles,),
        in_specs=[tok(D_MODEL), _resident((D_MODEL, IN_W)),
                  pl.BlockSpec((1, MEM_LEN, 2 * HW), lambda i: (i // tiles_per_seq, 0, 0)),
                  _resident((1, SG_W)), _resident((1, SG_W)),
                  _resident(wsp.shape), _resident(bsp.shape), _resident(hmask.shape)],
        out_specs=[tok(HW)] * 3 + [tok(2 * HW)] * 3 + [tok(SG_W), tok(HW)],
        out_shape=[jax.ShapeDtypeStruct((n_tok, HW), BF16)] * 3
        + [jax.ShapeDtypeStruct((n_tok, 2 * HW), BF16)] * 3
        + [jax.ShapeDtypeStruct((n_tok, SG_W), BF16), jax.ShapeDtypeStruct((n_tok, HW), BF16)],
        compiler_params=_cparams("parallel"),
        name="proj",
    )(x2, w_in.astype(BF16), kvm, row(sg_ln_g), row(sg_ln_b), wsp, bsp, hmask)

    outs, lses = [], []
    for g, (dil, q, kv) in enumerate(zip(DILATIONS, (q1, q2, q3), (kv1, kv2, kv3))):
        kernel, rows, nblk = _make_attn_kernel(dil)
        length = seq // dil
        blk = lambda b, j: (b, j, 0)
        o_g, lse_g = pl.pallas_call(
            kernel,
            grid=(bsz, seq // SUPER),
            in_specs=[pl.BlockSpec((1, rows, dil * HW), blk),
                      pl.BlockSpec((1, rows, dil * 2 * HW), blk),
                      pl.BlockSpec((1, Q_BLOCK, dil * 2 * HW),
                                   lambda b, j, nblk=nblk: (b, jnp.maximum(j * nblk - 1, 0), 0)),
                      _resident((HEADS, Q_BLOCK, 2 * Q_BLOCK)), _resident(hmask.shape)],
            out_specs=[pl.BlockSpec((1, rows, dil * HW), blk)] * 2,
            out_shape=[jax.ShapeDtypeStruct((bsz, length, dil * HW), F32)] * 2,
            compiler_params=_cparams("parallel", "parallel"),
            name=f"attn_d{dil}",
        )(q.reshape(bsz, length, dil * HW), kv.reshape(bsz, length, dil * 2 * HW),
          kv.reshape(bsz, length, dil * 2 * HW), band_bias[g], hmask)
        outs.append(o_g.reshape(n_tok, HW))
        lses.append(lse_g.reshape(n_tok, HW))

    h1 = pl.pallas_call(
        _merge_kernel,
        grid=(n_tiles,),
        in_specs=[tok(D_MODEL)] + [tok(HW)] * 6 + [tok(SG_W), tok(HW),
                  _resident((D_MODEL, 3 * D_MODEL)), _resident((1, 3 * D_MODEL)),
                  _resident((HW, D_MODEL)), _resident((SG_W, D_MODEL)), _resident((HW, D_MODEL)),
                  _resident((D_MODEL, D_MODEL)), _resident((1, D_MODEL)), _resident((1, D_MODEL))],
        out_specs=tok(D_MODEL),
        out_shape=jax.ShapeDtypeStruct((n_tok, D_MODEL), F32),
        compiler_params=_cparams("parallel"),
        name="merge",
    )(x2, *outs, *lses, ob, oc, w_gate.astype(BF16), row(b_gate), w_proj_a.astype(BF16),
      w_proj_b.astype(BF16), w_proj_c.astype(BF16), w_out.astype(BF16), row(ln1_g), row(ln1_b))

    h2 = pl.pallas_call(
        functools.partial(_ffn_kernel, tiles_per_seq=tiles_per_seq),
        grid=(n_tiles,),
        in_specs=[tok(D_MODEL), _resident((D_MODEL, 2 * D_FF)), _resident(conv_w.shape),
                  _resident((1, D_FF)), _resident((D_FF, D_MODEL)),
                  _resident((1, D_MODEL)), _resident((1, D_MODEL))],
        out_specs=tok(D_MODEL),
        out_shape=jax.ShapeDtypeStruct((n_tok, D_MODEL), F32),
        scratch_shapes=[pltpu.VMEM((CARRY_ROWS + tm, FF_CHUNK), F32),
                        pltpu.VMEM((D_FF // FF_CHUNK, CARRY_ROWS, FF_CHUNK), F32),
                        pltpu.VMEM((tm, D_FF), BF16)],
        compiler_params=_cparams("arbitrary"),
        name="ffn",
    )(h1, w_ffn_up.astype(BF16), conv_w, row(conv_b), w_ffn_down.astype(BF16),
      row(ln2_g), row(ln2_b))
    return h2.reshape(bsz, seq, D_MODEL)


def kernel(x, mem, rel_bias, w_in, w_mem_kv, sg_ln_g, sg_ln_b, w_spatial, b_spatial, w_proj_a,
           w_proj_b, w_proj_c, w_gate, b_gate, w_out, ln1_g, ln1_b, w_ffn_up, conv_w, conv_b,
           w_ffn_down, ln2_g, ln2_b):
    band_bias = _band_bias(rel_bias)
    lane = jnp.arange(HW)[None, None, :] // HEAD_DIM
    hmask = (lane == jnp.arange(HEADS)[:, None, None]).astype(BF16)
    h = x
    for l in range(w_in.shape[0]):
        h = _layer(h, mem, band_bias, hmask, w_in[l], w_mem_kv[l], sg_ln_g[l], sg_ln_b[l],
                   w_spatial[l], b_spatial[l], w_proj_a[l], w_proj_b[l], w_proj_c[l], w_gate[l],
                   b_gate[l], w_out[l], ln1_g[l], ln1_b[l], w_ffn_up[l], conv_w[l], conv_b[l],
                   w_ffn_down[l], ln2_g[l], ln2_b[l])
    return h
```

```python
import functools
import math

import jax
import jax.numpy as jnp
from jax import lax
from jax.experimental import pallas as pl
from jax.experimental.pallas import tpu as pltpu

F32 = jnp.float32
BF16 = jnp.bfloat16

LANES = 128
D_MODEL = 1024
HEAD_DIM = 64
DILATIONS = (1, 4, 16)
N_GROUPS = 3
HEADS = 4
HW = HEADS * HEAD_DIM
ATTN_W = N_GROUPS * HW
WINDOW_STEPS = 128
Q_BLOCK = 128
SG_CHUNK = 128
SG_W = 512
MEM_LEN = 256
IN_W = 3 * ATTN_W + 2 * SG_W + HW
D_FF = 2816
N_BUCKETS = 32
MAX_DISTANCE = 2048
LN_EPS = 1e-5
DEPTH = 1
ALPHA = (2 * DEPTH) ** 0.25
ATTN_SCALE = HEAD_DIM ** -0.5
NEG = -1e30

TOKEN_TILE = 512
SUPER = Q_BLOCK * DILATIONS[-1]
FF_CHUNK = 256
CARRY_ROWS = 8
VMEM_LIMIT = 56 * 1024 * 1024

_NT = (((1,), (1,)), ((), ()))


def _gelu(x):
    return 0.5 * x * (1.0 + jnp.tanh(math.sqrt(2.0 / math.pi) * (x + 0.044715 * (x * x * x))))


def _layer_norm(x, g, b):
    mu = jnp.mean(x, axis=-1, keepdims=True)
    xc = x - mu
    var = jnp.mean(xc * xc, axis=-1, keepdims=True)
    return xc * lax.rsqrt(var + LN_EPS) * g + b


def _resident(shape):
    zeros = (0,) * len(shape)
    return pl.BlockSpec(shape, lambda *_: zeros, pipeline_mode=pl.Buffered(1))


def _packed_heads_attention(q, k, v, hm_ref, bias):
    m_rows = q.shape[0]
    qs = jnp.concatenate([q * hm_ref[h] for h in range(HEADS)], axis=0)
    s = lax.dot_general(qs, k, _NT, preferred_element_type=F32)
    if bias is not None:
        s = s + bias
    m = jnp.max(s, axis=-1, keepdims=True)
    p = jnp.exp(s - m)
    l = jnp.sum(p, axis=-1, keepdims=True)
    pv = jnp.dot(p.astype(BF16), v, preferred_element_type=F32) * (1.0 / l)
    lse = m + jnp.log(l)
    lane_head = lax.broadcasted_iota(jnp.int32, (m_rows, HW), 1) // HEAD_DIM
    out = pv[:m_rows]
    lse_out = jnp.broadcast_to(lse[:m_rows], (m_rows, HW))
    for h in range(1, HEADS):
        rows = slice(h * m_rows, (h + 1) * m_rows)
        out = jnp.where(lane_head == h, pv[rows], out)
        lse_out = jnp.where(lane_head == h, lse[rows], lse_out)
    return out, lse_out


def _memkv_kernel(mem_ref, w_ref, o_ref):
    o_ref[0] = jnp.dot(mem_ref[0].astype(BF16), w_ref[...],
                       preferred_element_type=F32).astype(BF16)


def _band_bias_kernel(f_ref, o_ref):
    for h in range(HEADS):
        rep = jnp.broadcast_to(f_ref[0, h], (Q_BLOCK, 2 * Q_BLOCK))
        o_ref[0, h * Q_BLOCK:(h + 1) * Q_BLOCK] = pltpu.roll(rep, 0, 1, stride=1, stride_axis=0)


def _proj_kernel(x_ref, win_ref, kvm_ref, lng_ref, lnb_ref, wsp_ref, bsp_ref, hm_ref,
                 q1_ref, q2_ref, q3_ref, kv1_ref, kv2_ref, kv3_ref, ob_ref, oc_ref, de_buf):
    xb = x_ref[0].astype(BF16)
    tm = xb.shape[0]

    def proj(lo, width):
        return jnp.dot(xb, win_ref[:, lo:lo + width], preferred_element_type=F32)

    def put(ref, lane0, val, dil):
        if dil == 1:
            ref[0, :, lane0:lane0 + HW] = val.astype(BF16)
            return
        stride_cols = ref.shape[2] // dil
        for c in range(HW // LANES):
            de_buf[...] = val[:, c * LANES:(c + 1) * LANES]
            for r in range(dil):
                piece = de_buf[pl.ds(r, tm // dil, stride=dil), :]
                col = r * stride_cols + lane0 + c * LANES
                ref[0, :, col:col + LANES] = piece.astype(BF16)

    for g, (q_ref, kv_ref) in enumerate(((q1_ref, kv1_ref), (q2_ref, kv2_ref), (q3_ref, kv3_ref))):
        put(q_ref, 0, proj(g * HW, HW) * ATTN_SCALE, DILATIONS[g])
        put(kv_ref, 0, proj(ATTN_W + g * HW, HW), DILATIONS[g])
        put(kv_ref, HW, proj(2 * ATTN_W + g * HW, HW), DILATIONS[g])

    u = _gelu(proj(3 * ATTN_W, SG_W))
    vn = _layer_norm(_gelu(proj(3 * ATTN_W + SG_W, SG_W)), lng_ref[...], lnb_ref[...]).astype(BF16)
    row = lax.broadcasted_iota(jnp.int32, (2 * SG_CHUNK, SG_CHUNK), 0) & (SG_CHUNK - 1)
    col = lax.broadcasted_iota(jnp.int32, (2 * SG_CHUNK, SG_CHUNK), 1)
    low_half = lax.broadcasted_iota(jnp.int32, (SG_CHUNK, SG_CHUNK), 1) < HEAD_DIM
    for j in range(SG_W // SG_CHUNK):
        w_pair = jnp.where(row >= col, wsp_ref[j], 0.0).astype(BF16)
        for c in range(tm // SG_CHUNK):
            rows = slice(c * SG_CHUNK, (c + 1) * SG_CHUNK)
            cols = slice(j * SG_CHUNK, (j + 1) * SG_CHUNK)
            r = jnp.dot(w_pair, vn[rows, cols], preferred_element_type=F32)
            sv = jnp.where(low_half, r[:SG_CHUNK], r[SG_CHUNK:]) + bsp_ref[j]
            ob_ref[0, rows, cols] = (u[rows, cols] * sv).astype(BF16)

    qc = (proj(3 * ATTN_W + 2 * SG_W, HW) * ATTN_SCALE).astype(BF16)
    half = tm // 2
    for part in range(2):
        rows = slice(part * half, (part + 1) * half)
        oc, _ = _packed_heads_attention(qc[rows], kvm_ref[0, :, :HW], kvm_ref[0, :, HW:],
                                        hm_ref, None)
        oc_ref[0, rows, :] = oc.astype(BF16)


def _attn_kernel(q1_ref, q2_ref, q3_ref, kv1_ref, kv2_ref, kv3_ref, kp1_ref, kp2_ref, kp3_ref,
                 bias_ref, hm_ref, oa_ref, o_nat, lse_nat, bias_first):
    lane = lax.broadcasted_iota(jnp.int32, (HEADS * Q_BLOCK, 2 * Q_BLOCK), 1)
    no_prev = jnp.where(lane < Q_BLOCK, jnp.where(pl.program_id(1) == 0, NEG, 0.0), 0.0)
    for g in range(N_GROUPS):
        bias_first[g] = bias_ref[g] + no_prev

    groups = ((q1_ref, kv1_ref, kp1_ref), (q2_ref, kv2_ref, kp2_ref), (q3_ref, kv3_ref, kp3_ref))
    for g, (q_ref, kv_ref, kvp_ref) in enumerate(groups):
        dil = DILATIONS[g]
        nblk = SUPER // dil // Q_BLOCK
        for r in range(dil):
            kc = slice(r * 2 * HW, r * 2 * HW + HW)
            vc = slice(r * 2 * HW + HW, (r + 1) * 2 * HW)
            qc = slice(r * HW, (r + 1) * HW)
            for i in range(nblk):
                rq = slice(i * Q_BLOCK, (i + 1) * Q_BLOCK)
                q = q_ref[0, rq, qc]
                if i == 0:
                    k2 = jnp.concatenate([kvp_ref[0, :, kc], kv_ref[0, rq, kc]], axis=0)
                    v2 = jnp.concatenate([kvp_ref[0, :, vc], kv_ref[0, rq, vc]], axis=0)
                    bias = bias_first[g]
                else:
                    rk = slice((i - 1) * Q_BLOCK, (i + 1) * Q_BLOCK)
                    k2 = kv_ref[0, rk, kc]
                    v2 = kv_ref[0, rk, vc]
                    bias = bias_ref[g]
                out, lse = _packed_heads_attention(q, k2, v2, hm_ref, bias)
                nat = pl.ds(i * Q_BLOCK * dil + r, Q_BLOCK, stride=dil) if dil > 1 else rq
                for c in range(HW // LANES):
                    o_nat[g, c, nat, :] = out[:, c * LANES:(c + 1) * LANES]
                    lse_nat[g, c, nat, :] = lse[:, c * LANES:(c + 1) * LANES]

    for t in range(SUPER // TOKEN_TILE):
        rows = slice(t * TOKEN_TILE, (t + 1) * TOKEN_TILE)
        for c in range(HW // LANES):
            l1, l2, l3 = lse_nat[0, c, rows, :], lse_nat[1, c, rows, :], lse_nat[2, c, rows, :]
            m = jnp.maximum(jnp.maximum(l1, l2), l3)
            e1, e2, e3 = jnp.exp(l1 - m), jnp.exp(l2 - m), jnp.exp(l3 - m)
            oa = e1 * o_nat[0, c, rows, :] + e2 * o_nat[1, c, rows, :] + e3 * o_nat[2, c, rows, :]
            oa_ref[0, rows, c * LANES:(c + 1) * LANES] = (oa * (1.0 / (e1 + e2 + e3))).astype(BF16)


def _merge_kernel(x_ref, oa_ref, ob_ref, oc_ref, wg_ref, bg_ref, wpa_ref, wpb_ref, wpc_ref,
                  wout_ref, lng_ref, lnb_ref, h_ref):
    x = x_ref[...]
    xb = x.astype(BF16)

    def gate(k):
        z = jnp.dot(xb, wg_ref[:, k * D_MODEL:(k + 1) * D_MODEL], preferred_element_type=F32)
        return jax.nn.sigmoid(z + bg_ref[:, k * D_MODEL:(k + 1) * D_MODEL])

    merged = gate(0) * jnp.dot(oa_ref[...], wpa_ref[...], preferred_element_type=F32)
    merged += gate(1) * jnp.dot(ob_ref[...], wpb_ref[...], preferred_element_type=F32)
    merged += gate(2) * jnp.dot(oc_ref[...], wpc_ref[...], preferred_element_type=F32)
    y = jnp.dot(merged.astype(BF16), wout_ref[...], preferred_element_type=F32)
    h_ref[...] = _layer_norm(ALPHA * x + y, lng_ref[...], lnb_ref[...])


def _ffn_kernel(h_ref, wup_ref, cw_ref, cb_ref, wdn_ref, lng_ref, lnb_ref, o_ref,
                a_buf, carry, g_buf, *, tiles_per_seq):
    tm = h_ref.shape[0]
    n_chunks = D_FF // FF_CHUNK

    @pl.when(pl.program_id(0) % tiles_per_seq == 0)
    def _():
        carry[...] = jnp.zeros(carry.shape, F32)

    h = h_ref[...]
    hb = h.astype(BF16)
    for c in range(n_chunks):
        cols = slice(c * FF_CHUNK, (c + 1) * FF_CHUNK)
        a = jnp.dot(hb, wup_ref[:, cols], preferred_element_type=F32)
        gate = jnp.dot(hb, wup_ref[:, D_FF + c * FF_CHUNK:D_FF + (c + 1) * FF_CHUNK],
                       preferred_element_type=F32)
        a_buf[:CARRY_ROWS] = carry[c]
        a_buf[CARRY_ROWS:] = a
        carry[c] = a[tm - CARRY_ROWS:]
        conv = (cw_ref[0:1, cols] * a_buf[CARRY_ROWS - 2:CARRY_ROWS - 2 + tm]
                + cw_ref[1:2, cols] * a_buf[CARRY_ROWS - 1:CARRY_ROWS - 1 + tm]
                + cw_ref[2:3, cols] * a + cb_ref[:, cols])
        g_buf[:, cols] = (_gelu(conv) * gate).astype(BF16)
    y = jnp.dot(g_buf[...], wdn_ref[...], preferred_element_type=F32)
    o_ref[...] = _layer_norm(ALPHA * h + y, lng_ref[...], lnb_ref[...])


def _t5_bucket(dist):
    max_exact = N_BUCKETS // 2
    n = jnp.maximum(dist, 1).astype(jnp.float32)
    large = max_exact + (jnp.log(n / max_exact) / math.log(MAX_DISTANCE / max_exact)
                         * (N_BUCKETS - max_exact)).astype(jnp.int32)
    large = jnp.minimum(large, N_BUCKETS - 1)
    return jnp.where(dist < max_exact, dist, large)


def _offset_bias(rel_bias):
    steps = WINDOW_STEPS - jnp.arange(2 * Q_BLOCK)
    rows = []
    for g, dil in enumerate(DILATIONS):
        bucket = _t5_bucket(jnp.maximum(steps, 0) * dil)
        bias = rel_bias[:, g * HEADS:(g + 1) * HEADS][bucket].astype(F32)
        rows.append(jnp.where(steps[:, None] >= 0, bias, NEG).T)
    return jnp.stack(rows, 0)[:, :, None, :]


def _cparams(*sem):
    return pltpu.CompilerParams(dimension_semantics=sem, vmem_limit_bytes=VMEM_LIMIT)


def _layer(x, mem, band_bias, hmask, w_in, w_mem_kv, sg_ln_g, sg_ln_b, w_spatial, b_spatial,
           w_proj_a, w_proj_b, w_proj_c, w_gate, b_gate, w_out, ln1_g, ln1_b,
           w_ffn_up, conv_w, conv_b, w_ffn_down, ln2_g, ln2_b):
    bsz, seq, _ = x.shape
    n_tok = bsz * seq
    tm = TOKEN_TILE
    assert seq % SUPER == 0 and SUPER % tm == 0 and tm % (SG_CHUNK * 2) == 0
    assert D_FF % FF_CHUNK == 0 and tm % (16 * DILATIONS[-1]) == 0
    n_tiles = n_tok // tm
    tiles_per_seq = seq // tm
    row = lambda v: v.reshape(1, -1)
    tok = lambda width: pl.BlockSpec((tm, width), lambda i: (i, 0))
    tile3 = lambda rows, width: pl.BlockSpec((1, rows, width), lambda b, t: (b, t, 0))

    kvm = pl.pallas_call(
        _memkv_kernel,
        grid=(bsz,),
        in_specs=[pl.BlockSpec((1, MEM_LEN, D_MODEL), lambda b: (b, 0, 0)),
                  _resident((D_MODEL, 2 * HW))],
        out_specs=pl.BlockSpec((1, MEM_LEN, 2 * HW), lambda b: (b, 0, 0)),
        out_shape=jax.ShapeDtypeStruct((bsz, MEM_LEN, 2 * HW), BF16),
        compiler_params=_cparams("parallel"),
        name="memkv",
    )(mem, w_mem_kv.astype(BF16))

    wsp = w_spatial.reshape(SG_W // SG_CHUNK, 2 * SG_CHUNK, SG_CHUNK)
    bsp = jnp.repeat(b_spatial.reshape(SG_W // SG_CHUNK, 2, SG_CHUNK).transpose(0, 2, 1),
                     HEAD_DIM, axis=-1)
    qkv_shapes = ([jax.ShapeDtypeStruct((bsz, seq // d, d * HW), BF16) for d in DILATIONS]
                  + [jax.ShapeDtypeStruct((bsz, seq // d, d * 2 * HW), BF16) for d in DILATIONS])
    qkv_specs = ([tile3(tm // d, d * HW) for d in DILATIONS]
                 + [tile3(tm // d, d * 2 * HW) for d in DILATIONS])
    q1, q2, q3, kv1, kv2, kv3, ob, oc = pl.pallas_call(
        _proj_kernel,
        grid=(bsz, tiles_per_seq),
        in_specs=[tile3(tm, D_MODEL), _resident((D_MODEL, IN_W)),
                  pl.BlockSpec((1, MEM_LEN, 2 * HW), lambda b, t: (b, 0, 0)),
                  _resident((1, SG_W)), _resident((1, SG_W)),
                  _resident(wsp.shape), _resident(bsp.shape), _resident(hmask.shape)],
        out_specs=qkv_specs + [tile3(tm, SG_W), tile3(tm, HW)],
        out_shape=qkv_shapes + [jax.ShapeDtypeStruct((bsz, seq, SG_W), BF16),
                                jax.ShapeDtypeStruct((bsz, seq, HW), BF16)],
        scratch_shapes=[pltpu.VMEM((tm, LANES), F32)],
        compiler_params=_cparams("parallel", "parallel"),
        name="proj",
    )(x, w_in.astype(BF16), kvm, row(sg_ln_g), row(sg_ln_b), wsp, bsp, hmask)

    n_super = seq // SUPER
    super3 = lambda d, width: pl.BlockSpec((1, SUPER // d, d * width), lambda b, j: (b, j, 0))

    def prev3(d):
        nblk = SUPER // d // Q_BLOCK
        return pl.BlockSpec((1, Q_BLOCK, d * 2 * HW),
                            lambda b, j: (b, jnp.maximum(j * nblk - 1, 0), 0))

    oa = pl.pallas_call(
        _attn_kernel,
        grid=(bsz, n_super),
        in_specs=[super3(d, HW) for d in DILATIONS] + [super3(d, 2 * HW) for d in DILATIONS]
        + [prev3(d) for d in DILATIONS] + [_resident(band_bias.shape), _resident(hmask.shape)],
        out_specs=pl.BlockSpec((1, SUPER, HW), lambda b, j: (b, j, 0)),
        out_shape=jax.ShapeDtypeStruct((bsz, seq, HW), BF16),
        scratch_shapes=[pltpu.VMEM((N_GROUPS, HW // LANES, SUPER, LANES), F32),
                        pltpu.VMEM((N_GROUPS, HW // LANES, SUPER, LANES), F32),
                        pltpu.VMEM((N_GROUPS, HEADS * Q_BLOCK, 2 * Q_BLOCK), F32)],
        compiler_params=_cparams("parallel", "parallel"),
        name="attn",
    )(q1, q2, q3, kv1, kv2, kv3, kv1, kv2, kv3, band_bias, hmask)

    h1 = pl.pallas_call(
        _merge_kernel,
        grid=(n_tiles,),
        in_specs=[tok(D_MODEL), tok(HW), tok(SG_W), tok(HW),
                  _resident((D_MODEL, 3 * D_MODEL)), _resident((1, 3 * D_MODEL)),
                  _resident((HW, D_MODEL)), _resident((SG_W, D_MODEL)), _resident((HW, D_MODEL)),
                  _resident((D_MODEL, D_MODEL)), _resident((1, D_MODEL)), _resident((1, D_MODEL))],
        out_specs=tok(D_MODEL),
        out_shape=jax.ShapeDtypeStruct((n_tok, D_MODEL), F32),
        compiler_params=_cparams("parallel"),
        name="merge",
    )(x.reshape(n_tok, D_MODEL), oa.reshape(n_tok, HW), ob.reshape(n_tok, SG_W),
      oc.reshape(n_tok, HW), w_gate.astype(BF16), row(b_gate), w_proj_a.astype(BF16),
      w_proj_b.astype(BF16), w_proj_c.astype(BF16), w_out.astype(BF16), row(ln1_g), row(ln1_b))

    h2 = pl.pallas_call(
        functools.partial(_ffn_kernel, tiles_per_seq=tiles_per_seq),
        grid=(n_tiles,),
        in_specs=[tok(D_MODEL), _resident((D_MODEL, 2 * D_FF)), _resident(conv_w.shape),
                  _resident((1, D_FF)), _resident((D_FF, D_MODEL)),
                  _resident((1, D_MODEL)), _resident((1, D_MODEL))],
        out_specs=tok(D_MODEL),
        out_shape=jax.ShapeDtypeStruct((n_tok, D_MODEL), F32),
        scratch_shapes=[pltpu.VMEM((CARRY_ROWS + tm, FF_CHUNK), F32),
                        pltpu.VMEM((D_FF // FF_CHUNK, CARRY_ROWS, FF_CHUNK), F32),
                        pltpu.VMEM((tm, D_FF), BF16)],
        compiler_params=_cparams("arbitrary"),
        name="ffn",
    )(h1, w_ffn_up.astype(BF16), conv_w, row(conv_b), w_ffn_down.astype(BF16),
      row(ln2_g), row(ln2_b))
    return h2.reshape(bsz, seq, D_MODEL)


def kernel(x, mem, rel_bias, w_in, w_mem_kv, sg_ln_g, sg_ln_b, w_spatial, b_spatial, w_proj_a,
           w_proj_b, w_proj_c, w_gate, b_gate, w_out, ln1_g, ln1_b, w_ffn_up, conv_w, conv_b,
           w_ffn_down, ln2_g, ln2_b):
    band_bias = pl.pallas_call(
        _band_bias_kernel,
        grid=(N_GROUPS,),
        in_specs=[pl.BlockSpec((1, HEADS, 1, 2 * Q_BLOCK), lambda g: (g, 0, 0, 0))],
        out_specs=pl.BlockSpec((1, HEADS * Q_BLOCK, 2 * Q_BLOCK), lambda g: (g, 0, 0)),
        out_shape=jax.ShapeDtypeStruct((N_GROUPS, HEADS * Q_BLOCK, 2 * Q_BLOCK), F32),
        name="band_bias",
    )(_offset_bias(rel_bias))
    lane = jnp.arange(HW)[None, None, :] // HEAD_DIM
    hmask = (lane == jnp.arange(HEADS)[:, None, None]).astype(BF16)
    h = x
    for l in range(w_in.shape[0]):
        h = _layer(h, mem, band_bias, hmask, w_in[l], w_mem_kv[l], sg_ln_g[l], sg_ln_b[l],
                   w_spatial[l], b_spatial[l], w_proj_a[l], w_proj_b[l], w_proj_c[l], w_gate[l],
                   b_gate[l], w_out[l], ln1_g[l], ln1_b[l], w_ffn_up[l], conv_w[l], conv_b[l],
                   w_ffn_down[l], ln2_g[l], ln2_b[l])
    return h
```

```python
import functools
import math

import jax
import jax.numpy as jnp
from jax import lax
from jax.experimental import pallas as pl
from jax.experimental.pallas import tpu as pltpu

F32 = jnp.float32
BF16 = jnp.bfloat16

LANES = 128
D_MODEL = 1024
HEAD_DIM = 64
DILATIONS = (1, 4, 16)
N_GROUPS = 3
HEADS = 4
HW = HEADS * HEAD_DIM
ATTN_W = N_GROUPS * HW
WINDOW_STEPS = 128
Q_BLOCK = 128
SG_CHUNK = 128
SG_W = 512
MEM_LEN = 256
IN_W = 3 * ATTN_W + 2 * SG_W + HW
D_FF = 2816
N_BUCKETS = 32
MAX_DISTANCE = 2048
LN_EPS = 1e-5
DEPTH = 1
ALPHA = (2 * DEPTH) ** 0.25
ATTN_SCALE = HEAD_DIM ** -0.5
LOG2E = math.log2(math.e)
Q_SCALE = ATTN_SCALE * LOG2E
NEG = -1e30

TOKEN_TILE = 512
SUPER = Q_BLOCK * DILATIONS[-1]
COMBINE_ROWS = 256
FF_CHUNK = 256
CARRY_ROWS = 8
VMEM_LIMIT = 56 * 1024 * 1024

_NT = (((1,), (1,)), ((), ()))


def _gelu(x):
    return 0.5 * x * (1.0 + jnp.tanh(math.sqrt(2.0 / math.pi) * (x + 0.044715 * (x * x * x))))


def _layer_norm(x, g, b):
    mu = jnp.mean(x, axis=-1, keepdims=True)
    xc = x - mu
    var = jnp.mean(xc * xc, axis=-1, keepdims=True)
    return xc * lax.rsqrt(var + LN_EPS) * g + b


def _resident(shape):
    zeros = (0,) * len(shape)
    return pl.BlockSpec(shape, lambda *_: zeros, pipeline_mode=pl.Buffered(1))


def _head_dense(cols, lane_head):
    out = jnp.broadcast_to(cols[0], lane_head.shape)
    for h in range(1, HEADS):
        out = jnp.where(lane_head == h, cols[h], out)
    return out


def _mask_heads(v, hm_ref):
    return [v * hm_ref[h] for h in range(HEADS)]


def _packed_heads_attention(q, k, vm, hm_ref, bias_fn):
    m_rows = q.shape[0]
    qs = jnp.concatenate([q * hm_ref[h] for h in range(HEADS)], axis=0)
    s = lax.dot_general(qs, k, _NT, preferred_element_type=F32)
    ps, ms, ls = [], [], []
    for h in range(HEADS):
        sh = s[h * m_rows:(h + 1) * m_rows]
        bias = bias_fn(h)
        if bias is not None:
            sh = sh + bias
        m = jnp.max(sh, axis=-1, keepdims=True)
        p = jnp.exp2(sh - m)
        ps.append(p.astype(BF16))
        ms.append(m)
        ls.append(jnp.sum(p, axis=-1, keepdims=True))
    pv = jnp.dot(jnp.concatenate(ps, axis=1), vm, preferred_element_type=F32)
    lane_head = lax.broadcasted_iota(jnp.int32, (m_rows, HW), 1) // HEAD_DIM
    return pv, _head_dense(ms, lane_head), _head_dense(ls, lane_head)


def _memkv_kernel(mem_ref, w_ref, o_ref):
    o_ref[0] = jnp.dot(mem_ref[0].astype(BF16), w_ref[...],
                       preferred_element_type=F32).astype(BF16)


def _band_bias_kernel(f_ref, o_ref):
    for h in range(HEADS):
        rep = jnp.broadcast_to(f_ref[0, h] * LOG2E, (Q_BLOCK, 2 * Q_BLOCK))
        o_ref[0, h * Q_BLOCK:(h + 1) * Q_BLOCK] = pltpu.roll(rep, 0, 1, stride=1, stride_axis=0)


def _proj_kernel(x_ref, win_ref, kvm_ref, lng_ref, lnb_ref, wsp_ref, bsp_ref, hm_ref,
                 q1_ref, q2_ref, q3_ref, kv1_ref, kv2_ref, kv3_ref, ob_ref, oc_ref, de_buf):
    xb = x_ref[0].astype(BF16)
    tm = xb.shape[0]

    def proj(lo, width):
        return jnp.dot(xb, win_ref[:, lo:lo + width], preferred_element_type=F32)

    def put(ref, lane0, val, dil):
        if dil == 1:
            ref[0, :, lane0:lane0 + HW] = val.astype(BF16)
            return
        stride_cols = ref.shape[2] // dil
        for c in range(HW // LANES):
            de_buf[...] = val[:, c * LANES:(c + 1) * LANES]
            for r in range(dil):
                piece = de_buf[pl.ds(r, tm // dil, stride=dil), :]
                col = r * stride_cols + lane0 + c * LANES
                ref[0, :, col:col + LANES] = piece.astype(BF16)

    for g, (q_ref, kv_ref) in enumerate(((q1_ref, kv1_ref), (q2_ref, kv2_ref), (q3_ref, kv3_ref))):
        put(q_ref, 0, proj(g * HW, HW) * Q_SCALE, DILATIONS[g])
        put(kv_ref, 0, proj(ATTN_W + g * HW, HW), DILATIONS[g])
        put(kv_ref, HW, proj(2 * ATTN_W + g * HW, HW), DILATIONS[g])

    u = _gelu(proj(3 * ATTN_W, SG_W))
    vn = _layer_norm(_gelu(proj(3 * ATTN_W + SG_W, SG_W)), lng_ref[...], lnb_ref[...]).astype(BF16)
    row = lax.broadcasted_iota(jnp.int32, (2 * SG_CHUNK, SG_CHUNK), 0) & (SG_CHUNK - 1)
    col = lax.broadcasted_iota(jnp.int32, (2 * SG_CHUNK, SG_CHUNK), 1)
    low_half = lax.broadcasted_iota(jnp.int32, (SG_CHUNK, SG_CHUNK), 1) < HEAD_DIM
    n_chunks = tm // SG_CHUNK
    for j in range(SG_W // SG_CHUNK):
        cols = slice(j * SG_CHUNK, (j + 1) * SG_CHUNK)
        w_pair = jnp.where(row >= col, wsp_ref[j], 0.0).astype(BF16)
        rhs = jnp.concatenate([vn[c * SG_CHUNK:(c + 1) * SG_CHUNK, cols] for c in range(n_chunks)],
                              axis=1)
        r = jnp.dot(w_pair, rhs, preferred_element_type=F32)
        for c in range(n_chunks):
            rows = slice(c * SG_CHUNK, (c + 1) * SG_CHUNK)
            sv = jnp.where(low_half, r[:SG_CHUNK, rows], r[SG_CHUNK:, rows]) + bsp_ref[j]
            ob_ref[0, rows, cols] = (u[rows, cols] * sv).astype(BF16)

    qc = (proj(3 * ATTN_W + 2 * SG_W, HW) * Q_SCALE).astype(BF16)
    vm = jnp.concatenate(_mask_heads(kvm_ref[0, :, HW:], hm_ref), axis=0)
    for c in range(tm // Q_BLOCK):
        rows = slice(c * Q_BLOCK, (c + 1) * Q_BLOCK)
        pv, _, l = _packed_heads_attention(qc[rows], kvm_ref[0, :, :HW], vm, hm_ref,
                                           lambda h: None)
        oc_ref[0, rows, :] = (pv * (1.0 / l)).astype(BF16)


def _attn_kernel(q1_ref, q2_ref, q3_ref, kv1_ref, kv2_ref, kv3_ref, kp1_ref, kp2_ref, kp3_ref,
                 bias_ref, hm_ref, oa_ref, pv_nat, m_nat, l_nat, bias_first):
    lane = lax.broadcasted_iota(jnp.int32, (HEADS * Q_BLOCK, 2 * Q_BLOCK), 1)
    no_prev = jnp.where(lane < Q_BLOCK, jnp.where(pl.program_id(1) == 0, NEG, 0.0), 0.0)
    for g in range(N_GROUPS):
        bias_first[g] = bias_ref[g] + no_prev

    groups = ((q1_ref, kv1_ref, kp1_ref), (q2_ref, kv2_ref, kp2_ref), (q3_ref, kv3_ref, kp3_ref))
    for g, (q_ref, kv_ref, kvp_ref) in enumerate(groups):
        dil = DILATIONS[g]
        nblk = SUPER // dil // Q_BLOCK
        for r in range(dil):
            kc = slice(r * 2 * HW, r * 2 * HW + HW)
            vc = slice(r * 2 * HW + HW, (r + 1) * 2 * HW)
            qc = slice(r * HW, (r + 1) * HW)
            k_prev = kvp_ref[0, :, kc]
            vm_prev = _mask_heads(kvp_ref[0, :, vc], hm_ref)
            for i in range(nblk):
                rq = slice(i * Q_BLOCK, (i + 1) * Q_BLOCK)
                k_cur = kv_ref[0, rq, kc]
                vm_cur = _mask_heads(kv_ref[0, rq, vc], hm_ref)
                k2 = jnp.concatenate([k_prev, k_cur], axis=0)
                vm = jnp.concatenate([blk[h] for h in range(HEADS) for blk in (vm_prev, vm_cur)],
                                     axis=0)
                table = bias_first if i == 0 else bias_ref
                pv, m, l = _packed_heads_attention(
                    q_ref[0, rq, qc], k2, vm, hm_ref,
                    lambda h, table=table: table[g, h * Q_BLOCK:(h + 1) * Q_BLOCK, :])
                nat = pl.ds(i * Q_BLOCK * dil + r, Q_BLOCK, stride=dil) if dil > 1 else rq
                for c in range(HW // LANES):
                    lanes = slice(c * LANES, (c + 1) * LANES)
                    pv_nat[g, c, nat, :] = pv[:, lanes]
                    m_nat[g, c, nat, :] = m[:, lanes]
                    l_nat[g, c, nat, :] = l[:, lanes]
                k_prev, vm_prev = k_cur, vm_cur

    for t in range(SUPER // COMBINE_ROWS):
        rows = slice(t * COMBINE_ROWS, (t + 1) * COMBINE_ROWS)
        for c in range(HW // LANES):
            m1, m2, m3 = m_nat[0, c, rows, :], m_nat[1, c, rows, :], m_nat[2, c, rows, :]
            m = jnp.maximum(jnp.maximum(m1, m2), m3)
            e1, e2, e3 = jnp.exp2(m1 - m), jnp.exp2(m2 - m), jnp.exp2(m3 - m)
            num = e1 * pv_nat[0, c, rows, :] + e2 * pv_nat[1, c, rows, :] + e3 * pv_nat[2, c, rows, :]
            den = e1 * l_nat[0, c, rows, :] + e2 * l_nat[1, c, rows, :] + e3 * l_nat[2, c, rows, :]
            oa_ref[0, rows, c * LANES:(c + 1) * LANES] = (num * (1.0 / den)).astype(BF16)


def _merge_kernel(x_ref, oa_ref, ob_ref, oc_ref, wg_ref, bg_ref, wpa_ref, wpb_ref, wpc_ref,
                  wout_ref, lng_ref, lnb_ref, h_ref):
    x = x_ref[...]
    xb = x.astype(BF16)

    def gated(k, o_ref, wp_ref):
        cols = slice(k * D_MODEL, (k + 1) * D_MODEL)
        t = jnp.tanh(jnp.dot(xb, wg_ref[:, cols], preferred_element_type=F32) + bg_ref[:, cols])
        half_p = jnp.dot(o_ref[...], wp_ref[...], preferred_element_type=F32)
        return half_p + half_p * t

    merged = gated(0, oa_ref, wpa_ref) + gated(1, ob_ref, wpb_ref) + gated(2, oc_ref, wpc_ref)
    y = jnp.dot(merged.astype(BF16), wout_ref[...], preferred_element_type=F32)
    h_ref[...] = _layer_norm(ALPHA * x + y, lng_ref[...], lnb_ref[...])


def _ffn_kernel(h_ref, wup_ref, cw_ref, cb_ref, wdn_ref, lng_ref, lnb_ref, o_ref,
                a_buf, carry, g_buf, *, tiles_per_seq):
    tm = h_ref.shape[0]
    n_chunks = D_FF // FF_CHUNK

    @pl.when(pl.program_id(0) % tiles_per_seq == 0)
    def _():
        carry[...] = jnp.zeros(carry.shape, F32)

    h = h_ref[...]
    hb = h.astype(BF16)
    for c in range(n_chunks):
        cols = slice(c * FF_CHUNK, (c + 1) * FF_CHUNK)
        a = jnp.dot(hb, wup_ref[:, cols], preferred_element_type=F32)
        gate = jnp.dot(hb, wup_ref[:, D_FF + c * FF_CHUNK:D_FF + (c + 1) * FF_CHUNK],
                       preferred_element_type=F32)
        a_buf[:CARRY_ROWS] = carry[c]
        a_buf[CARRY_ROWS:] = a
        carry[c] = a[tm - CARRY_ROWS:]
        conv = (cw_ref[0:1, cols] * a_buf[CARRY_ROWS - 2:CARRY_ROWS - 2 + tm]
                + cw_ref[1:2, cols] * a_buf[CARRY_ROWS - 1:CARRY_ROWS - 1 + tm]
                + cw_ref[2:3, cols] * a + cb_ref[:, cols])
        g_buf[:, cols] = (_gelu(conv) * gate).astype(BF16)
    y = jnp.dot(g_buf[...], wdn_ref[...], preferred_element_type=F32)
    o_ref[...] = _layer_norm(ALPHA * h + y, lng_ref[...], lnb_ref[...])


def _t5_bucket(dist):
    max_exact = N_BUCKETS // 2
    n = jnp.maximum(dist, 1).astype(jnp.float32)
    large = max_exact + (jnp.log(n / max_exact) / math.log(MAX_DISTANCE / max_exact)
                         * (N_BUCKETS - max_exact)).astype(jnp.int32)
    large = jnp.minimum(large, N_BUCKETS - 1)
    return jnp.where(dist < max_exact, dist, large)


def _offset_bias(rel_bias):
    steps = WINDOW_STEPS - jnp.arange(2 * Q_BLOCK)
    rows = []
    for g, dil in enumerate(DILATIONS):
        bucket = _t5_bucket(jnp.maximum(steps, 0) * dil)
        bias = rel_bias[:, g * HEADS:(g + 1) * HEADS][bucket].astype(F32)
        rows.append(jnp.where(steps[:, None] >= 0, bias, NEG).T)
    return jnp.stack(rows, 0)[:, :, None, :]


def _cparams(*sem):
    return pltpu.CompilerParams(dimension_semantics=sem, vmem_limit_bytes=VMEM_LIMIT)


def _layer(x, mem, band_bias, hmask, w_in, w_mem_kv, sg_ln_g, sg_ln_b, w_spatial, b_spatial,
           w_proj_a, w_proj_b, w_proj_c, w_gate, b_gate, w_out, ln1_g, ln1_b,
           w_ffn_up, conv_w, conv_b, w_ffn_down, ln2_g, ln2_b):
    bsz, seq, _ = x.shape
    n_tok = bsz * seq
    tm = TOKEN_TILE
    assert seq % SUPER == 0 and SUPER % tm == 0 and tm % (SG_CHUNK * 2) == 0
    assert D_FF % FF_CHUNK == 0 and tm % (16 * DILATIONS[-1]) == 0
    n_tiles = n_tok // tm
    tiles_per_seq = seq // tm
    row = lambda v: v.reshape(1, -1)
    tok = lambda width: pl.BlockSpec((tm, width), lambda i: (i, 0))
    tile3 = lambda rows, width: pl.BlockSpec((1, rows, width), lambda b, t: (b, t, 0))

    kvm = pl.pallas_call(
        _memkv_kernel,
        grid=(bsz,),
        in_specs=[pl.BlockSpec((1, MEM_LEN, D_MODEL), lambda b: (b, 0, 0)),
                  _resident((D_MODEL, 2 * HW))],
        out_specs=pl.BlockSpec((1, MEM_LEN, 2 * HW), lambda b: (b, 0, 0)),
        out_shape=jax.ShapeDtypeStruct((bsz, MEM_LEN, 2 * HW), BF16),
        compiler_params=_cparams("parallel"),
        name="memkv",
    )(mem, w_mem_kv.astype(BF16))

    wsp = w_spatial.reshape(SG_W // SG_CHUNK, 2 * SG_CHUNK, SG_CHUNK)
    bsp = jnp.repeat(b_spatial.reshape(SG_W // SG_CHUNK, 2, SG_CHUNK).transpose(0, 2, 1),
                     HEAD_DIM, axis=-1)
    qkv_shapes = ([jax.ShapeDtypeStruct((bsz, seq // d, d * HW), BF16) for d in DILATIONS]
                  + [jax.ShapeDtypeStruct((bsz, seq // d, d * 2 * HW), BF16) for d in DILATIONS])
    qkv_specs = ([tile3(tm // d, d * HW) for d in DILATIONS]
                 + [tile3(tm // d, d * 2 * HW) for d in DILATIONS])
    q1, q2, q3, kv1, kv2, kv3, ob, oc = pl.pallas_call(
        _proj_kernel,
        grid=(bsz, tiles_per_seq),
        in_specs=[tile3(tm, D_MODEL), _resident((D_MODEL, IN_W)),
                  pl.BlockSpec((1, MEM_LEN, 2 * HW), lambda b, t: (b, 0, 0)),
                  _resident((1, SG_W)), _resident((1, SG_W)),
                  _resident(wsp.shape), _resident(bsp.shape), _resident(hmask.shape)],
        out_specs=qkv_specs + [tile3(tm, SG_W), tile3(tm, HW)],
        out_shape=qkv_shapes + [jax.ShapeDtypeStruct((bsz, seq, SG_W), BF16),
                                jax.ShapeDtypeStruct((bsz, seq, HW), BF16)],
        scratch_shapes=[pltpu.VMEM((tm, LANES), F32)],
        compiler_params=_cparams("parallel", "parallel"),
        name="proj",
    )(x, w_in.astype(BF16), kvm, row(sg_ln_g), row(sg_ln_b), wsp, bsp, hmask)

    n_super = seq // SUPER
    super3 = lambda d, width: pl.BlockSpec((1, SUPER // d, d * width), lambda b, j: (b, j, 0))

    def prev3(d):
        nblk = SUPER // d // Q_BLOCK
        return pl.BlockSpec((1, Q_BLOCK, d * 2 * HW),
                            lambda b, j: (b, jnp.maximum(j * nblk - 1, 0), 0))

    oa = pl.pallas_call(
        _attn_kernel,
        grid=(bsz, n_super),
        in_specs=[super3(d, HW) for d in DILATIONS] + [super3(d, 2 * HW) for d in DILATIONS]
        + [prev3(d) for d in DILATIONS] + [_resident(band_bias.shape), _resident(hmask.shape)],
        out_specs=pl.BlockSpec((1, SUPER, HW), lambda b, j: (b, j, 0)),
        out_shape=jax.ShapeDtypeStruct((bsz, seq, HW), BF16),
        scratch_shapes=[pltpu.VMEM((N_GROUPS, HW // LANES, SUPER, LANES), F32)] * 3
        + [pltpu.VMEM((N_GROUPS, HEADS * Q_BLOCK, 2 * Q_BLOCK), F32)],
        compiler_params=_cparams("parallel", "parallel"),
        name="attn",
    )(q1, q2, q3, kv1, kv2, kv3, kv1, kv2, kv3, band_bias, hmask)

    h1 = pl.pallas_call(
        _merge_kernel,
        grid=(n_tiles,),
        in_specs=[tok(D_MODEL), tok(HW), tok(SG_W), tok(HW),
                  _resident((D_MODEL, 3 * D_MODEL)), _resident((1, 3 * D_MODEL)),
                  _resident((HW, D_MODEL)), _resident((SG_W, D_MODEL)), _resident((HW, D_MODEL)),
                  _resident((D_MODEL, D_MODEL)), _resident((1, D_MODEL)), _resident((1, D_MODEL))],
        out_specs=tok(D_MODEL),
        out_shape=jax.ShapeDtypeStruct((n_tok, D_MODEL), F32),
        compiler_params=_cparams("parallel"),
        name="merge",
    )(x.reshape(n_tok, D_MODEL), oa.reshape(n_tok, HW), ob.reshape(n_tok, SG_W),
      oc.reshape(n_tok, HW), (0.5 * w_gate).astype(BF16), row(0.5 * b_gate),
      (0.5 * w_proj_a).astype(BF16), (0.5 * w_proj_b).astype(BF16), (0.5 * w_proj_c).astype(BF16),
      w_out.astype(BF16), row(ln1_g), row(ln1_b))

    h2 = pl.pallas_call(
        functools.partial(_ffn_kernel, tiles_per_seq=tiles_per_seq),
        grid=(n_tiles,),
        in_specs=[tok(D_MODEL), _resident((D_MODEL, 2 * D_FF)), _resident(conv_w.shape),
                  _resident((1, D_FF)), _resident((D_FF, D_MODEL)),
                  _resident((1, D_MODEL)), _resident((1, D_MODEL))],
        out_specs=tok(D_MODEL),
        out_shape=jax.ShapeDtypeStruct((n_tok, D_MODEL), F32),
        scratch_shapes=[pltpu.VMEM((CARRY_ROWS + tm, FF_CHUNK), F32),
                        pltpu.VMEM((D_FF // FF_CHUNK, CARRY_ROWS, FF_CHUNK), F32),
                        pltpu.VMEM((tm, D_FF), BF16)],
        compiler_params=_cparams("arbitrary"),
        name="ffn",
    )(h1, w_ffn_up.astype(BF16), conv_w, row(conv_b), w_ffn_down.astype(BF16),
      row(ln2_g), row(ln2_b))
    return h2.reshape(bsz, seq, D_MODEL)


def kernel(x, mem, rel_bias, w_in, w_mem_kv, sg_ln_g, sg_ln_b, w_spatial, b_spatial, w_proj_a,
           w_proj_b, w_proj_c, w_gate, b_gate, w_out, ln1_g, ln1_b, w_ffn_up, conv_w, conv_b,
           w_ffn_down, ln2_g, ln2_b):
    band_bias = pl.pallas_call(
        _band_bias_kernel,
        grid=(N_GROUPS,),
        in_specs=[pl.BlockSpec((1, HEADS, 1, 2 * Q_BLOCK), lambda g: (g, 0, 0, 0))],
        out_specs=pl.BlockSpec((1, HEADS * Q_BLOCK, 2 * Q_BLOCK), lambda g: (g, 0, 0)),
        out_shape=jax.ShapeDtypeStruct((N_GROUPS, HEADS * Q_BLOCK, 2 * Q_BLOCK), F32),
        name="band_bias",
    )(_offset_bias(rel_bias))
    lane = jnp.arange(HW)[None, None, :] // HEAD_DIM
    hmask = (lane == jnp.arange(HEADS)[:, None, None]).astype(BF16)
    h = x
    for l in range(w_in.shape[0]):
        h = _layer(h, mem, band_bias, hmask, w_in[l], w_mem_kv[l], sg_ln_g[l], sg_ln_b[l],
                   w_spatial[l], b_spatial[l], w_proj_a[l], w_proj_b[l], w_proj_c[l], w_gate[l],
                   b_gate[l], w_out[l], ln1_g[l], ln1_b[l], w_ffn_up[l], conv_w[l], conv_b[l],
                   w_ffn_down[l], ln2_g[l], ln2_b[l])
    return h
```

```python
import functools
import math

import jax
import jax.numpy as jnp
from jax import lax
from jax.experimental import pallas as pl
from jax.experimental.pallas import tpu as pltpu

F32 = jnp.float32
BF16 = jnp.bfloat16

LANES = 128
D_MODEL = 1024
HEAD_DIM = 64
DILATIONS = (1, 4, 16)
N_GROUPS = 3
HEADS = 4
HW = HEADS * HEAD_DIM
ATTN_W = N_GROUPS * HW
WINDOW_STEPS = 128
Q_BLOCK = 128
SG_CHUNK = 128
SG_W = 512
MEM_LEN = 256
IN_W = 3 * ATTN_W + 2 * SG_W + HW
D_FF = 2816
N_BUCKETS = 32
MAX_DISTANCE = 2048
LN_EPS = 1e-5
DEPTH = 1
ALPHA = (2 * DEPTH) ** 0.25
ATTN_SCALE = HEAD_DIM ** -0.5
LOG2E = math.log2(math.e)
Q_SCALE = ATTN_SCALE * LOG2E
NEG = -1e30

TOKEN_TILE = 512
SUPER = Q_BLOCK * DILATIONS[-1]
COMBINE_ROWS = 256
FF_CHUNK = 256
CARRY_ROWS = 8
VMEM_LIMIT = 56 * 1024 * 1024

_NT = (((1,), (1,)), ((), ()))


def _gelu(x):
    return 0.5 * x * (1.0 + jnp.tanh(math.sqrt(2.0 / math.pi) * (x + 0.044715 * (x * x * x))))


def _layer_norm(x, g, b):
    mu = jnp.mean(x, axis=-1, keepdims=True)
    xc = x - mu
    var = jnp.mean(xc * xc, axis=-1, keepdims=True)
    return xc * lax.rsqrt(var + LN_EPS) * g + b


def _resident(shape):
    zeros = (0,) * len(shape)
    return pl.BlockSpec(shape, lambda *_: zeros, pipeline_mode=pl.Buffered(1))


def _head_dense(cols):
    low = lax.broadcasted_iota(jnp.int32, (cols[0].shape[0], LANES), 1) < HEAD_DIM
    return jnp.concatenate([jnp.where(low, cols[h], cols[h + 1]) for h in range(0, HEADS, 2)],
                           axis=1)


def _mask_heads(v, hm_ref):
    chunks = [v[i:i + Q_BLOCK] for i in range(0, v.shape[0], Q_BLOCK)]
    return [jnp.concatenate([c * hm_ref[h] for c in chunks], axis=0) for h in range(HEADS)]


def _packed_heads_attention(q, k, vm, hm_ref, bias_fn):
    m_rows = q.shape[0]
    qs = jnp.concatenate([q * hm_ref[h] for h in range(HEADS)], axis=0)
    s = lax.dot_general(qs, k, _NT, preferred_element_type=F32)
    ps, ms, ls = [], [], []
    for h in range(HEADS):
        sh = s[h * m_rows:(h + 1) * m_rows]
        bias = bias_fn(h)
        if bias is not None:
            sh = sh + bias
        m = jnp.max(sh, axis=-1, keepdims=True)
        p = jnp.exp2(sh - m)
        ps.append(p.astype(BF16))
        ms.append(m)
        ls.append(jnp.sum(p, axis=-1, keepdims=True))
    pv = jnp.dot(jnp.concatenate(ps, axis=1), vm, preferred_element_type=F32)
    return pv, _head_dense(ms), _head_dense(ls)


def _memkv_kernel(mem_ref, w_ref, o_ref):
    o_ref[0] = jnp.dot(mem_ref[0].astype(BF16), w_ref[...],
                       preferred_element_type=F32).astype(BF16)


def _band_bias_kernel(f_ref, o_ref):
    for h in range(HEADS):
        rep = jnp.broadcast_to(f_ref[0, h] * LOG2E, (Q_BLOCK, 2 * Q_BLOCK))
        o_ref[0, h * Q_BLOCK:(h + 1) * Q_BLOCK] = pltpu.roll(rep, 0, 1, stride=1, stride_axis=0)


def _proj_kernel(x_ref, win_ref, kvm_ref, lng_ref, lnb_ref, wsp_ref, bsp_ref, hm_ref,
                 q1_ref, q2_ref, q3_ref, kv1_ref, kv2_ref, kv3_ref, ob_ref, oc_ref, de_buf):
    xb = x_ref[0].astype(BF16)
    tm = xb.shape[0]

    def proj(lo, width):
        return jnp.dot(xb, win_ref[:, lo:lo + width], preferred_element_type=F32)

    def put(ref, lane0, val, dil):
        if dil == 1:
            ref[0, :, lane0:lane0 + HW] = val.astype(BF16)
            return
        stride_cols = ref.shape[2] // dil
        for c in range(HW // LANES):
            de_buf[...] = val[:, c * LANES:(c + 1) * LANES]
            for r in range(dil):
                piece = de_buf[pl.ds(r, tm // dil, stride=dil), :]
                col = r * stride_cols + lane0 + c * LANES
                ref[0, :, col:col + LANES] = piece.astype(BF16)

    for g, (q_ref, kv_ref) in enumerate(((q1_ref, kv1_ref), (q2_ref, kv2_ref), (q3_ref, kv3_ref))):
        put(q_ref, 0, proj(g * HW, HW) * Q_SCALE, DILATIONS[g])
        put(kv_ref, 0, proj(ATTN_W + g * HW, HW), DILATIONS[g])
        put(kv_ref, HW, proj(2 * ATTN_W + g * HW, HW), DILATIONS[g])

    u = _gelu(proj(3 * ATTN_W, SG_W))
    vn = _layer_norm(_gelu(proj(3 * ATTN_W + SG_W, SG_W)), lng_ref[...], lnb_ref[...]).astype(BF16)
    row = lax.broadcasted_iota(jnp.int32, (2 * SG_CHUNK, SG_CHUNK), 0) & (SG_CHUNK - 1)
    col = lax.broadcasted_iota(jnp.int32, (2 * SG_CHUNK, SG_CHUNK), 1)
    low_half = lax.broadcasted_iota(jnp.int32, (SG_CHUNK, SG_CHUNK), 1) < HEAD_DIM
    n_chunks = tm // SG_CHUNK
    for j in range(SG_W // SG_CHUNK):
        cols = slice(j * SG_CHUNK, (j + 1) * SG_CHUNK)
        w_pair = jnp.where(row >= col, wsp_ref[j], 0.0).astype(BF16)
        rhs = jnp.concatenate([vn[c * SG_CHUNK:(c + 1) * SG_CHUNK, cols] for c in range(n_chunks)],
                              axis=1)
        r = jnp.dot(w_pair, rhs, preferred_element_type=F32)
        for c in range(n_chunks):
            rows = slice(c * SG_CHUNK, (c + 1) * SG_CHUNK)
            sv = jnp.where(low_half, r[:SG_CHUNK, rows], r[SG_CHUNK:, rows]) + bsp_ref[j]
            ob_ref[0, rows, cols] = (u[rows, cols] * sv).astype(BF16)

    qc = (proj(3 * ATTN_W + 2 * SG_W, HW) * Q_SCALE).astype(BF16)
    vm = jnp.concatenate(_mask_heads(kvm_ref[0, :, HW:], hm_ref), axis=0)
    for c in range(tm // Q_BLOCK):
        rows = slice(c * Q_BLOCK, (c + 1) * Q_BLOCK)
        pv, _, l = _packed_heads_attention(qc[rows], kvm_ref[0, :, :HW], vm, hm_ref,
                                           lambda h: None)
        oc_ref[0, rows, :] = (pv * (1.0 / l)).astype(BF16)


def _attn_kernel(q1_ref, q2_ref, q3_ref, kv1_ref, kv2_ref, kv3_ref, kp1_ref, kp2_ref, kp3_ref,
                 bias_ref, hm_ref, oa_ref, pv_nat, m_nat, l_nat, bias_first):
    lane = lax.broadcasted_iota(jnp.int32, (HEADS * Q_BLOCK, 2 * Q_BLOCK), 1)
    no_prev = jnp.where(lane < Q_BLOCK, jnp.where(pl.program_id(1) == 0, NEG, 0.0), 0.0)
    for g in range(N_GROUPS):
        bias_first[g] = bias_ref[g] + no_prev

    groups = ((q1_ref, kv1_ref, kp1_ref), (q2_ref, kv2_ref, kp2_ref), (q3_ref, kv3_ref, kp3_ref))
    for g, (q_ref, kv_ref, kvp_ref) in enumerate(groups):
        dil = DILATIONS[g]
        nblk = SUPER // dil // Q_BLOCK
        for r in range(dil):
            kc = slice(r * 2 * HW, r * 2 * HW + HW)
            vc = slice(r * 2 * HW + HW, (r + 1) * 2 * HW)
            qc = slice(r * HW, (r + 1) * HW)
            k_prev = kvp_ref[0, :, kc]
            vm_prev = _mask_heads(kvp_ref[0, :, vc], hm_ref)
            for i in range(nblk):
                rq = slice(i * Q_BLOCK, (i + 1) * Q_BLOCK)
                k_cur = kv_ref[0, rq, kc]
                vm_cur = _mask_heads(kv_ref[0, rq, vc], hm_ref)
                k2 = jnp.concatenate([k_prev, k_cur], axis=0)
                vm = jnp.concatenate([blk[h] for h in range(HEADS) for blk in (vm_prev, vm_cur)],
                                     axis=0)
                table = bias_first if i == 0 else bias_ref
                pv, m, l = _packed_heads_attention(
                    q_ref[0, rq, qc], k2, vm, hm_ref,
                    lambda h, table=table: table[g, h * Q_BLOCK:(h + 1) * Q_BLOCK, :])
                nat = pl.ds(i * Q_BLOCK * dil + r, Q_BLOCK, stride=dil) if dil > 1 else rq
                for c in range(HW // LANES):
                    lanes = slice(c * LANES, (c + 1) * LANES)
                    pv_nat[g, c, nat, :] = pv[:, lanes]
                    m_nat[g, c, nat, :] = m[:, lanes]
                    l_nat[g, c, nat, :] = l[:, lanes]
                k_prev, vm_prev = k_cur, vm_cur

    for t in range(SUPER // COMBINE_ROWS):
        rows = slice(t * COMBINE_ROWS, (t + 1) * COMBINE_ROWS)
        for c in range(HW // LANES):
            m1, m2, m3 = m_nat[0, c, rows, :], m_nat[1, c, rows, :], m_nat[2, c, rows, :]
            m = jnp.maximum(jnp.maximum(m1, m2), m3)
            e1, e2, e3 = jnp.exp2(m1 - m), jnp.exp2(m2 - m), jnp.exp2(m3 - m)
            num = e1 * pv_nat[0, c, rows, :] + e2 * pv_nat[1, c, rows, :] + e3 * pv_nat[2, c, rows, :]
            den = e1 * l_nat[0, c, rows, :] + e2 * l_nat[1, c, rows, :] + e3 * l_nat[2, c, rows, :]
            oa_ref[0, rows, c * LANES:(c + 1) * LANES] = (num * (1.0 / den)).astype(BF16)


def _merge_kernel(x_ref, oa_ref, ob_ref, oc_ref, wg_ref, bg_ref, wpa_ref, wpb_ref, wpc_ref,
                  wout_ref, lng_ref, lnb_ref, h_ref):
    x = x_ref[...]
    xb = x.astype(BF16)

    def gated(k, o_ref, wp_ref):
        cols = slice(k * D_MODEL, (k + 1) * D_MODEL)
        t = jnp.tanh(jnp.dot(xb, wg_ref[:, cols], preferred_element_type=F32) + bg_ref[:, cols])
        half_p = jnp.dot(o_ref[...], wp_ref[...], preferred_element_type=F32)
        return half_p + half_p * t

    merged = gated(0, oa_ref, wpa_ref) + gated(1, ob_ref, wpb_ref) + gated(2, oc_ref, wpc_ref)
    y = jnp.dot(merged.astype(BF16), wout_ref[...], preferred_element_type=F32)
    h_ref[...] = _layer_norm(ALPHA * x + y, lng_ref[...], lnb_ref[...])


def _ffn_kernel(h_ref, wup_ref, cw_ref, cb_ref, wdn_ref, lng_ref, lnb_ref, o_ref,
                a_buf, carry, g_buf, *, tiles_per_seq):
    tm = h_ref.shape[0]
    n_chunks = D_FF // FF_CHUNK

    @pl.when(pl.program_id(0) % tiles_per_seq == 0)
    def _():
        carry[...] = jnp.zeros(carry.shape, F32)

    h = h_ref[...]
    hb = h.astype(BF16)
    for c in range(n_chunks):
        cols = slice(c * FF_CHUNK, (c + 1) * FF_CHUNK)
        a = jnp.dot(hb, wup_ref[:, cols], preferred_element_type=F32)
        gate = jnp.dot(hb, wup_ref[:, D_FF + c * FF_CHUNK:D_FF + (c + 1) * FF_CHUNK],
                       preferred_element_type=F32)
        a_buf[:CARRY_ROWS] = carry[c]
        a_buf[CARRY_ROWS:] = a
        carry[c] = a[tm - CARRY_ROWS:]
        conv = (cw_ref[0:1, cols] * a_buf[CARRY_ROWS - 2:CARRY_ROWS - 2 + tm]
                + cw_ref[1:2, cols] * a_buf[CARRY_ROWS - 1:CARRY_ROWS - 1 + tm]
                + cw_ref[2:3, cols] * a + cb_ref[:, cols])
        g_buf[:, cols] = (_gelu(conv) * gate).astype(BF16)
    y = jnp.dot(g_buf[...], wdn_ref[...], preferred_element_type=F32)
    o_ref[...] = _layer_norm(ALPHA * h + y, lng_ref[...], lnb_ref[...])


def _t5_bucket(dist):
    max_exact = N_BUCKETS // 2
    n = jnp.maximum(dist, 1).astype(jnp.float32)
    large = max_exact + (jnp.log(n / max_exact) / math.log(MAX_DISTANCE / max_exact)
                         * (N_BUCKETS - max_exact)).astype(jnp.int32)
    large = jnp.minimum(large, N_BUCKETS - 1)
    return jnp.where(dist < max_exact, dist, large)


def _offset_bias(rel_bias):
    steps = WINDOW_STEPS - jnp.arange(2 * Q_BLOCK)
    rows = []
    for g, dil in enumerate(DILATIONS):
        bucket = _t5_bucket(jnp.maximum(steps, 0) * dil)
        bias = rel_bias[:, g * HEADS:(g + 1) * HEADS][bucket].astype(F32)
        rows.append(jnp.where(steps[:, None] >= 0, bias, NEG).T)
    return jnp.stack(rows, 0)[:, :, None, :]


def _cparams(*sem):
    return pltpu.CompilerParams(dimension_semantics=sem, vmem_limit_bytes=VMEM_LIMIT)


def _layer(x, mem, band_bias, hmask, w_in, w_mem_kv, sg_ln_g, sg_ln_b, w_spatial, b_spatial,
           w_proj_a, w_proj_b, w_proj_c, w_gate, b_gate, w_out, ln1_g, ln1_b,
           w_ffn_up, conv_w, conv_b, w_ffn_down, ln2_g, ln2_b):
    bsz, seq, _ = x.shape
    n_tok = bsz * seq
    tm = TOKEN_TILE
    assert seq % SUPER == 0 and SUPER % tm == 0 and tm % (SG_CHUNK * 2) == 0
    assert D_FF % FF_CHUNK == 0 and tm % (16 * DILATIONS[-1]) == 0
    n_tiles = n_tok // tm
    tiles_per_seq = seq // tm
    row = lambda v: v.reshape(1, -1)
    tok = lambda width: pl.BlockSpec((tm, width), lambda i: (i, 0))
    tile3 = lambda rows, width: pl.BlockSpec((1, rows, width), lambda b, t: (b, t, 0))

    kvm = pl.pallas_call(
        _memkv_kernel,
        grid=(bsz,),
        in_specs=[pl.BlockSpec((1, MEM_LEN, D_MODEL), lambda b: (b, 0, 0)),
                  _resident((D_MODEL, 2 * HW))],
        out_specs=pl.BlockSpec((1, MEM_LEN, 2 * HW), lambda b: (b, 0, 0)),
        out_shape=jax.ShapeDtypeStruct((bsz, MEM_LEN, 2 * HW), BF16),
        compiler_params=_cparams("parallel"),
        name="memkv",
    )(mem, w_mem_kv.astype(BF16))

    wsp = w_spatial.reshape(SG_W // SG_CHUNK, 2 * SG_CHUNK, SG_CHUNK)
    bsp = jnp.repeat(b_spatial.reshape(SG_W // SG_CHUNK, 2, SG_CHUNK).transpose(0, 2, 1),
                     HEAD_DIM, axis=-1)
    qkv_shapes = ([jax.ShapeDtypeStruct((bsz, seq // d, d * HW), BF16) for d in DILATIONS]
                  + [jax.ShapeDtypeStruct((bsz, seq // d, d * 2 * HW), BF16) for d in DILATIONS])
    qkv_specs = ([tile3(tm // d, d * HW) for d in DILATIONS]
                 + [tile3(tm // d, d * 2 * HW) for d in DILATIONS])
    q1, q2, q3, kv1, kv2, kv3, ob, oc = pl.pallas_call(
        _proj_kernel,
        grid=(bsz, tiles_per_seq),
        in_specs=[tile3(tm, D_MODEL), _resident((D_MODEL, IN_W)),
                  pl.BlockSpec((1, MEM_LEN, 2 * HW), lambda b, t: (b, 0, 0)),
                  _resident((1, SG_W)), _resident((1, SG_W)),
                  _resident(wsp.shape), _resident(bsp.shape), _resident(hmask.shape)],
        out_specs=qkv_specs + [tile3(tm, SG_W), tile3(tm, HW)],
        out_shape=qkv_shapes + [jax.ShapeDtypeStruct((bsz, seq, SG_W), BF16),
                                jax.ShapeDtypeStruct((bsz, seq, HW), BF16)],
        scratch_shapes=[pltpu.VMEM((tm, LANES), F32)],
        compiler_params=_cparams("parallel", "parallel"),
        name="proj",
    )(x, w_in.astype(BF16), kvm, row(sg_ln_g), row(sg_ln_b), wsp, bsp, hmask)

    n_super = seq // SUPER
    super3 = lambda d, width: pl.BlockSpec((1, SUPER // d, d * width), lambda b, j: (b, j, 0))

    def prev3(d):
        nblk = SUPER // d // Q_BLOCK
        return pl.BlockSpec((1, Q_BLOCK, d * 2 * HW),
                            lambda b, j: (b, jnp.maximum(j * nblk - 1, 0), 0))

    oa = pl.pallas_call(
        _attn_kernel,
        grid=(bsz, n_super),
        in_specs=[super3(d, HW) for d in DILATIONS] + [super3(d, 2 * HW) for d in DILATIONS]
        + [prev3(d) for d in DILATIONS] + [_resident(band_bias.shape), _resident(hmask.shape)],
        out_specs=pl.BlockSpec((1, SUPER, HW), lambda b, j: (b, j, 0)),
        out_shape=jax.ShapeDtypeStruct((bsz, seq, HW), BF16),
        scratch_shapes=[pltpu.VMEM((N_GROUPS, HW // LANES, SUPER, LANES), F32)] * 3
        + [pltpu.VMEM((N_GROUPS, HEADS * Q_BLOCK, 2 * Q_BLOCK), F32)],
        compiler_params=_cparams("parallel", "parallel"),
        name="attn",
    )(q1, q2, q3, kv1, kv2, kv3, kv1, kv2, kv3, band_bias, hmask)

    h1 = pl.pallas_call(
        _merge_kernel,
        grid=(n_tiles,),
        in_specs=[tok(D_MODEL), tok(HW), tok(SG_W), tok(HW),
                  _resident((D_MODEL, 3 * D_MODEL)), _resident((1, 3 * D_MODEL)),
                  _resident((HW, D_MODEL)), _resident((SG_W, D_MODEL)), _resident((HW, D_MODEL)),
                  _resident((D_MODEL, D_MODEL)), _resident((1, D_MODEL)), _resident((1, D_MODEL))],
        out_specs=tok(D_MODEL),
        out_shape=jax.ShapeDtypeStruct((n_tok, D_MODEL), F32),
        compiler_params=_cparams("parallel"),
        name="merge",
    )(x.reshape(n_tok, D_MODEL), oa.reshape(n_tok, HW), ob.reshape(n_tok, SG_W),
      oc.reshape(n_tok, HW), (0.5 * w_gate).astype(BF16), row(0.5 * b_gate),
      (0.5 * w_proj_a).astype(BF16), (0.5 * w_proj_b).astype(BF16), (0.5 * w_proj_c).astype(BF16),
      w_out.astype(BF16), row(ln1_g), row(ln1_b))

    h2 = pl.pallas_call(
        functools.partial(_ffn_kernel, tiles_per_seq=tiles_per_seq),
        grid=(n_tiles,),
        in_specs=[tok(D_MODEL), _resident((D_MODEL, 2 * D_FF)), _resident(conv_w.shape),
                  _resident((1, D_FF)), _resident((D_FF, D_MODEL)),
                  _resident((1, D_MODEL)), _resident((1, D_MODEL))],
        out_specs=tok(D_MODEL),
        out_shape=jax.ShapeDtypeStruct((n_tok, D_MODEL), F32),
        scratch_shapes=[pltpu.VMEM((CARRY_ROWS + tm, FF_CHUNK), F32),
                        pltpu.VMEM((D_FF // FF_CHUNK, CARRY_ROWS, FF_CHUNK), F32),
                        pltpu.VMEM((tm, D_FF), BF16)],
        compiler_params=_cparams("arbitrary"),
        name="ffn",
    )(h1, w_ffn_up.astype(BF16), conv_w, row(conv_b), w_ffn_down.astype(BF16),
      row(ln2_g), row(ln2_b))
    return h2.reshape(bsz, seq, D_MODEL)


def kernel(x, mem, rel_bias, w_in, w_mem_kv, sg_ln_g, sg_ln_b, w_spatial, b_spatial, w_proj_a,
           w_proj_b, w_proj_c, w_gate, b_gate, w_out, ln1_g, ln1_b, w_ffn_up, conv_w, conv_b,
           w_ffn_down, ln2_g, ln2_b):
    band_bias = pl.pallas_call(
        _band_bias_kernel,
        grid=(N_GROUPS,),
        in_specs=[pl.BlockSpec((1, HEADS, 1, 2 * Q_BLOCK), lambda g: (g, 0, 0, 0))],
        out_specs=pl.BlockSpec((1, HEADS * Q_BLOCK, 2 * Q_BLOCK), lambda g: (g, 0, 0)),
        out_shape=jax.ShapeDtypeStruct((N_GROUPS, HEADS * Q_BLOCK, 2 * Q_BLOCK), F32),
        name="band_bias",
    )(_offset_bias(rel_bias))
    lane = jnp.arange(HW)[None, None, :] // HEAD_DIM
    hmask = jnp.broadcast_to(lane == jnp.arange(HEADS)[:, None, None],
                             (HEADS, Q_BLOCK, HW)).astype(BF16)
    h = x
    for l in range(w_in.shape[0]):
        h = _layer(h, mem, band_bias, hmask, w_in[l], w_mem_kv[l], sg_ln_g[l], sg_ln_b[l],
                   w_spatial[l], b_spatial[l], w_proj_a[l], w_proj_b[l], w_proj_c[l], w_gate[l],
                   b_gate[l], w_out[l], ln1_g[l], ln1_b[l], w_ffn_up[l], conv_w[l], conv_b[l],
                   w_ffn_down[l], ln2_g[l], ln2_b[l])
    return h
```

```python
import functools
import math

import jax
import jax.numpy as jnp
from jax import lax
from jax.experimental import pallas as pl
from jax.experimental.pallas import tpu as pltpu

F32 = jnp.float32
BF16 = jnp.bfloat16

LANES = 128
D_MODEL = 1024
HEAD_DIM = 64
DILATIONS = (1, 4, 16)
N_GROUPS = 3
HEADS = 4
HW = HEADS * HEAD_DIM
ATTN_W = N_GROUPS * HW
WINDOW_STEPS = 128
Q_BLOCK = 128
SG_CHUNK = 128
SG_W = 512
MEM_LEN = 256
IN_W = 3 * ATTN_W + 2 * SG_W + HW
D_FF = 2816
N_BUCKETS = 32
MAX_DISTANCE = 2048
LN_EPS = 1e-5
DEPTH = 1
ALPHA = (2 * DEPTH) ** 0.25
ATTN_SCALE = HEAD_DIM ** -0.5
LOG2E = math.log2(math.e)
Q_SCALE = ATTN_SCALE * LOG2E
NEG = -1e30

TOKEN_TILE = 512
SUPER = Q_BLOCK * DILATIONS[-1]
COMBINE_ROWS = 256
FF_CHUNK = 256
CARRY_ROWS = 8
VMEM_LIMIT = 56 * 1024 * 1024

_NT = (((1,), (1,)), ((), ()))


def _gelu(x):
    return 0.5 * x * (1.0 + jnp.tanh(math.sqrt(2.0 / math.pi) * (x + 0.044715 * (x * x * x))))


def _layer_norm(x, g, b):
    mu = jnp.mean(x, axis=-1, keepdims=True)
    xc = x - mu
    var = jnp.mean(xc * xc, axis=-1, keepdims=True)
    return xc * lax.rsqrt(var + LN_EPS) * g + b


def _resident(shape):
    zeros = (0,) * len(shape)
    return pl.BlockSpec(shape, lambda *_: zeros, pipeline_mode=pl.Buffered(1))


def _head_dense(per_head):
    low = lax.broadcasted_iota(jnp.int32, (per_head[0].shape[0], LANES), 1) < HEAD_DIM
    return jnp.concatenate([jnp.where(low, per_head[h], per_head[h + 1])
                            for h in range(0, HEADS, 2)], axis=1)


def _packed_heads_attention(q, k, v, hm_ref, bias_fn):
    m_rows = q.shape[0]
    qs = jnp.concatenate([q * hm_ref[h] for h in range(HEADS)], axis=0)
    s = lax.dot_general(qs, k, _NT, preferred_element_type=F32)
    ps, ms, ls = [], [], []
    for h in range(HEADS):
        sh = s[h * m_rows:(h + 1) * m_rows]
        bias = bias_fn(h)
        if bias is not None:
            sh = sh + bias
        m = jnp.max(sh, axis=-1, keepdims=True)
        p = jnp.exp2(sh - m)
        ps.append(p.astype(BF16))
        ms.append(m)
        ls.append(jnp.sum(p, axis=-1, keepdims=True))
    pv = jnp.dot(jnp.concatenate(ps, axis=0), v, preferred_element_type=F32)
    half = lambda h: slice((h // 2) * LANES, (h // 2 + 1) * LANES)
    pv = _head_dense([pv[h * m_rows:(h + 1) * m_rows, half(h)] for h in range(HEADS)])
    return pv, _head_dense(ms), _head_dense(ls)


def _memkv_kernel(mem_ref, w_ref, o_ref):
    o_ref[0] = jnp.dot(mem_ref[0].astype(BF16), w_ref[...],
                       preferred_element_type=F32).astype(BF16)


def _band_bias_kernel(f_ref, o_ref):
    for h in range(HEADS):
        rep = jnp.broadcast_to(f_ref[0, h] * LOG2E, (Q_BLOCK, 2 * Q_BLOCK))
        o_ref[0, h * Q_BLOCK:(h + 1) * Q_BLOCK] = pltpu.roll(rep, 0, 1, stride=1, stride_axis=0)


def _proj_kernel(x_ref, win_ref, kvm_ref, lng_ref, lnb_ref, wsp_ref, bsp_ref, hm_ref,
                 q1_ref, q2_ref, q3_ref, kv1_ref, kv2_ref, kv3_ref, ob_ref, oc_ref, de_buf):
    xb = x_ref[0].astype(BF16)
    tm = xb.shape[0]

    def proj(lo, width):
        return jnp.dot(xb, win_ref[:, lo:lo + width], preferred_element_type=F32)

    def put(ref, lane0, val, dil):
        if dil == 1:
            ref[0, :, lane0:lane0 + HW] = val.astype(BF16)
            return
        stride_cols = ref.shape[2] // dil
        for c in range(HW // LANES):
            de_buf[...] = val[:, c * LANES:(c + 1) * LANES]
            for r in range(dil):
                piece = de_buf[pl.ds(r, tm // dil, stride=dil), :]
                col = r * stride_cols + lane0 + c * LANES
                ref[0, :, col:col + LANES] = piece.astype(BF16)

    for g, (q_ref, kv_ref) in enumerate(((q1_ref, kv1_ref), (q2_ref, kv2_ref), (q3_ref, kv3_ref))):
        put(q_ref, 0, proj(g * HW, HW) * Q_SCALE, DILATIONS[g])
        put(kv_ref, 0, proj(ATTN_W + g * HW, HW), DILATIONS[g])
        put(kv_ref, HW, proj(2 * ATTN_W + g * HW, HW), DILATIONS[g])

    u = _gelu(proj(3 * ATTN_W, SG_W))
    vn = _layer_norm(_gelu(proj(3 * ATTN_W + SG_W, SG_W)), lng_ref[...], lnb_ref[...]).astype(BF16)
    row = lax.broadcasted_iota(jnp.int32, (2 * SG_CHUNK, SG_CHUNK), 0) & (SG_CHUNK - 1)
    col = lax.broadcasted_iota(jnp.int32, (2 * SG_CHUNK, SG_CHUNK), 1)
    low_half = lax.broadcasted_iota(jnp.int32, (SG_CHUNK, SG_CHUNK), 1) < HEAD_DIM
    n_chunks = tm // SG_CHUNK
    for j in range(SG_W // SG_CHUNK):
        cols = slice(j * SG_CHUNK, (j + 1) * SG_CHUNK)
        w_pair = jnp.where(row >= col, wsp_ref[j], 0.0).astype(BF16)
        rhs = jnp.concatenate([vn[c * SG_CHUNK:(c + 1) * SG_CHUNK, cols] for c in range(n_chunks)],
                              axis=1)
        r = jnp.dot(w_pair, rhs, preferred_element_type=F32)
        for c in range(n_chunks):
            rows = slice(c * SG_CHUNK, (c + 1) * SG_CHUNK)
            sv = jnp.where(low_half, r[:SG_CHUNK, rows], r[SG_CHUNK:, rows]) + bsp_ref[j]
            ob_ref[0, rows, cols] = (u[rows, cols] * sv).astype(BF16)

    qc = (proj(3 * ATTN_W + 2 * SG_W, HW) * Q_SCALE).astype(BF16)
    for c in range(tm // Q_BLOCK):
        rows = slice(c * Q_BLOCK, (c + 1) * Q_BLOCK)
        pv, _, l = _packed_heads_attention(qc[rows], kvm_ref[0, :, :HW], kvm_ref[0, :, HW:],
                                           hm_ref, lambda h: None)
        oc_ref[0, rows, :] = (pv * (1.0 / l)).astype(BF16)


def _attn_kernel(q1_ref, q2_ref, q3_ref, kv1_ref, kv2_ref, kv3_ref, kp1_ref, kp2_ref, kp3_ref,
                 bias_ref, hm_ref, oa_ref, pv_nat, m_nat, l_nat, bias_first):
    lane = lax.broadcasted_iota(jnp.int32, (HEADS * Q_BLOCK, 2 * Q_BLOCK), 1)
    no_prev = jnp.where(lane < Q_BLOCK, jnp.where(pl.program_id(1) == 0, NEG, 0.0), 0.0)
    for g in range(N_GROUPS):
        bias_first[g] = bias_ref[g] + no_prev

    groups = ((q1_ref, kv1_ref, kp1_ref), (q2_ref, kv2_ref, kp2_ref), (q3_ref, kv3_ref, kp3_ref))
    for g, (q_ref, kv_ref, kvp_ref) in enumerate(groups):
        dil = DILATIONS[g]
        nblk = SUPER // dil // Q_BLOCK
        for r in range(dil):
            kc = slice(r * 2 * HW, r * 2 * HW + HW)
            vc = slice(r * 2 * HW + HW, (r + 1) * 2 * HW)
            qc = slice(r * HW, (r + 1) * HW)
            for i in range(nblk):
                rq = slice(i * Q_BLOCK, (i + 1) * Q_BLOCK)
                if i == 0:
                    k2 = jnp.concatenate([kvp_ref[0, :, kc], kv_ref[0, rq, kc]], axis=0)
                    v2 = jnp.concatenate([kvp_ref[0, :, vc], kv_ref[0, rq, vc]], axis=0)
                else:
                    rk = slice((i - 1) * Q_BLOCK, (i + 1) * Q_BLOCK)
                    k2 = kv_ref[0, rk, kc]
                    v2 = kv_ref[0, rk, vc]
                table = bias_first if i == 0 else bias_ref
                pv, m, l = _packed_heads_attention(
                    q_ref[0, rq, qc], k2, v2, hm_ref,
                    lambda h, table=table: table[g, h * Q_BLOCK:(h + 1) * Q_BLOCK, :])
                nat = pl.ds(i * Q_BLOCK * dil + r, Q_BLOCK, stride=dil) if dil > 1 else rq
                for c in range(HW // LANES):
                    lanes = slice(c * LANES, (c + 1) * LANES)
                    pv_nat[g, c, nat, :] = pv[:, lanes]
                    m_nat[g, c, nat, :] = m[:, lanes]
                    l_nat[g, c, nat, :] = l[:, lanes]

    for t in range(SUPER // COMBINE_ROWS):
        rows = slice(t * COMBINE_ROWS, (t + 1) * COMBINE_ROWS)
        for c in range(HW // LANES):
            m1, m2, m3 = m_nat[0, c, rows, :], m_nat[1, c, rows, :], m_nat[2, c, rows, :]
            m = jnp.maximum(jnp.maximum(m1, m2), m3)
            e1, e2, e3 = jnp.exp2(m1 - m), jnp.exp2(m2 - m), jnp.exp2(m3 - m)
            num = e1 * pv_nat[0, c, rows, :] + e2 * pv_nat[1, c, rows, :] + e3 * pv_nat[2, c, rows, :]
            den = e1 * l_nat[0, c, rows, :] + e2 * l_nat[1, c, rows, :] + e3 * l_nat[2, c, rows, :]
            oa_ref[0, rows, c * LANES:(c + 1) * LANES] = (num * (1.0 / den)).astype(BF16)


def _merge_kernel(x_ref, oa_ref, ob_ref, oc_ref, wg_ref, bg_ref, wpa_ref, wpb_ref, wpc_ref,
                  wout_ref, lng_ref, lnb_ref, h_ref):
    x = x_ref[...]
    xb = x.astype(BF16)

    def gated(k, o_ref, wp_ref):
        cols = slice(k * D_MODEL, (k + 1) * D_MODEL)
        t = jnp.tanh(jnp.dot(xb, wg_ref[:, cols], preferred_element_type=F32) + bg_ref[:, cols])
        half_p = jnp.dot(o_ref[...], wp_ref[...], preferred_element_type=F32)
        return half_p + half_p * t

    merged = gated(0, oa_ref, wpa_ref) + gated(1, ob_ref, wpb_ref) + gated(2, oc_ref, wpc_ref)
    y = jnp.dot(merged.astype(BF16), wout_ref[...], preferred_element_type=F32)
    h_ref[...] = _layer_norm(ALPHA * x + y, lng_ref[...], lnb_ref[...])


def _ffn_kernel(h_ref, wup_ref, cw_ref, cb_ref, wdn_ref, lng_ref, lnb_ref, o_ref,
                a_buf, carry, g_buf, *, tiles_per_seq):
    tm = h_ref.shape[0]
    n_chunks = D_FF // FF_CHUNK

    @pl.when(pl.program_id(0) % tiles_per_seq == 0)
    def _():
        carry[...] = jnp.zeros(carry.shape, F32)

    h = h_ref[...]
    hb = h.astype(BF16)
    for c in range(n_chunks):
        cols = slice(c * FF_CHUNK, (c + 1) * FF_CHUNK)
        a = jnp.dot(hb, wup_ref[:, cols], preferred_element_type=F32)
        gate = jnp.dot(hb, wup_ref[:, D_FF + c * FF_CHUNK:D_FF + (c + 1) * FF_CHUNK],
                       preferred_element_type=F32)
        a_buf[:CARRY_ROWS] = carry[c]
        a_buf[CARRY_ROWS:] = a
        carry[c] = a[tm - CARRY_ROWS:]
        conv = (cw_ref[0:1, cols] * a_buf[CARRY_ROWS - 2:CARRY_ROWS - 2 + tm]
                + cw_ref[1:2, cols] * a_buf[CARRY_ROWS - 1:CARRY_ROWS - 1 + tm]
                + cw_ref[2:3, cols] * a + cb_ref[:, cols])
        g_buf[:, cols] = (_gelu(conv) * gate).astype(BF16)
    y = jnp.dot(g_buf[...], wdn_ref[...], preferred_element_type=F32)
    o_ref[...] = _layer_norm(ALPHA * h + y, lng_ref[...], lnb_ref[...])


def _t5_bucket(dist):
    max_exact = N_BUCKETS // 2
    n = jnp.maximum(dist, 1).astype(jnp.float32)
    large = max_exact + (jnp.log(n / max_exact) / math.log(MAX_DISTANCE / max_exact)
                         * (N_BUCKETS - max_exact)).astype(jnp.int32)
    large = jnp.minimum(large, N_BUCKETS - 1)
    return jnp.where(dist < max_exact, dist, large)


def _offset_bias(rel_bias):
    steps = WINDOW_STEPS - jnp.arange(2 * Q_BLOCK)
    rows = []
    for g, dil in enumerate(DILATIONS):
        bucket = _t5_bucket(jnp.maximum(steps, 0) * dil)
        bias = rel_bias[:, g * HEADS:(g + 1) * HEADS][bucket].astype(F32)
        rows.append(jnp.where(steps[:, None] >= 0, bias, NEG).T)
    return jnp.stack(rows, 0)[:, :, None, :]


def _cparams(*sem):
    return pltpu.CompilerParams(dimension_semantics=sem, vmem_limit_bytes=VMEM_LIMIT)


def _layer(x, mem, band_bias, hmask, w_in, w_mem_kv, sg_ln_g, sg_ln_b, w_spatial, b_spatial,
           w_proj_a, w_proj_b, w_proj_c, w_gate, b_gate, w_out, ln1_g, ln1_b,
           w_ffn_up, conv_w, conv_b, w_ffn_down, ln2_g, ln2_b):
    bsz, seq, _ = x.shape
    n_tok = bsz * seq
    tm = TOKEN_TILE
    assert seq % SUPER == 0 and SUPER % tm == 0 and tm % (SG_CHUNK * 2) == 0
    assert D_FF % FF_CHUNK == 0 and tm % (16 * DILATIONS[-1]) == 0
    n_tiles = n_tok // tm
    tiles_per_seq = seq // tm
    row = lambda v: v.reshape(1, -1)
    tok = lambda width: pl.BlockSpec((tm, width), lambda i: (i, 0))
    tile3 = lambda rows, width: pl.BlockSpec((1, rows, width), lambda b, t: (b, t, 0))

    kvm = pl.pallas_call(
        _memkv_kernel,
        grid=(bsz,),
        in_specs=[pl.BlockSpec((1, MEM_LEN, D_MODEL), lambda b: (b, 0, 0)),
                  _resident((D_MODEL, 2 * HW))],
        out_specs=pl.BlockSpec((1, MEM_LEN, 2 * HW), lambda b: (b, 0, 0)),
        out_shape=jax.ShapeDtypeStruct((bsz, MEM_LEN, 2 * HW), BF16),
        compiler_params=_cparams("parallel"),
        name="memkv",
    )(mem, w_mem_kv.astype(BF16))

    wsp = w_spatial.reshape(SG_W // SG_CHUNK, 2 * SG_CHUNK, SG_CHUNK)
    bsp = jnp.repeat(b_spatial.reshape(SG_W // SG_CHUNK, 2, SG_CHUNK).transpose(0, 2, 1),
                     HEAD_DIM, axis=-1)
    qkv_shapes = ([jax.ShapeDtypeStruct((bsz, seq // d, d * HW), BF16) for d in DILATIONS]
                  + [jax.ShapeDtypeStruct((bsz, seq // d, d * 2 * HW), BF16) for d in DILATIONS])
    qkv_specs = ([tile3(tm // d, d * HW) for d in DILATIONS]
                 + [tile3(tm // d, d * 2 * HW) for d in DILATIONS])
    q1, q2, q3, kv1, kv2, kv3, ob, oc = pl.pallas_call(
        _proj_kernel,
        grid=(bsz, tiles_per_seq),
        in_specs=[tile3(tm, D_MODEL), _resident((D_MODEL, IN_W)),
                  pl.BlockSpec((1, MEM_LEN, 2 * HW), lambda b, t: (b, 0, 0)),
                  _resident((1, SG_W)), _resident((1, SG_W)),
                  _resident(wsp.shape), _resident(bsp.shape), _resident(hmask.shape)],
        out_specs=qkv_specs + [tile3(tm, SG_W), tile3(tm, HW)],
        out_shape=qkv_shapes + [jax.ShapeDtypeStruct((bsz, seq, SG_W), BF16),
                                jax.ShapeDtypeStruct((bsz, seq, HW), BF16)],
        scratch_shapes=[pltpu.VMEM((tm, LANES), F32)],
        compiler_params=_cparams("parallel", "parallel"),
        name="proj",
    )(x, w_in.astype(BF16), kvm, row(sg_ln_g), row(sg_ln_b), wsp, bsp, hmask)

    n_super = seq // SUPER
    super3 = lambda d, width: pl.BlockSpec((1, SUPER // d, d * width), lambda b, j: (b, j, 0))

    def prev3(d):
        nblk = SUPER // d // Q_BLOCK
        return pl.BlockSpec((1, Q_BLOCK, d * 2 * HW),
                            lambda b, j: (b, jnp.maximum(j * nblk - 1, 0), 0))

    oa = pl.pallas_call(
        _attn_kernel,
        grid=(bsz, n_super),
        in_specs=[super3(d, HW) for d in DILATIONS] + [super3(d, 2 * HW) for d in DILATIONS]
        + [prev3(d) for d in DILATIONS] + [_resident(band_bias.shape), _resident(hmask.shape)],
        out_specs=pl.BlockSpec((1, SUPER, HW), lambda b, j: (b, j, 0)),
        out_shape=jax.ShapeDtypeStruct((bsz, seq, HW), BF16),
        scratch_shapes=[pltpu.VMEM((N_GROUPS, HW // LANES, SUPER, LANES), F32)] * 3
        + [pltpu.VMEM((N_GROUPS, HEADS * Q_BLOCK, 2 * Q_BLOCK), F32)],
        compiler_params=_cparams("parallel", "parallel"),
        name="attn",
    )(q1, q2, q3, kv1, kv2, kv3, kv1, kv2, kv3, band_bias, hmask)

    h1 = pl.pallas_call(
        _merge_kernel,
        grid=(n_tiles,),
        in_specs=[tok(D_MODEL), tok(HW), tok(SG_W), tok(HW),
                  _resident((D_MODEL, 3 * D_MODEL)), _resident((1, 3 * D_MODEL)),
                  _resident((HW, D_MODEL)), _resident((SG_W, D_MODEL)), _resident((HW, D_MODEL)),
                  _resident((D_MODEL, D_MODEL)), _resident((1, D_MODEL)), _resident((1, D_MODEL))],
        out_specs=tok(D_MODEL),
        out_shape=jax.ShapeDtypeStruct((n_tok, D_MODEL), F32),
        compiler_params=_cparams("parallel"),
        name="merge",
    )(x.reshape(n_tok, D_MODEL), oa.reshape(n_tok, HW), ob.reshape(n_tok, SG_W),
      oc.reshape(n_tok, HW), (0.5 * w_gate).astype(BF16), row(0.5 * b_gate),
      (0.5 * w_proj_a).astype(BF16), (0.5 * w_proj_b).astype(BF16), (0.5 * w_proj_c).astype(BF16),
      w_out.astype(BF16), row(ln1_g), row(ln1_b))

    h2 = pl.pallas_call(
        functools.partial(_ffn_kernel, tiles_per_seq=tiles_per_seq),
        grid=(n_tiles,),
        in_specs=[tok(D_MODEL), _resident((D_MODEL, 2 * D_FF)), _resident(conv_w.shape),
                  _resident((1, D_FF)), _resident((D_FF, D_MODEL)),
                  _resident((1, D_MODEL)), _resident((1, D_MODEL))],
        out_specs=tok(D_MODEL),
        out_shape=jax.ShapeDtypeStruct((n_tok, D_MODEL), F32),
        scratch_shapes=[pltpu.VMEM((CARRY_ROWS + tm, FF_CHUNK), F32),
                        pltpu.VMEM((D_FF // FF_CHUNK, CARRY_ROWS, FF_CHUNK), F32),
                        pltpu.VMEM((tm, D_FF), BF16)],
        compiler_params=_cparams("arbitrary"),
        name="ffn",
    )(h1, w_ffn_up.astype(BF16), conv_w, row(conv_b), w_ffn_down.astype(BF16),
      row(ln2_g), row(ln2_b))
    return h2.reshape(bsz, seq, D_MODEL)


def kernel(x, mem, rel_bias, w_in, w_mem_kv, sg_ln_g, sg_ln_b, w_spatial, b_spatial, w_proj_a,
           w_proj_b, w_proj_c, w_gate, b_gate, w_out, ln1_g, ln1_b, w_ffn_up, conv_w, conv_b,
           w_ffn_down, ln2_g, ln2_b):
    band_bias = pl.pallas_call(
        _band_bias_kernel,
        grid=(N_GROUPS,),
        in_specs=[pl.BlockSpec((1, HEADS, 1, 2 * Q_BLOCK), lambda g: (g, 0, 0, 0))],
        out_specs=pl.BlockSpec((1, HEADS * Q_BLOCK, 2 * Q_BLOCK), lambda g: (g, 0, 0)),
        out_shape=jax.ShapeDtypeStruct((N_GROUPS, HEADS * Q_BLOCK, 2 * Q_BLOCK), F32),
        name="band_bias",
    )(_offset_bias(rel_bias))
    lane = jnp.arange(HW)[None, None, :] // HEAD_DIM
    hmask = jnp.broadcast_to(lane == jnp.arange(HEADS)[:, None, None],
                             (HEADS, Q_BLOCK, HW)).astype(BF16)
    h = x
    for l in range(w_in.shape[0]):
        h = _layer(h, mem, band_bias, hmask, w_in[l], w_mem_kv[l], sg_ln_g[l], sg_ln_b[l],
                   w_spatial[l], b_spatial[l], w_proj_a[l], w_proj_b[l], w_proj_c[l], w_gate[l],
                   b_gate[l], w_out[l], ln1_g[l], ln1_b[l], w_ffn_up[l], conv_w[l], conv_b[l],
                   w_ffn_down[l], ln2_g[l], ln2_b[l])
    return h
```

```python
import functools
import math

import jax
import jax.numpy as jnp
from jax import lax
from jax.experimental import pallas as pl
from jax.experimental.pallas import tpu as pltpu

F32 = jnp.float32
BF16 = jnp.bfloat16

LANES = 128
D_MODEL = 1024
HEAD_DIM = 64
DILATIONS = (1, 4, 16)
N_GROUPS = 3
HEADS = 4
HW = HEADS * HEAD_DIM
ATTN_W = N_GROUPS * HW
WINDOW_STEPS = 128
Q_BLOCK = 128
SG_CHUNK = 128
SG_W = 512
MEM_LEN = 256
IN_W = 3 * ATTN_W + 2 * SG_W + HW
D_FF = 2816
N_BUCKETS = 32
MAX_DISTANCE = 2048
LN_EPS = 1e-5
DEPTH = 1
ALPHA = (2 * DEPTH) ** 0.25
ATTN_SCALE = HEAD_DIM ** -0.5
LOG2E = math.log2(math.e)
Q_SCALE = ATTN_SCALE * LOG2E
NEG = -1e30

TOKEN_TILE = 512
SUPER = Q_BLOCK * DILATIONS[-1]
ROW_CHUNK = 256
COMBINE_ROWS = 256
FF_CHUNK = 256
CARRY_ROWS = 8
VMEM_LIMIT = 56 * 1024 * 1024

_NT = (((1,), (1,)), ((), ()))


_GELU_C0 = math.sqrt(2.0 / math.pi)
_GELU_C1 = _GELU_C0 * 0.044715


def _gelu_tanh(x):
    return jnp.tanh(x * (_GELU_C0 + _GELU_C1 * (x * x)))


def _gelu(x):
    half_x = 0.5 * x
    return half_x + half_x * _gelu_tanh(x)


def _layer_norm(x, g, b):
    mu = jnp.mean(x, axis=-1, keepdims=True)
    xc = x - mu
    var = jnp.mean(xc * xc, axis=-1, keepdims=True)
    return xc * lax.rsqrt(var + LN_EPS) * g + b


def _resident(shape):
    zeros = (0,) * len(shape)
    return pl.BlockSpec(shape, lambda *_: zeros, pipeline_mode=pl.Buffered(1))


def _head_dense(per_head):
    low = lax.broadcasted_iota(jnp.int32, (per_head[0].shape[0], LANES), 1) < HEAD_DIM
    return jnp.concatenate([jnp.where(low, per_head[h], per_head[h + 1])
                            for h in range(0, HEADS, 2)], axis=1)


def _packed_heads_attention(q, k, v, hm_ref, bias_fn):
    m_rows = q.shape[0]
    qs = jnp.concatenate([q * hm_ref[h] for h in range(HEADS)], axis=0)
    s = lax.dot_general(qs, k, _NT, preferred_element_type=F32)
    ps, ms, ls = [], [], []
    for h in range(HEADS):
        sh = s[h * m_rows:(h + 1) * m_rows]
        bias = bias_fn(h)
        if bias is not None:
            sh = sh + bias
        m = jnp.max(sh, axis=-1, keepdims=True)
        p = jnp.exp2(sh - m)
        ps.append(p.astype(BF16))
        ms.append(m)
        ls.append(jnp.sum(p, axis=-1, keepdims=True))
    pv = jnp.dot(jnp.concatenate(ps, axis=0), v, preferred_element_type=F32)
    half = lambda h: slice((h // 2) * LANES, (h // 2 + 1) * LANES)
    pv = _head_dense([pv[h * m_rows:(h + 1) * m_rows, half(h)] for h in range(HEADS)])
    return pv, _head_dense(ms), _head_dense(ls)


def _memkv_kernel(mem_ref, w_ref, o_ref):
    o_ref[0] = jnp.dot(mem_ref[0].astype(BF16), w_ref[...],
                       preferred_element_type=F32).astype(BF16)


def _band_bias_kernel(f_ref, o_ref):
    for h in range(HEADS):
        rep = jnp.broadcast_to(f_ref[0, h] * LOG2E, (Q_BLOCK, 2 * Q_BLOCK))
        o_ref[0, h * Q_BLOCK:(h + 1) * Q_BLOCK] = pltpu.roll(rep, 0, 1, stride=1, stride_axis=0)


def _proj_kernel(x_ref, win_ref, kvm_ref, lng_ref, lnb_ref, wsp_ref, bsp_ref, hm_ref,
                 q1_ref, q2_ref, q3_ref, kv1_ref, kv2_ref, kv3_ref, ob_ref, oc_ref, de_buf):
    xb = x_ref[0].astype(BF16)
    tm = xb.shape[0]

    def proj(lo, width):
        return jnp.dot(xb, win_ref[:, lo:lo + width], preferred_element_type=F32)

    def put(ref, lane0, val, dil):
        if dil == 1:
            ref[0, :, lane0:lane0 + HW] = val.astype(BF16)
            return
        stride_cols = ref.shape[2] // dil
        for c in range(HW // LANES):
            de_buf[...] = val[:, c * LANES:(c + 1) * LANES]
            for r in range(dil):
                piece = de_buf[pl.ds(r, tm // dil, stride=dil), :]
                col = r * stride_cols + lane0 + c * LANES
                ref[0, :, col:col + LANES] = piece.astype(BF16)

    for g, (q_ref, kv_ref) in enumerate(((q1_ref, kv1_ref), (q2_ref, kv2_ref), (q3_ref, kv3_ref))):
        put(q_ref, 0, proj(g * HW, HW) * Q_SCALE, DILATIONS[g])
        put(kv_ref, 0, proj(ATTN_W + g * HW, HW), DILATIONS[g])
        put(kv_ref, HW, proj(2 * ATTN_W + g * HW, HW), DILATIONS[g])

    u = _gelu(proj(3 * ATTN_W, SG_W))
    vn = _layer_norm(_gelu(proj(3 * ATTN_W + SG_W, SG_W)), lng_ref[...], lnb_ref[...]).astype(BF16)
    row = lax.broadcasted_iota(jnp.int32, (2 * SG_CHUNK, SG_CHUNK), 0) & (SG_CHUNK - 1)
    col = lax.broadcasted_iota(jnp.int32, (2 * SG_CHUNK, SG_CHUNK), 1)
    low_half = lax.broadcasted_iota(jnp.int32, (SG_CHUNK, SG_CHUNK), 1) < HEAD_DIM
    n_chunks = tm // SG_CHUNK
    for j in range(SG_W // SG_CHUNK):
        cols = slice(j * SG_CHUNK, (j + 1) * SG_CHUNK)
        w_pair = jnp.where(row >= col, wsp_ref[j], 0.0).astype(BF16)
        rhs = jnp.concatenate([vn[c * SG_CHUNK:(c + 1) * SG_CHUNK, cols] for c in range(n_chunks)],
                              axis=1)
        r = jnp.dot(w_pair, rhs, preferred_element_type=F32)
        for c in range(n_chunks):
            rows = slice(c * SG_CHUNK, (c + 1) * SG_CHUNK)
            sv = jnp.where(low_half, r[:SG_CHUNK, rows], r[SG_CHUNK:, rows]) + bsp_ref[j]
            ob_ref[0, rows, cols] = (u[rows, cols] * sv).astype(BF16)

    qc = (proj(3 * ATTN_W + 2 * SG_W, HW) * Q_SCALE).astype(BF16)
    for c in range(tm // Q_BLOCK):
        rows = slice(c * Q_BLOCK, (c + 1) * Q_BLOCK)
        pv, _, l = _packed_heads_attention(qc[rows], kvm_ref[0, :, :HW], kvm_ref[0, :, HW:],
                                           hm_ref, lambda h: None)
        oc_ref[0, rows, :] = (pv * (1.0 / l)).astype(BF16)


def _attn_kernel(q1_ref, q2_ref, q3_ref, kv1_ref, kv2_ref, kv3_ref, kp1_ref, kp2_ref, kp3_ref,
                 bias_ref, hm_ref, oa_ref, pv_nat, m_nat, l_nat, bias_first):
    lane = lax.broadcasted_iota(jnp.int32, (HEADS * Q_BLOCK, 2 * Q_BLOCK), 1)
    no_prev = jnp.where(lane < Q_BLOCK, jnp.where(pl.program_id(1) == 0, NEG, 0.0), 0.0)
    for g in range(N_GROUPS):
        bias_first[g] = bias_ref[g] + no_prev

    groups = ((q1_ref, kv1_ref, kp1_ref), (q2_ref, kv2_ref, kp2_ref), (q3_ref, kv3_ref, kp3_ref))
    for g, (q_ref, kv_ref, kvp_ref) in enumerate(groups):
        dil = DILATIONS[g]
        nblk = SUPER // dil // Q_BLOCK
        for r in range(dil):
            kc = slice(r * 2 * HW, r * 2 * HW + HW)
            vc = slice(r * 2 * HW + HW, (r + 1) * 2 * HW)
            qc = slice(r * HW, (r + 1) * HW)
            for i in range(nblk):
                rq = slice(i * Q_BLOCK, (i + 1) * Q_BLOCK)
                if i == 0:
                    k2 = jnp.concatenate([kvp_ref[0, :, kc], kv_ref[0, rq, kc]], axis=0)
                    v2 = jnp.concatenate([kvp_ref[0, :, vc], kv_ref[0, rq, vc]], axis=0)
                else:
                    rk = slice((i - 1) * Q_BLOCK, (i + 1) * Q_BLOCK)
                    k2 = kv_ref[0, rk, kc]
                    v2 = kv_ref[0, rk, vc]
                table = bias_first if i == 0 else bias_ref
                pv, m, l = _packed_heads_attention(
                    q_ref[0, rq, qc], k2, v2, hm_ref,
                    lambda h, table=table: table[g, h * Q_BLOCK:(h + 1) * Q_BLOCK, :])
                nat = pl.ds(i * Q_BLOCK * dil + r, Q_BLOCK, stride=dil) if dil > 1 else rq
                for c in range(HW // LANES):
                    lanes = slice(c * LANES, (c + 1) * LANES)
                    pv_nat[g, c, nat, :] = pv[:, lanes]
                    m_nat[g, c, nat, :] = m[:, lanes]
                    l_nat[g, c, nat, :] = l[:, lanes]

    for t in range(SUPER // COMBINE_ROWS):
        rows = slice(t * COMBINE_ROWS, (t + 1) * COMBINE_ROWS)
        for c in range(HW // LANES):
            m1, m2, m3 = m_nat[0, c, rows, :], m_nat[1, c, rows, :], m_nat[2, c, rows, :]
            m = jnp.maximum(jnp.maximum(m1, m2), m3)
            e1, e2, e3 = jnp.exp2(m1 - m), jnp.exp2(m2 - m), jnp.exp2(m3 - m)
            num = e1 * pv_nat[0, c, rows, :] + e2 * pv_nat[1, c, rows, :] + e3 * pv_nat[2, c, rows, :]
            den = e1 * l_nat[0, c, rows, :] + e2 * l_nat[1, c, rows, :] + e3 * l_nat[2, c, rows, :]
            oa_ref[0, rows, c * LANES:(c + 1) * LANES] = (num * (1.0 / den)).astype(BF16)


def _merge_kernel(x_ref, oa_ref, ob_ref, oc_ref, wg_ref, bg_ref, wpa_ref, wpb_ref, wpc_ref,
                  wout_ref, lng_ref, lnb_ref, h_ref):
    for c in range(x_ref.shape[0] // ROW_CHUNK):
        rows = slice(c * ROW_CHUNK, (c + 1) * ROW_CHUNK)
        x = x_ref[rows, :]
        xb = x.astype(BF16)

        def gated(k, o_ref, wp_ref):
            cols = slice(k * D_MODEL, (k + 1) * D_MODEL)
            t = jnp.tanh(jnp.dot(xb, wg_ref[:, cols], preferred_element_type=F32) + bg_ref[:, cols])
            half_p = jnp.dot(o_ref[rows, :], wp_ref[...], preferred_element_type=F32)
            return half_p + half_p * t

        merged = gated(0, oa_ref, wpa_ref) + gated(1, ob_ref, wpb_ref) + gated(2, oc_ref, wpc_ref)
        y = jnp.dot(merged.astype(BF16), wout_ref[...], preferred_element_type=F32)
        h_ref[rows, :] = _layer_norm(ALPHA * x + y, lng_ref[...], lnb_ref[...])


def _ffn_kernel(h_ref, wup_ref, cw_ref, cb_ref, wdn_ref, lng_ref, lnb_ref, o_ref,
                a_buf, carry, g_buf, *, tiles_per_seq):
    tm = h_ref.shape[0]
    n_chunks = D_FF // FF_CHUNK

    @pl.when(pl.program_id(0) % tiles_per_seq == 0)
    def _():
        carry[...] = jnp.zeros(carry.shape, F32)

    h = h_ref[...]
    hb = h.astype(BF16)
    for c in range(n_chunks):
        cols = slice(c * FF_CHUNK, (c + 1) * FF_CHUNK)
        a = jnp.dot(hb, wup_ref[:, cols], preferred_element_type=F32)
        gate = jnp.dot(hb, wup_ref[:, D_FF + c * FF_CHUNK:D_FF + (c + 1) * FF_CHUNK],
                       preferred_element_type=F32)
        a_buf[:CARRY_ROWS] = carry[c]
        a_buf[CARRY_ROWS:] = a
        carry[c] = a[tm - CARRY_ROWS:]
        conv = (cw_ref[0:1, cols] * a_buf[CARRY_ROWS - 2:CARRY_ROWS - 2 + tm]
                + cw_ref[1:2, cols] * a_buf[CARRY_ROWS - 1:CARRY_ROWS - 1 + tm]
                + cw_ref[2:3, cols] * a + cb_ref[:, cols])
        half_cg = conv * gate
        g_buf[:, cols] = (half_cg + half_cg * _gelu_tanh(conv)).astype(BF16)
    y = jnp.dot(g_buf[...], wdn_ref[...], preferred_element_type=F32)
    o_ref[...] = _layer_norm(ALPHA * h + y, lng_ref[...], lnb_ref[...])


def _t5_bucket(dist):
    max_exact = N_BUCKETS // 2
    n = jnp.maximum(dist, 1).astype(jnp.float32)
    large = max_exact + (jnp.log(n / max_exact) / math.log(MAX_DISTANCE / max_exact)
                         * (N_BUCKETS - max_exact)).astype(jnp.int32)
    large = jnp.minimum(large, N_BUCKETS - 1)
    return jnp.where(dist < max_exact, dist, large)


def _offset_bias(rel_bias):
    steps = WINDOW_STEPS - jnp.arange(2 * Q_BLOCK)
    rows = []
    for g, dil in enumerate(DILATIONS):
        bucket = _t5_bucket(jnp.maximum(steps, 0) * dil)
        bias = rel_bias[:, g * HEADS:(g + 1) * HEADS][bucket].astype(F32)
        rows.append(jnp.where(steps[:, None] >= 0, bias, NEG).T)
    return jnp.stack(rows, 0)[:, :, None, :]


def _cparams(*sem):
    return pltpu.CompilerParams(dimension_semantics=sem, vmem_limit_bytes=VMEM_LIMIT)


def _layer(x, mem, band_bias, hmask, w_in, w_mem_kv, sg_ln_g, sg_ln_b, w_spatial, b_spatial,
           w_proj_a, w_proj_b, w_proj_c, w_gate, b_gate, w_out, ln1_g, ln1_b,
           w_ffn_up, conv_w, conv_b, w_ffn_down, ln2_g, ln2_b):
    bsz, seq, _ = x.shape
    n_tok = bsz * seq
    tm = TOKEN_TILE
    assert seq % SUPER == 0 and SUPER % tm == 0 and tm % (SG_CHUNK * 2) == 0
    assert D_FF % FF_CHUNK == 0 and tm % (16 * DILATIONS[-1]) == 0
    n_tiles = n_tok // tm
    tiles_per_seq = seq // tm
    row = lambda v: v.reshape(1, -1)
    tok = lambda width: pl.BlockSpec((tm, width), lambda i: (i, 0))
    tile3 = lambda rows, width: pl.BlockSpec((1, rows, width), lambda b, t: (b, t, 0))

    kvm = pl.pallas_call(
        _memkv_kernel,
        grid=(bsz,),
        in_specs=[pl.BlockSpec((1, MEM_LEN, D_MODEL), lambda b: (b, 0, 0)),
                  _resident((D_MODEL, 2 * HW))],
        out_specs=pl.BlockSpec((1, MEM_LEN, 2 * HW), lambda b: (b, 0, 0)),
        out_shape=jax.ShapeDtypeStruct((bsz, MEM_LEN, 2 * HW), BF16),
        compiler_params=_cparams("parallel"),
        name="memkv",
    )(mem, w_mem_kv.astype(BF16))

    wsp = w_spatial.reshape(SG_W // SG_CHUNK, 2 * SG_CHUNK, SG_CHUNK)
    bsp = jnp.repeat(b_spatial.reshape(SG_W // SG_CHUNK, 2, SG_CHUNK).transpose(0, 2, 1),
                     HEAD_DIM, axis=-1)
    qkv_shapes = ([jax.ShapeDtypeStruct((bsz, seq // d, d * HW), BF16) for d in DILATIONS]
                  + [jax.ShapeDtypeStruct((bsz, seq // d, d * 2 * HW), BF16) for d in DILATIONS])
    qkv_specs = ([tile3(tm // d, d * HW) for d in DILATIONS]
                 + [tile3(tm // d, d * 2 * HW) for d in DILATIONS])
    q1, q2, q3, kv1, kv2, kv3, ob, oc = pl.pallas_call(
        _proj_kernel,
        grid=(bsz, tiles_per_seq),
        in_specs=[tile3(tm, D_MODEL), _resident((D_MODEL, IN_W)),
                  pl.BlockSpec((1, MEM_LEN, 2 * HW), lambda b, t: (b, 0, 0)),
                  _resident((1, SG_W)), _resident((1, SG_W)),
                  _resident(wsp.shape), _resident(bsp.shape), _resident(hmask.shape)],
        out_specs=qkv_specs + [tile3(tm, SG_W), tile3(tm, HW)],
        out_shape=qkv_shapes + [jax.ShapeDtypeStruct((bsz, seq, SG_W), BF16),
                                jax.ShapeDtypeStruct((bsz, seq, HW), BF16)],
        scratch_shapes=[pltpu.VMEM((tm, LANES), F32)],
        compiler_params=_cparams("parallel", "parallel"),
        name="proj",
    )(x, w_in.astype(BF16), kvm, row(sg_ln_g), row(sg_ln_b), wsp, bsp, hmask)

    n_super = seq // SUPER
    super3 = lambda d, width: pl.BlockSpec((1, SUPER // d, d * width), lambda b, j: (b, j, 0))

    def prev3(d):
        nblk = SUPER // d // Q_BLOCK
        return pl.BlockSpec((1, Q_BLOCK, d * 2 * HW),
                            lambda b, j: (b, jnp.maximum(j * nblk - 1, 0), 0))

    oa = pl.pallas_call(
        _attn_kernel,
        grid=(bsz, n_super),
        in_specs=[super3(d, HW) for d in DILATIONS] + [super3(d, 2 * HW) for d in DILATIONS]
        + [prev3(d) for d in DILATIONS] + [_resident(band_bias.shape), _resident(hmask.shape)],
        out_specs=pl.BlockSpec((1, SUPER, HW), lambda b, j: (b, j, 0)),
        out_shape=jax.ShapeDtypeStruct((bsz, seq, HW), BF16),
        scratch_shapes=[pltpu.VMEM((N_GROUPS, HW // LANES, SUPER, LANES), F32)] * 3
        + [pltpu.VMEM((N_GROUPS, HEADS * Q_BLOCK, 2 * Q_BLOCK), F32)],
        compiler_params=_cparams("parallel", "parallel"),
        name="attn",
    )(q1, q2, q3, kv1, kv2, kv3, kv1, kv2, kv3, band_bias, hmask)

    h1 = pl.pallas_call(
        _merge_kernel,
        grid=(n_tiles,),
        in_specs=[tok(D_MODEL), tok(HW), tok(SG_W), tok(HW),
                  _resident((D_MODEL, 3 * D_MODEL)), _resident((1, 3 * D_MODEL)),
                  _resident((HW, D_MODEL)), _resident((SG_W, D_MODEL)), _resident((HW, D_MODEL)),
                  _resident((D_MODEL, D_MODEL)), _resident((1, D_MODEL)), _resident((1, D_MODEL))],
        out_specs=tok(D_MODEL),
        out_shape=jax.ShapeDtypeStruct((n_tok, D_MODEL), F32),
        compiler_params=_cparams("parallel"),
        name="merge",
    )(x.reshape(n_tok, D_MODEL), oa.reshape(n_tok, HW), ob.reshape(n_tok, SG_W),
      oc.reshape(n_tok, HW), (0.5 * w_gate).astype(BF16), row(0.5 * b_gate),
      (0.5 * w_proj_a).astype(BF16), (0.5 * w_proj_b).astype(BF16), (0.5 * w_proj_c).astype(BF16),
      w_out.astype(BF16), row(ln1_g), row(ln1_b))

    up_scale = jnp.where(jnp.arange(2 * D_FF) < D_FF, 1.0, 0.5).astype(F32)
    h2 = pl.pallas_call(
        functools.partial(_ffn_kernel, tiles_per_seq=tiles_per_seq),
        grid=(n_tiles,),
        in_specs=[tok(D_MODEL), _resident((D_MODEL, 2 * D_FF)), _resident(conv_w.shape),
                  _resident((1, D_FF)), _resident((D_FF, D_MODEL)),
                  _resident((1, D_MODEL)), _resident((1, D_MODEL))],
        out_specs=tok(D_MODEL),
        out_shape=jax.ShapeDtypeStruct((n_tok, D_MODEL), F32),
        scratch_shapes=[pltpu.VMEM((CARRY_ROWS + tm, FF_CHUNK), F32),
                        pltpu.VMEM((D_FF // FF_CHUNK, CARRY_ROWS, FF_CHUNK), F32),
                        pltpu.VMEM((tm, D_FF), BF16)],
        compiler_params=_cparams("arbitrary"),
        name="ffn",
    )(h1, (w_ffn_up * up_scale).astype(BF16), conv_w, row(conv_b), w_ffn_down.astype(BF16),
      row(ln2_g), row(ln2_b))
    return h2.reshape(bsz, seq, D_MODEL)


def kernel(x, mem, rel_bias, w_in, w_mem_kv, sg_ln_g, sg_ln_b, w_spatial, b_spatial, w_proj_a,
           w_proj_b, w_proj_c, w_gate, b_gate, w_out, ln1_g, ln1_b, w_ffn_up, conv_w, conv_b,
           w_ffn_down, ln2_g, ln2_b):
    band_bias = pl.pallas_call(
        _band_bias_kernel,
        grid=(N_GROUPS,),
        in_specs=[pl.BlockSpec((1, HEADS, 1, 2 * Q_BLOCK), lambda g: (g, 0, 0, 0))],
        out_specs=pl.BlockSpec((1, HEADS * Q_BLOCK, 2 * Q_BLOCK), lambda g: (g, 0, 0)),
        out_shape=jax.ShapeDtypeStruct((N_GROUPS, HEADS * Q_BLOCK, 2 * Q_BLOCK), F32),
        name="band_bias",
    )(_offset_bias(rel_bias))
    lane = jnp.arange(HW)[None, None, :] // HEAD_DIM
    hmask = jnp.broadcast_to(lane == jnp.arange(HEADS)[:, None, None],
                             (HEADS, Q_BLOCK, HW)).astype(BF16)
    h = x
    for l in range(w_in.shape[0]):
        h = _layer(h, mem, band_bias, hmask, w_in[l], w_mem_kv[l], sg_ln_g[l], sg_ln_b[l],
                   w_spatial[l], b_spatial[l], w_proj_a[l], w_proj_b[l], w_proj_c[l], w_gate[l],
                   b_gate[l], w_out[l], ln1_g[l], ln1_b[l], w_ffn_up[l], conv_w[l], conv_b[l],
                   w_ffn_down[l], ln2_g[l], ln2_b[l])
    return h
```

```python
import functools
import math

import jax
import jax.numpy as jnp
from jax import lax
from jax.experimental import pallas as pl
from jax.experimental.pallas import tpu as pltpu

F32 = jnp.float32
BF16 = jnp.bfloat16

LANES = 128
D_MODEL = 1024
HEAD_DIM = 64
DILATIONS = (1, 4, 16)
N_GROUPS = 3
HEADS = 4
HW = HEADS * HEAD_DIM
ATTN_W = N_GROUPS * HW
WINDOW_STEPS = 128
Q_BLOCK = 128
SG_CHUNK = 128
SG_W = 512
MEM_LEN = 256
IN_W = 3 * ATTN_W + 2 * SG_W + HW
D_FF = 2816
N_BUCKETS = 32
MAX_DISTANCE = 2048
LN_EPS = 1e-5
DEPTH = 1
ALPHA = (2 * DEPTH) ** 0.25
ATTN_SCALE = HEAD_DIM ** -0.5
LOG2E = math.log2(math.e)
Q_SCALE = ATTN_SCALE * LOG2E
NEG = -1e30

TOKEN_TILE = 512
MERGE_TILE = 1024
FFN_TILE = 1024
SUPER = Q_BLOCK * DILATIONS[-1]
COMBINE_ROWS = 256
FF_CHUNK = 256
CARRY_ROWS = 8
VMEM_LIMIT = 56 * 1024 * 1024

_NT = (((1,), (1,)), ((), ()))


def _gelu(x):
    return 0.5 * x * (1.0 + jnp.tanh(math.sqrt(2.0 / math.pi) * (x + 0.044715 * (x * x * x))))


def _layer_norm(x, g, b):
    mu = jnp.mean(x, axis=-1, keepdims=True)
    xc = x - mu
    var = jnp.mean(xc * xc, axis=-1, keepdims=True)
    return xc * lax.rsqrt(var + LN_EPS) * g + b


def _resident(shape):
    zeros = (0,) * len(shape)
    return pl.BlockSpec(shape, lambda *_: zeros, pipeline_mode=pl.Buffered(1))


def _head_dense(per_head):
    low = lax.broadcasted_iota(jnp.int32, (per_head[0].shape[0], LANES), 1) < HEAD_DIM
    return jnp.concatenate([jnp.where(low, per_head[h], per_head[h + 1])
                            for h in range(0, HEADS, 2)], axis=1)


def _packed_heads_attention(q, k, v, hm_ref, bias_fn):
    m_rows = q.shape[0]
    qs = jnp.concatenate([q * hm_ref[h] for h in range(HEADS)], axis=0)
    s = lax.dot_general(qs, k, _NT, preferred_element_type=F32)
    ps, ms, ls = [], [], []
    for h in range(HEADS):
        sh = s[h * m_rows:(h + 1) * m_rows]
        bias = bias_fn(h)
        if bias is not None:
            sh = sh + bias
        m = jnp.max(sh, axis=-1, keepdims=True)
        p = jnp.exp2(sh - m)
        ps.append(p.astype(BF16))
        ms.append(m)
        ls.append(jnp.sum(p, axis=-1, keepdims=True))
    pv = jnp.dot(jnp.concatenate(ps, axis=0), v, preferred_element_type=F32)
    half = lambda h: slice((h // 2) * LANES, (h // 2 + 1) * LANES)
    pv = _head_dense([pv[h * m_rows:(h + 1) * m_rows, half(h)] for h in range(HEADS)])
    return pv, _head_dense(ms), _head_dense(ls)


def _memkv_kernel(mem_ref, w_ref, o_ref):
    o_ref[0] = jnp.dot(mem_ref[0].astype(BF16), w_ref[...],
                       preferred_element_type=F32).astype(BF16)


def _band_bias_kernel(f_ref, o_ref):
    for h in range(HEADS):
        rep = jnp.broadcast_to(f_ref[0, h] * LOG2E, (Q_BLOCK, 2 * Q_BLOCK))
        o_ref[0, h * Q_BLOCK:(h + 1) * Q_BLOCK] = pltpu.roll(rep, 0, 1, stride=1, stride_axis=0)


def _proj_kernel(x_ref, win_ref, kvm_ref, lng_ref, lnb_ref, wsp_ref, bsp_ref, hm_ref,
                 q1_ref, q2_ref, q3_ref, kv1_ref, kv2_ref, kv3_ref, ob_ref, oc_ref, de_buf):
    xb = x_ref[0].astype(BF16)
    tm = xb.shape[0]

    def proj(lo, width):
        return jnp.dot(xb, win_ref[:, lo:lo + width], preferred_element_type=F32)

    def put(ref, lane0, val, dil):
        if dil == 1:
            ref[0, :, lane0:lane0 + HW] = val.astype(BF16)
            return
        stride_cols = ref.shape[2] // dil
        for c in range(HW // LANES):
            de_buf[...] = val[:, c * LANES:(c + 1) * LANES]
            for r in range(dil):
                piece = de_buf[pl.ds(r, tm // dil, stride=dil), :]
                col = r * stride_cols + lane0 + c * LANES
                ref[0, :, col:col + LANES] = piece.astype(BF16)

    for g, (q_ref, kv_ref) in enumerate(((q1_ref, kv1_ref), (q2_ref, kv2_ref), (q3_ref, kv3_ref))):
        put(q_ref, 0, proj(g * HW, HW) * Q_SCALE, DILATIONS[g])
        put(kv_ref, 0, proj(ATTN_W + g * HW, HW), DILATIONS[g])
        put(kv_ref, HW, proj(2 * ATTN_W + g * HW, HW), DILATIONS[g])

    u = _gelu(proj(3 * ATTN_W, SG_W))
    vn = _layer_norm(_gelu(proj(3 * ATTN_W + SG_W, SG_W)), lng_ref[...], lnb_ref[...]).astype(BF16)
    row = lax.broadcasted_iota(jnp.int32, (2 * SG_CHUNK, SG_CHUNK), 0) & (SG_CHUNK - 1)
    col = lax.broadcasted_iota(jnp.int32, (2 * SG_CHUNK, SG_CHUNK), 1)
    low_half = lax.broadcasted_iota(jnp.int32, (SG_CHUNK, SG_CHUNK), 1) < HEAD_DIM
    n_chunks = tm // SG_CHUNK
    for j in range(SG_W // SG_CHUNK):
        cols = slice(j * SG_CHUNK, (j + 1) * SG_CHUNK)
        w_pair = jnp.where(row >= col, wsp_ref[j], 0.0).astype(BF16)
        rhs = jnp.concatenate([vn[c * SG_CHUNK:(c + 1) * SG_CHUNK, cols] for c in range(n_chunks)],
                              axis=1)
        r = jnp.dot(w_pair, rhs, preferred_element_type=F32)
        for c in range(n_chunks):
            rows = slice(c * SG_CHUNK, (c + 1) * SG_CHUNK)
            sv = jnp.where(low_half, r[:SG_CHUNK, rows], r[SG_CHUNK:, rows]) + bsp_ref[j]
            ob_ref[0, rows, cols] = (u[rows, cols] * sv).astype(BF16)

    qc = (proj(3 * ATTN_W + 2 * SG_W, HW) * Q_SCALE).astype(BF16)
    for c in range(tm // Q_BLOCK):
        rows = slice(c * Q_BLOCK, (c + 1) * Q_BLOCK)
        pv, _, l = _packed_heads_attention(qc[rows], kvm_ref[0, :, :HW], kvm_ref[0, :, HW:],
                                           hm_ref, lambda h: None)
        oc_ref[0, rows, :] = (pv * (1.0 / l)).astype(BF16)


def _attn_kernel(q1_ref, q2_ref, q3_ref, kv1_ref, kv2_ref, kv3_ref, kp1_ref, kp2_ref, kp3_ref,
                 bias_ref, hm_ref, oa_ref, pv_nat, m_nat, l_nat, bias_first):
    lane = lax.broadcasted_iota(jnp.int32, (HEADS * Q_BLOCK, 2 * Q_BLOCK), 1)
    no_prev = jnp.where(lane < Q_BLOCK, jnp.where(pl.program_id(1) == 0, NEG, 0.0), 0.0)
    for g in range(N_GROUPS):
        bias_first[g] = bias_ref[g] + no_prev

    groups = ((q1_ref, kv1_ref, kp1_ref), (q2_ref, kv2_ref, kp2_ref), (q3_ref, kv3_ref, kp3_ref))
    for g, (q_ref, kv_ref, kvp_ref) in enumerate(groups):
        dil = DILATIONS[g]
        nblk = SUPER // dil // Q_BLOCK
        for r in range(dil):
            kc = slice(r * 2 * HW, r * 2 * HW + HW)
            vc = slice(r * 2 * HW + HW, (r + 1) * 2 * HW)
            qc = slice(r * HW, (r + 1) * HW)
            for i in range(nblk):
                rq = slice(i * Q_BLOCK, (i + 1) * Q_BLOCK)
                if i == 0:
                    k2 = jnp.concatenate([kvp_ref[0, :, kc], kv_ref[0, rq, kc]], axis=0)
                    v2 = jnp.concatenate([kvp_ref[0, :, vc], kv_ref[0, rq, vc]], axis=0)
                else:
                    rk = slice((i - 1) * Q_BLOCK, (i + 1) * Q_BLOCK)
                    k2 = kv_ref[0, rk, kc]
                    v2 = kv_ref[0, rk, vc]
                table = bias_first if i == 0 else bias_ref
                pv, m, l = _packed_heads_attention(
                    q_ref[0, rq, qc], k2, v2, hm_ref,
                    lambda h, table=table: table[g, h * Q_BLOCK:(h + 1) * Q_BLOCK, :])
                nat = pl.ds(i * Q_BLOCK * dil + r, Q_BLOCK, stride=dil) if dil > 1 else rq
                for c in range(HW // LANES):
                    lanes = slice(c * LANES, (c + 1) * LANES)
                    pv_nat[g, c, nat, :] = pv[:, lanes]
                    m_nat[g, c, nat, :] = m[:, lanes]
                    l_nat[g, c, nat, :] = l[:, lanes]

    for t in range(SUPER // COMBINE_ROWS):
        rows = slice(t * COMBINE_ROWS, (t + 1) * COMBINE_ROWS)
        for c in range(HW // LANES):
            m1, m2, m3 = m_nat[0, c, rows, :], m_nat[1, c, rows, :], m_nat[2, c, rows, :]
            m = jnp.maximum(jnp.maximum(m1, m2), m3)
            e1, e2, e3 = jnp.exp2(m1 - m), jnp.exp2(m2 - m), jnp.exp2(m3 - m)
            num = e1 * pv_nat[0, c, rows, :] + e2 * pv_nat[1, c, rows, :] + e3 * pv_nat[2, c, rows, :]
            den = e1 * l_nat[0, c, rows, :] + e2 * l_nat[1, c, rows, :] + e3 * l_nat[2, c, rows, :]
            oa_ref[0, rows, c * LANES:(c + 1) * LANES] = (num * (1.0 / den)).astype(BF16)


def _merge_kernel(x_ref, oa_ref, ob_ref, oc_ref, wg_ref, bg_ref, wpa_ref, wpb_ref, wpc_ref,
                  wout_ref, lng_ref, lnb_ref, h_ref):
    x = x_ref[...]
    xb = x.astype(BF16)

    def gated(k, o_ref, wp_ref):
        cols = slice(k * D_MODEL, (k + 1) * D_MODEL)
        t = jnp.tanh(jnp.dot(xb, wg_ref[:, cols], preferred_element_type=F32) + bg_ref[:, cols])
        half_p = jnp.dot(o_ref[...], wp_ref[...], preferred_element_type=F32)
        return half_p + half_p * t

    merged = gated(0, oa_ref, wpa_ref) + gated(1, ob_ref, wpb_ref) + gated(2, oc_ref, wpc_ref)
    y = jnp.dot(merged.astype(BF16), wout_ref[...], preferred_element_type=F32)
    h_ref[...] = _layer_norm(ALPHA * x + y, lng_ref[...], lnb_ref[...])


def _ffn_kernel(h_ref, wup_ref, cw_ref, cb_ref, wdn_ref, lng_ref, lnb_ref, o_ref,
                a_buf, carry, g_buf, *, tiles_per_seq):
    tm = h_ref.shape[0]
    n_chunks = D_FF // FF_CHUNK

    @pl.when(pl.program_id(0) % tiles_per_seq == 0)
    def _():
        carry[...] = jnp.zeros(carry.shape, F32)

    h = h_ref[...]
    hb = h.astype(BF16)
    for c in range(n_chunks):
        cols = slice(c * FF_CHUNK, (c + 1) * FF_CHUNK)
        a = jnp.dot(hb, wup_ref[:, cols], preferred_element_type=F32)
        gate = jnp.dot(hb, wup_ref[:, D_FF + c * FF_CHUNK:D_FF + (c + 1) * FF_CHUNK],
                       preferred_element_type=F32)
        a_buf[:CARRY_ROWS] = carry[c]
        a_buf[CARRY_ROWS:] = a
        carry[c] = a[tm - CARRY_ROWS:]
        conv = (cw_ref[0:1, cols] * a_buf[CARRY_ROWS - 2:CARRY_ROWS - 2 + tm]
                + cw_ref[1:2, cols] * a_buf[CARRY_ROWS - 1:CARRY_ROWS - 1 + tm]
                + cw_ref[2:3, cols] * a + cb_ref[:, cols])
        g_buf[:, cols] = (_gelu(conv) * gate).astype(BF16)
    y = jnp.dot(g_buf[...], wdn_ref[...], preferred_element_type=F32)
    o_ref[...] = _layer_norm(ALPHA * h + y, lng_ref[...], lnb_ref[...])


def _t5_bucket(dist):
    max_exact = N_BUCKETS // 2
    n = jnp.maximum(dist, 1).astype(jnp.float32)
    large = max_exact + (jnp.log(n / max_exact) / math.log(MAX_DISTANCE / max_exact)
                         * (N_BUCKETS - max_exact)).astype(jnp.int32)
    large = jnp.minimum(large, N_BUCKETS - 1)
    return jnp.where(dist < max_exact, dist, large)


def _offset_bias(rel_bias):
    steps = WINDOW_STEPS - jnp.arange(2 * Q_BLOCK)
    rows = []
    for g, dil in enumerate(DILATIONS):
        bucket = _t5_bucket(jnp.maximum(steps, 0) * dil)
        bias = rel_bias[:, g * HEADS:(g + 1) * HEADS][bucket].astype(F32)
        rows.append(jnp.where(steps[:, None] >= 0, bias, NEG).T)
    return jnp.stack(rows, 0)[:, :, None, :]


def _cparams(*sem):
    return pltpu.CompilerParams(dimension_semantics=sem, vmem_limit_bytes=VMEM_LIMIT)


def _layer(x, mem, band_bias, hmask, w_in, w_mem_kv, sg_ln_g, sg_ln_b, w_spatial, b_spatial,
           w_proj_a, w_proj_b, w_proj_c, w_gate, b_gate, w_out, ln1_g, ln1_b,
           w_ffn_up, conv_w, conv_b, w_ffn_down, ln2_g, ln2_b):
    bsz, seq, _ = x.shape
    n_tok = bsz * seq
    tm = TOKEN_TILE
    assert seq % SUPER == 0 and SUPER % tm == 0 and tm % (SG_CHUNK * 2) == 0
    assert D_FF % FF_CHUNK == 0 and tm % (16 * DILATIONS[-1]) == 0
    assert seq % MERGE_TILE == 0 and seq % FFN_TILE == 0
    tiles_per_seq = seq // tm
    row = lambda v: v.reshape(1, -1)
    tok = lambda rows, width: pl.BlockSpec((rows, width), lambda i: (i, 0))
    tile3 = lambda rows, width: pl.BlockSpec((1, rows, width), lambda b, t: (b, t, 0))

    kvm = pl.pallas_call(
        _memkv_kernel,
        grid=(bsz,),
        in_specs=[pl.BlockSpec((1, MEM_LEN, D_MODEL), lambda b: (b, 0, 0)),
                  _resident((D_MODEL, 2 * HW))],
        out_specs=pl.BlockSpec((1, MEM_LEN, 2 * HW), lambda b: (b, 0, 0)),
        out_shape=jax.ShapeDtypeStruct((bsz, MEM_LEN, 2 * HW), BF16),
        compiler_params=_cparams("parallel"),
        name="memkv",
    )(mem, w_mem_kv.astype(BF16))

    wsp = w_spatial.reshape(SG_W // SG_CHUNK, 2 * SG_CHUNK, SG_CHUNK)
    bsp = jnp.repeat(b_spatial.reshape(SG_W // SG_CHUNK, 2, SG_CHUNK).transpose(0, 2, 1),
                     HEAD_DIM, axis=-1)
    qkv_shapes = ([jax.ShapeDtypeStruct((bsz, seq // d, d * HW), BF16) for d in DILATIONS]
                  + [jax.ShapeDtypeStruct((bsz, seq // d, d * 2 * HW), BF16) for d in DILATIONS])
    qkv_specs = ([tile3(tm // d, d * HW) for d in DILATIONS]
                 + [tile3(tm // d, d * 2 * HW) for d in DILATIONS])
    q1, q2, q3, kv1, kv2, kv3, ob, oc = pl.pallas_call(
        _proj_kernel,
        grid=(bsz, tiles_per_seq),
        in_specs=[tile3(tm, D_MODEL), _resident((D_MODEL, IN_W)),
                  pl.BlockSpec((1, MEM_LEN, 2 * HW), lambda b, t: (b, 0, 0)),
                  _resident((1, SG_W)), _resident((1, SG_W)),
                  _resident(wsp.shape), _resident(bsp.shape), _resident(hmask.shape)],
        out_specs=qkv_specs + [tile3(tm, SG_W), tile3(tm, HW)],
        out_shape=qkv_shapes + [jax.ShapeDtypeStruct((bsz, seq, SG_W), BF16),
                                jax.ShapeDtypeStruct((bsz, seq, HW), BF16)],
        scratch_shapes=[pltpu.VMEM((tm, LANES), F32)],
        compiler_params=_cparams("parallel", "parallel"),
        name="proj",
    )(x, w_in.astype(BF16), kvm, row(sg_ln_g), row(sg_ln_b), wsp, bsp, hmask)

    n_super = seq // SUPER
    super3 = lambda d, width: pl.BlockSpec((1, SUPER // d, d * width), lambda b, j: (b, j, 0))

    def prev3(d):
        nblk = SUPER // d // Q_BLOCK
        return pl.BlockSpec((1, Q_BLOCK, d * 2 * HW),
                            lambda b, j: (b, jnp.maximum(j * nblk - 1, 0), 0))

    oa = pl.pallas_call(
        _attn_kernel,
        grid=(bsz, n_super),
        in_specs=[super3(d, HW) for d in DILATIONS] + [super3(d, 2 * HW) for d in DILATIONS]
        + [prev3(d) for d in DILATIONS] + [_resident(band_bias.shape), _resident(hmask.shape)],
        out_specs=pl.BlockSpec((1, SUPER, HW), lambda b, j: (b, j, 0)),
        out_shape=jax.ShapeDtypeStruct((bsz, seq, HW), BF16),
        scratch_shapes=[pltpu.VMEM((N_GROUPS, HW // LANES, SUPER, LANES), F32)] * 3
        + [pltpu.VMEM((N_GROUPS, HEADS * Q_BLOCK, 2 * Q_BLOCK), F32)],
        compiler_params=_cparams("parallel", "parallel"),
        name="attn",
    )(q1, q2, q3, kv1, kv2, kv3, kv1, kv2, kv3, band_bias, hmask)

    h1 = pl.pallas_call(
        _merge_kernel,
        grid=(n_tok // MERGE_TILE,),
        in_specs=[tok(MERGE_TILE, D_MODEL), tok(MERGE_TILE, HW), tok(MERGE_TILE, SG_W),
                  tok(MERGE_TILE, HW),
                  _resident((D_MODEL, 3 * D_MODEL)), _resident((1, 3 * D_MODEL)),
                  _resident((HW, D_MODEL)), _resident((SG_W, D_MODEL)), _resident((HW, D_MODEL)),
                  _resident((D_MODEL, D_MODEL)), _resident((1, D_MODEL)), _resident((1, D_MODEL))],
        out_specs=tok(MERGE_TILE, D_MODEL),
        out_shape=jax.ShapeDtypeStruct((n_tok, D_MODEL), F32),
        compiler_params=_cparams("parallel"),
        name="merge",
    )(x.reshape(n_tok, D_MODEL), oa.reshape(n_tok, HW), ob.reshape(n_tok, SG_W),
      oc.reshape(n_tok, HW), (0.5 * w_gate).astype(BF16), row(0.5 * b_gate),
      (0.5 * w_proj_a).astype(BF16), (0.5 * w_proj_b).astype(BF16), (0.5 * w_proj_c).astype(BF16),
      w_out.astype(BF16), row(ln1_g), row(ln1_b))

    h2 = pl.pallas_call(
        functools.partial(_ffn_kernel, tiles_per_seq=seq // FFN_TILE),
        grid=(n_tok // FFN_TILE,),
        in_specs=[tok(FFN_TILE, D_MODEL), _resident((D_MODEL, 2 * D_FF)), _resident(conv_w.shape),
                  _resident((1, D_FF)), _resident((D_FF, D_MODEL)),
                  _resident((1, D_MODEL)), _resident((1, D_MODEL))],
        out_specs=tok(FFN_TILE, D_MODEL),
        out_shape=jax.ShapeDtypeStruct((n_tok, D_MODEL), F32),
        scratch_shapes=[pltpu.VMEM((CARRY_ROWS + FFN_TILE, FF_CHUNK), F32),
                        pltpu.VMEM((D_FF // FF_CHUNK, CARRY_ROWS, FF_CHUNK), F32),
                        pltpu.VMEM((FFN_TILE, D_FF), BF16)],
        compiler_params=_cparams("arbitrary"),
        name="ffn",
    )(h1, w_ffn_up.astype(BF16), conv_w, row(conv_b), w_ffn_down.astype(BF16),
      row(ln2_g), row(ln2_b))
    return h2.reshape(bsz, seq, D_MODEL)


def kernel(x, mem, rel_bias, w_in, w_mem_kv, sg_ln_g, sg_ln_b, w_spatial, b_spatial, w_proj_a,
           w_proj_b, w_proj_c, w_gate, b_gate, w_out, ln1_g, ln1_b, w_ffn_up, conv_w, conv_b,
           w_ffn_down, ln2_g, ln2_b):
    band_bias = pl.pallas_call(
        _band_bias_kernel,
        grid=(N_GROUPS,),
        in_specs=[pl.BlockSpec((1, HEADS, 1, 2 * Q_BLOCK), lambda g: (g, 0, 0, 0))],
        out_specs=pl.BlockSpec((1, HEADS * Q_BLOCK, 2 * Q_BLOCK), lambda g: (g, 0, 0)),
        out_shape=jax.ShapeDtypeStruct((N_GROUPS, HEADS * Q_BLOCK, 2 * Q_BLOCK), F32),
        name="band_bias",
    )(_offset_bias(rel_bias))
    lane = jnp.arange(HW)[None, None, :] // HEAD_DIM
    hmask = jnp.broadcast_to(lane == jnp.arange(HEADS)[:, None, None],
                             (HEADS, Q_BLOCK, HW)).astype(BF16)
    h = x
    for l in range(w_in.shape[0]):
        h = _layer(h, mem, band_bias, hmask, w_in[l], w_mem_kv[l], sg_ln_g[l], sg_ln_b[l],
                   w_spatial[l], b_spatial[l], w_proj_a[l], w_proj_b[l], w_proj_c[l], w_gate[l],
                   b_gate[l], w_out[l], ln1_g[l], ln1_b[l], w_ffn_up[l], conv_w[l], conv_b[l],
                   w_ffn_down[l], ln2_g[l], ln2_b[l])
    return h
```

```python
import functools
import math

import jax
import jax.numpy as jnp
from jax import lax
from jax.experimental import pallas as pl
from jax.experimental.pallas import tpu as pltpu

F32 = jnp.float32
BF16 = jnp.bfloat16

LANES = 128
D_MODEL = 1024
HEAD_DIM = 64
DILATIONS = (1, 4, 16)
N_GROUPS = 3
HEADS = 4
HW = HEADS * HEAD_DIM
ATTN_W = N_GROUPS * HW
WINDOW_STEPS = 128
Q_BLOCK = 128
SG_CHUNK = 128
SG_W = 512
MEM_LEN = 256
IN_W = 3 * ATTN_W + 2 * SG_W + HW
D_FF = 2816
N_BUCKETS = 32
MAX_DISTANCE = 2048
LN_EPS = 1e-5
DEPTH = 1
ALPHA = (2 * DEPTH) ** 0.25
ATTN_SCALE = HEAD_DIM ** -0.5
LOG2E = math.log2(math.e)
Q_SCALE = ATTN_SCALE * LOG2E
NEG = -1e30

TOKEN_TILE = 512
MERGE_TILE = 1024
FFN_TILE = 512
SUPER = Q_BLOCK * DILATIONS[-1]
COMBINE_ROWS = 256
FF_CHUNK = 256
CARRY_ROWS = 8
VMEM_LIMIT = 56 * 1024 * 1024

_NT = (((1,), (1,)), ((), ()))


def _gelu(x):
    return 0.5 * x * (1.0 + jnp.tanh(math.sqrt(2.0 / math.pi) * (x + 0.044715 * (x * x * x))))


def _layer_norm(x, g, b):
    mu = jnp.mean(x, axis=-1, keepdims=True)
    xc = x - mu
    var = jnp.mean(xc * xc, axis=-1, keepdims=True)
    return xc * lax.rsqrt(var + LN_EPS) * g + b


def _resident(shape):
    zeros = (0,) * len(shape)
    return pl.BlockSpec(shape, lambda *_: zeros, pipeline_mode=pl.Buffered(1))


def _head_dense(per_head):
    low = lax.broadcasted_iota(jnp.int32, (per_head[0].shape[0], LANES), 1) < HEAD_DIM
    return jnp.concatenate([jnp.where(low, per_head[h], per_head[h + 1])
                            for h in range(0, HEADS, 2)], axis=1)


def _packed_heads_attention(q, k, v, hm_ref, bias_fn):
    m_rows = q.shape[0]
    qs = jnp.concatenate([q * hm_ref[h] for h in range(HEADS)], axis=0)
    s = lax.dot_general(qs, k, _NT, preferred_element_type=F32)
    ps, ms, ls = [], [], []
    for h in range(HEADS):
        sh = s[h * m_rows:(h + 1) * m_rows]
        bias = bias_fn(h)
        if bias is not None:
            sh = sh + bias
        m = jnp.max(sh, axis=-1, keepdims=True)
        p = jnp.exp2(sh - m)
        ps.append(p.astype(BF16))
        ms.append(m)
        ls.append(jnp.sum(p, axis=-1, keepdims=True))
    pv = jnp.dot(jnp.concatenate(ps, axis=0), v, preferred_element_type=F32)
    half = lambda h: slice((h // 2) * LANES, (h // 2 + 1) * LANES)
    pv = _head_dense([pv[h * m_rows:(h + 1) * m_rows, half(h)] for h in range(HEADS)])
    return pv, _head_dense(ms), _head_dense(ls)


def _band_bias_kernel(f_ref, o_ref):
    for h in range(HEADS):
        rep = jnp.broadcast_to(f_ref[0, h] * LOG2E, (Q_BLOCK, 2 * Q_BLOCK))
        o_ref[0, h * Q_BLOCK:(h + 1) * Q_BLOCK] = pltpu.roll(rep, 0, 1, stride=1, stride_axis=0)


def _proj_kernel(x_ref, win_ref, mem_ref, wkv_ref, lng_ref, lnb_ref, wsp_ref, bsp_ref, hm_ref,
                 q1_ref, q2_ref, q3_ref, kv1_ref, kv2_ref, kv3_ref, ob_ref, oc_ref, de_buf, kvm):
    xb = x_ref[0].astype(BF16)
    tm = xb.shape[0]

    @pl.when(pl.program_id(1) == 0)
    def _():
        kvm[...] = jnp.dot(mem_ref[0].astype(BF16), wkv_ref[...],
                           preferred_element_type=F32).astype(BF16)

    def proj(lo, width):
        return jnp.dot(xb, win_ref[:, lo:lo + width], preferred_element_type=F32)

    def put(ref, lane0, val, dil):
        if dil == 1:
            ref[0, :, lane0:lane0 + HW] = val.astype(BF16)
            return
        stride_cols = ref.shape[2] // dil
        for c in range(HW // LANES):
            de_buf[...] = val[:, c * LANES:(c + 1) * LANES]
            for r in range(dil):
                piece = de_buf[pl.ds(r, tm // dil, stride=dil), :]
                col = r * stride_cols + lane0 + c * LANES
                ref[0, :, col:col + LANES] = piece.astype(BF16)

    def a_proj(g, part):
        q_ref, kv_ref = ((q1_ref, kv1_ref), (q2_ref, kv2_ref), (q3_ref, kv3_ref))[g]
        if part == 0:
            put(q_ref, 0, proj(g * HW, HW) * Q_SCALE, DILATIONS[g])
        else:
            put(kv_ref, (part - 1) * HW, proj(part * ATTN_W + g * HW, HW), DILATIONS[g])

    fillers = iter([(g, part) for g in range(N_GROUPS) for part in range(3)])
    fill = lambda: a_proj(*next(fillers))

    def proj_halves(lo):
        return jnp.concatenate([proj(lo, HW), proj(lo + HW, HW)], axis=1)

    qc = (proj(3 * ATTN_W + 2 * SG_W, HW) * Q_SCALE).astype(BF16)
    u_pre = proj_halves(3 * ATTN_W)
    v_pre = proj_halves(3 * ATTN_W + SG_W)

    n_chunks = tm // SG_CHUNK
    u, vn = [], []
    for c in range(n_chunks):
        rows = slice(c * SG_CHUNK, (c + 1) * SG_CHUNK)
        fill()
        u.append(_gelu(u_pre[rows]))
        vn.append(_layer_norm(_gelu(v_pre[rows]), lng_ref[...], lnb_ref[...]).astype(BF16))
        pv, _, l = _packed_heads_attention(qc[rows], kvm[:, :HW], kvm[:, HW:], hm_ref,
                                           lambda h: None)
        oc_ref[0, rows, :] = (pv * (1.0 / l)).astype(BF16)
        fill()

    row = lax.broadcasted_iota(jnp.int32, (2 * SG_CHUNK, SG_CHUNK), 0) & (SG_CHUNK - 1)
    col = lax.broadcasted_iota(jnp.int32, (2 * SG_CHUNK, SG_CHUNK), 1)
    low_half = lax.broadcasted_iota(jnp.int32, (SG_CHUNK, SG_CHUNK), 1) < HEAD_DIM
    for j in range(SG_W // SG_CHUNK):
        cols = slice(j * SG_CHUNK, (j + 1) * SG_CHUNK)
        w_pair = jnp.where(row >= col, wsp_ref[j], 0.0).astype(BF16)
        rhs = jnp.concatenate([vn[c][:, cols] for c in range(n_chunks)], axis=1)
        r = jnp.dot(w_pair, rhs, preferred_element_type=F32)
        for c in range(n_chunks):
            rows = slice(c * SG_CHUNK, (c + 1) * SG_CHUNK)
            sv = jnp.where(low_half, r[:SG_CHUNK, rows], r[SG_CHUNK:, rows]) + bsp_ref[j]
            ob_ref[0, rows, cols] = (u[c][:, cols] * sv).astype(BF16)
        if j == 0:
            fill()


def _attn_kernel(q1_ref, q2_ref, q3_ref, kv1_ref, kv2_ref, kv3_ref, kp1_ref, kp2_ref, kp3_ref,
                 bias_ref, hm_ref, oa_ref, pv_nat, m_nat, l_nat, bias_first):
    lane = lax.broadcasted_iota(jnp.int32, (HEADS * Q_BLOCK, 2 * Q_BLOCK), 1)
    no_prev = jnp.where(lane < Q_BLOCK, jnp.where(pl.program_id(1) == 0, NEG, 0.0), 0.0)
    for g in range(N_GROUPS):
        bias_first[g] = bias_ref[g] + no_prev

    groups = ((q1_ref, kv1_ref, kp1_ref), (q2_ref, kv2_ref, kp2_ref), (q3_ref, kv3_ref, kp3_ref))
    for g, (q_ref, kv_ref, kvp_ref) in enumerate(groups):
        dil = DILATIONS[g]
        nblk = SUPER // dil // Q_BLOCK
        for r in range(dil):
            kc = slice(r * 2 * HW, r * 2 * HW + HW)
            vc = slice(r * 2 * HW + HW, (r + 1) * 2 * HW)
            qc = slice(r * HW, (r + 1) * HW)
            for i in range(nblk):
                rq = slice(i * Q_BLOCK, (i + 1) * Q_BLOCK)
                if i == 0:
                    k2 = jnp.concatenate([kvp_ref[0, :, kc], kv_ref[0, rq, kc]], axis=0)
                    v2 = jnp.concatenate([kvp_ref[0, :, vc], kv_ref[0, rq, vc]], axis=0)
                else:
                    rk = slice((i - 1) * Q_BLOCK, (i + 1) * Q_BLOCK)
                    k2 = kv_ref[0, rk, kc]
                    v2 = kv_ref[0, rk, vc]
                table = bias_first if i == 0 else bias_ref
                pv, m, l = _packed_heads_attention(
                    q_ref[0, rq, qc], k2, v2, hm_ref,
                    lambda h, table=table: table[g, h * Q_BLOCK:(h + 1) * Q_BLOCK, :])
                nat = pl.ds(i * Q_BLOCK * dil + r, Q_BLOCK, stride=dil) if dil > 1 else rq
                for c in range(HW // LANES):
                    lanes = slice(c * LANES, (c + 1) * LANES)
                    pv_nat[g, c, nat, :] = pv[:, lanes]
                    m_nat[g, c, nat, :] = m[:, lanes]
                    l_nat[g, c, nat, :] = l[:, lanes]

    for t in range(SUPER // COMBINE_ROWS):
        rows = slice(t * COMBINE_ROWS, (t + 1) * COMBINE_ROWS)
        for c in range(HW // LANES):
            m1, m2, m3 = m_nat[0, c, rows, :], m_nat[1, c, rows, :], m_nat[2, c, rows, :]
            m = jnp.maximum(jnp.maximum(m1, m2), m3)
            e1, e2, e3 = jnp.exp2(m1 - m), jnp.exp2(m2 - m), jnp.exp2(m3 - m)
            num = e1 * pv_nat[0, c, rows, :] + e2 * pv_nat[1, c, rows, :] + e3 * pv_nat[2, c, rows, :]
            den = e1 * l_nat[0, c, rows, :] + e2 * l_nat[1, c, rows, :] + e3 * l_nat[2, c, rows, :]
            oa_ref[0, rows, c * LANES:(c + 1) * LANES] = (num * (1.0 / den)).astype(BF16)


def _merge_kernel(x_ref, oa_ref, ob_ref, oc_ref, wg_ref, bg_ref, wpa_ref, wpb_ref, wpc_ref,
                  wout_ref, lng_ref, lnb_ref, h_ref):
    x = x_ref[...]
    xb = x.astype(BF16)

    def gated(k, o_ref, wp_ref):
        cols = slice(k * D_MODEL, (k + 1) * D_MODEL)
        t = jnp.tanh(jnp.dot(xb, wg_ref[:, cols], preferred_element_type=F32) + bg_ref[:, cols])
        half_p = jnp.dot(o_ref[...], wp_ref[...], preferred_element_type=F32)
        return half_p + half_p * t

    merged = gated(0, oa_ref, wpa_ref) + gated(1, ob_ref, wpb_ref) + gated(2, oc_ref, wpc_ref)
    y = jnp.dot(merged.astype(BF16), wout_ref[...], preferred_element_type=F32)
    h_ref[...] = _layer_norm(ALPHA * x + y, lng_ref[...], lnb_ref[...])


def _ffn_kernel(h_ref, wup_ref, cw_ref, cb_ref, wdn_ref, lng_ref, lnb_ref, o_ref,
                a_buf, carry, g_buf, *, tiles_per_seq):
    tm = h_ref.shape[0]
    n_chunks = D_FF // FF_CHUNK

    @pl.when(pl.program_id(0) % tiles_per_seq == 0)
    def _():
        carry[...] = jnp.zeros(carry.shape, F32)

    h = h_ref[...]
    hb = h.astype(BF16)
    for c in range(n_chunks):
        cols = slice(c * FF_CHUNK, (c + 1) * FF_CHUNK)
        a = jnp.dot(hb, wup_ref[:, cols], preferred_element_type=F32)
        gate = jnp.dot(hb, wup_ref[:, D_FF + c * FF_CHUNK:D_FF + (c + 1) * FF_CHUNK],
                       preferred_element_type=F32)
        a_buf[:CARRY_ROWS] = carry[c]
        a_buf[CARRY_ROWS:] = a
        carry[c] = a[tm - CARRY_ROWS:]
        conv = (cw_ref[0:1, cols] * a_buf[CARRY_ROWS - 2:CARRY_ROWS - 2 + tm]
                + cw_ref[1:2, cols] * a_buf[CARRY_ROWS - 1:CARRY_ROWS - 1 + tm]
                + cw_ref[2:3, cols] * a + cb_ref[:, cols])
        g_buf[:, cols] = (_gelu(conv) * gate).astype(BF16)
    y = jnp.dot(g_buf[...], wdn_ref[...], preferred_element_type=F32)
    o_ref[...] = _layer_norm(ALPHA * h + y, lng_ref[...], lnb_ref[...])


def _t5_bucket(dist):
    max_exact = N_BUCKETS // 2
    n = jnp.maximum(dist, 1).astype(jnp.float32)
    large = max_exact + (jnp.log(n / max_exact) / math.log(MAX_DISTANCE / max_exact)
                         * (N_BUCKETS - max_exact)).astype(jnp.int32)
    large = jnp.minimum(large, N_BUCKETS - 1)
    return jnp.where(dist < max_exact, dist, large)


def _offset_bias(rel_bias):
    steps = WINDOW_STEPS - jnp.arange(2 * Q_BLOCK)
    rows = []
    for g, dil in enumerate(DILATIONS):
        bucket = _t5_bucket(jnp.maximum(steps, 0) * dil)
        bias = rel_bias[:, g * HEADS:(g + 1) * HEADS][bucket].astype(F32)
        rows.append(jnp.where(steps[:, None] >= 0, bias, NEG).T)
    return jnp.stack(rows, 0)[:, :, None, :]


def _cparams(*sem):
    return pltpu.CompilerParams(dimension_semantics=sem, vmem_limit_bytes=VMEM_LIMIT)


def _layer(x, mem, band_bias, hmask, w_in, w_mem_kv, sg_ln_g, sg_ln_b, w_spatial, b_spatial,
           w_proj_a, w_proj_b, w_proj_c, w_gate, b_gate, w_out, ln1_g, ln1_b,
           w_ffn_up, conv_w, conv_b, w_ffn_down, ln2_g, ln2_b):
    bsz, seq, _ = x.shape
    n_tok = bsz * seq
    tm = TOKEN_TILE
    assert seq % SUPER == 0 and SUPER % tm == 0 and tm % (SG_CHUNK * 2) == 0
    assert D_FF % FF_CHUNK == 0 and tm % (16 * DILATIONS[-1]) == 0
    assert seq % MERGE_TILE == 0 and seq % FFN_TILE == 0
    tiles_per_seq = seq // tm
    row = lambda v: v.reshape(1, -1)
    tok = lambda rows, width: pl.BlockSpec((rows, width), lambda i: (i, 0))
    tile3 = lambda rows, width: pl.BlockSpec((1, rows, width), lambda b, t: (b, t, 0))

    wsp = w_spatial.reshape(SG_W // SG_CHUNK, 2 * SG_CHUNK, SG_CHUNK)
    bsp = jnp.repeat(b_spatial.reshape(SG_W // SG_CHUNK, 2, SG_CHUNK).transpose(0, 2, 1),
                     HEAD_DIM, axis=-1)
    qkv_shapes = ([jax.ShapeDtypeStruct((bsz, seq // d, d * HW), BF16) for d in DILATIONS]
                  + [jax.ShapeDtypeStruct((bsz, seq // d, d * 2 * HW), BF16) for d in DILATIONS])
    qkv_specs = ([tile3(tm // d, d * HW) for d in DILATIONS]
                 + [tile3(tm // d, d * 2 * HW) for d in DILATIONS])
    q1, q2, q3, kv1, kv2, kv3, ob, oc = pl.pallas_call(
        _proj_kernel,
        grid=(bsz, tiles_per_seq),
        in_specs=[tile3(tm, D_MODEL), _resident((D_MODEL, IN_W)),
                  pl.BlockSpec((1, MEM_LEN, D_MODEL), lambda b, t: (b, 0, 0)),
                  _resident((D_MODEL, 2 * HW)), _resident((1, SG_W)), _resident((1, SG_W)),
                  _resident(wsp.shape), _resident(bsp.shape), _resident(hmask.shape)],
        out_specs=qkv_specs + [tile3(tm, SG_W), tile3(tm, HW)],
        out_shape=qkv_shapes + [jax.ShapeDtypeStruct((bsz, seq, SG_W), BF16),
                                jax.ShapeDtypeStruct((bsz, seq, HW), BF16)],
        scratch_shapes=[pltpu.VMEM((tm, LANES), F32), pltpu.VMEM((MEM_LEN, 2 * HW), BF16)],
        compiler_params=_cparams("parallel", "arbitrary"),
        name="proj",
    )(x, w_in.astype(BF16), mem, w_mem_kv.astype(BF16), row(sg_ln_g), row(sg_ln_b), wsp, bsp,
      hmask)

    n_super = seq // SUPER
    super3 = lambda d, width: pl.BlockSpec((1, SUPER // d, d * width), lambda b, j: (b, j, 0))

    def prev3(d):
        nblk = SUPER // d // Q_BLOCK
        return pl.BlockSpec((1, Q_BLOCK, d * 2 * HW),
                            lambda b, j: (b, jnp.maximum(j * nblk - 1, 0), 0))

    oa = pl.pallas_call(
        _attn_kernel,
        grid=(bsz, n_super),
        in_specs=[super3(d, HW) for d in DILATIONS] + [super3(d, 2 * HW) for d in DILATIONS]
        + [prev3(d) for d in DILATIONS] + [_resident(band_bias.shape), _resident(hmask.shape)],
        out_specs=pl.BlockSpec((1, SUPER, HW), lambda b, j: (b, j, 0)),
        out_shape=jax.ShapeDtypeStruct((bsz, seq, HW), BF16),
        scratch_shapes=[pltpu.VMEM((N_GROUPS, HW // LANES, SUPER, LANES), F32)] * 3
        + [pltpu.VMEM((N_GROUPS, HEADS * Q_BLOCK, 2 * Q_BLOCK), F32)],
        compiler_params=_cparams("parallel", "parallel"),
        name="attn",
    )(q1, q2, q3, kv1, kv2, kv3, kv1, kv2, kv3, band_bias, hmask)

    h1 = pl.pallas_call(
        _merge_kernel,
        grid=(n_tok // MERGE_TILE,),
        in_specs=[tok(MERGE_TILE, D_MODEL), tok(MERGE_TILE, HW), tok(MERGE_TILE, SG_W),
                  tok(MERGE_TILE, HW),
                  _resident((D_MODEL, 3 * D_MODEL)), _resident((1, 3 * D_MODEL)),
                  _resident((HW, D_MODEL)), _resident((SG_W, D_MODEL)), _resident((HW, D_MODEL)),
                  _resident((D_MODEL, D_MODEL)), _resident((1, D_MODEL)), _resident((1, D_MODEL))],
        out_specs=tok(MERGE_TILE, D_MODEL),
        out_shape=jax.ShapeDtypeStruct((n_tok, D_MODEL), F32),
        compiler_params=_cparams("parallel"),
        name="merge",
    )(x.reshape(n_tok, D_MODEL), oa.reshape(n_tok, HW), ob.reshape(n_tok, SG_W),
      oc.reshape(n_tok, HW), (0.5 * w_gate).astype(BF16), row(0.5 * b_gate),
      (0.5 * w_proj_a).astype(BF16), (0.5 * w_proj_b).astype(BF16), (0.5 * w_proj_c).astype(BF16),
      w_out.astype(BF16), row(ln1_g), row(ln1_b))

    h2 = pl.pallas_call(
        functools.partial(_ffn_kernel, tiles_per_seq=seq // FFN_TILE),
        grid=(n_tok // FFN_TILE,),
        in_specs=[tok(FFN_TILE, D_MODEL), _resident((D_MODEL, 2 * D_FF)), _resident(conv_w.shape),
                  _resident((1, D_FF)), _resident((D_FF, D_MODEL)),
                  _resident((1, D_MODEL)), _resident((1, D_MODEL))],
        out_specs=tok(FFN_TILE, D_MODEL),
        out_shape=jax.ShapeDtypeStruct((n_tok, D_MODEL), F32),
        scratch_shapes=[pltpu.VMEM((CARRY_ROWS + FFN_TILE, FF_CHUNK), F32),
                        pltpu.VMEM((D_FF // FF_CHUNK, CARRY_ROWS, FF_CHUNK), F32),
                        pltpu.VMEM((FFN_TILE, D_FF), BF16)],
        compiler_params=_cparams("arbitrary"),
        name="ffn",
    )(h1, w_ffn_up.astype(BF16), conv_w, row(conv_b), w_ffn_down.astype(BF16),
      row(ln2_g), row(ln2_b))
    return h2.reshape(bsz, seq, D_MODEL)


def kernel(x, mem, rel_bias, w_in, w_mem_kv, sg_ln_g, sg_ln_b, w_spatial, b_spatial, w_proj_a,
           w_proj_b, w_proj_c, w_gate, b_gate, w_out, ln1_g, ln1_b, w_ffn_up, conv_w, conv_b,
           w_ffn_down, ln2_g, ln2_b):
    band_bias = pl.pallas_call(
        _band_bias_kernel,
        grid=(N_GROUPS,),
        in_specs=[pl.BlockSpec((1, HEADS, 1, 2 * Q_BLOCK), lambda g: (g, 0, 0, 0))],
        out_specs=pl.BlockSpec((1, HEADS * Q_BLOCK, 2 * Q_BLOCK), lambda g: (g, 0, 0)),
        out_shape=jax.ShapeDtypeStruct((N_GROUPS, HEADS * Q_BLOCK, 2 * Q_BLOCK), F32),
        name="band_bias",
    )(_offset_bias(rel_bias))
    lane = jnp.arange(HW)[None, None, :] // HEAD_DIM
    hmask = jnp.broadcast_to(lane == jnp.arange(HEADS)[:, None, None],
                             (HEADS, Q_BLOCK, HW)).astype(BF16)
    h = x
    for l in range(w_in.shape[0]):
        h = _layer(h, mem, band_bias, hmask, w_in[l], w_mem_kv[l], sg_ln_g[l], sg_ln_b[l],
                   w_spatial[l], b_spatial[l], w_proj_a[l], w_proj_b[l], w_proj_c[l], w_gate[l],
                   b_gate[l], w_out[l], ln1_g[l], ln1_b[l], w_ffn_up[l], conv_w[l], conv_b[l],
                   w_ffn_down[l], ln2_g[l], ln2_b[l])
    return h
```

```python
import functools
import math

import jax
import jax.numpy as jnp
from jax import lax
from jax.experimental import pallas as pl
from jax.experimental.pallas import tpu as pltpu

F32 = jnp.float32
BF16 = jnp.bfloat16

LANES = 128
D_MODEL = 1024
HEAD_DIM = 64
DILATIONS = (1, 4, 16)
N_GROUPS = 3
HEADS = 4
HW = HEADS * HEAD_DIM
ATTN_W = N_GROUPS * HW
WINDOW_STEPS = 128
Q_BLOCK = 128
SG_CHUNK = 128
SG_W = 512
MEM_LEN = 256
IN_W = 3 * ATTN_W + 2 * SG_W + HW
D_FF = 2816
N_BUCKETS = 32
MAX_DISTANCE = 2048
LN_EPS = 1e-5
DEPTH = 1
ALPHA = (2 * DEPTH) ** 0.25
ATTN_SCALE = HEAD_DIM ** -0.5
LOG2E = math.log2(math.e)
Q_SCALE = ATTN_SCALE * LOG2E
NEG = -1e30

TOKEN_TILE = 512
MERGE_TILE = 1024
FFN_TILE = 512
SUPER = Q_BLOCK * DILATIONS[-1]
TAIL_ROWS = 256
COMBINE_ROWS = 256
FF_CHUNK = 256
CARRY_ROWS = 8
VMEM_LIMIT = 56 * 1024 * 1024

_NT = (((1,), (1,)), ((), ()))


def _gelu(x):
    return 0.5 * x * (1.0 + jnp.tanh(math.sqrt(2.0 / math.pi) * (x + 0.044715 * (x * x * x))))


def _layer_norm(x, g, b):
    mu = jnp.mean(x, axis=-1, keepdims=True)
    xc = x - mu
    var = jnp.mean(xc * xc, axis=-1, keepdims=True)
    return xc * lax.rsqrt(var + LN_EPS) * g + b


def _resident(shape):
    zeros = (0,) * len(shape)
    return pl.BlockSpec(shape, lambda *_: zeros, pipeline_mode=pl.Buffered(1))


def _head_dense(per_head):
    low = lax.broadcasted_iota(jnp.int32, (per_head[0].shape[0], LANES), 1) < HEAD_DIM
    return jnp.concatenate([jnp.where(low, per_head[h], per_head[h + 1])
                            for h in range(0, HEADS, 2)], axis=1)


def _packed_heads_attention(q, k, v, hm_ref, bias_fn):
    m_rows = q.shape[0]
    qs = jnp.concatenate([q * hm_ref[h] for h in range(HEADS)], axis=0)
    s = lax.dot_general(qs, k, _NT, preferred_element_type=F32)
    ps, ms, ls = [], [], []
    for h in range(HEADS):
        sh = s[h * m_rows:(h + 1) * m_rows]
        bias = bias_fn(h)
        if bias is not None:
            sh = sh + bias
        m = jnp.max(sh, axis=-1, keepdims=True)
        p = jnp.exp2(sh - m)
        ps.append(p.astype(BF16))
        ms.append(m)
        ls.append(jnp.sum(p, axis=-1, keepdims=True))
    pv = jnp.dot(jnp.concatenate(ps, axis=0), v, preferred_element_type=F32)
    half = lambda h: slice((h // 2) * LANES, (h // 2 + 1) * LANES)
    pv = _head_dense([pv[h * m_rows:(h + 1) * m_rows, half(h)] for h in range(HEADS)])
    return pv, _head_dense(ms), _head_dense(ls)


def _band_bias_kernel(f_ref, o_ref):
    for h in range(HEADS):
        rep = jnp.broadcast_to(f_ref[0, h] * LOG2E, (Q_BLOCK, 2 * Q_BLOCK))
        o_ref[0, h * Q_BLOCK:(h + 1) * Q_BLOCK] = pltpu.roll(rep, 0, 1, stride=1, stride_axis=0)


def _proj_kernel(x_ref, win_ref, mem_ref, wkv_ref, lng_ref, lnb_ref, wsp_ref, bsp_ref, hm_ref,
                 q1_ref, q2_ref, q3_ref, kv1_ref, kv2_ref, kv3_ref, ob_ref, oc_ref, de_buf, kvm):
    xb = x_ref[0].astype(BF16)
    tm = xb.shape[0]

    @pl.when(pl.program_id(1) == 0)
    def _():
        kvm[...] = jnp.dot(mem_ref[0].astype(BF16), wkv_ref[...],
                           preferred_element_type=F32).astype(BF16)

    def proj(lo, width):
        return jnp.dot(xb, win_ref[:, lo:lo + width], preferred_element_type=F32)

    def put(ref, lane0, val, dil):
        if dil == 1:
            ref[0, :, lane0:lane0 + HW] = val.astype(BF16)
            return
        stride_cols = ref.shape[2] // dil
        for c in range(HW // LANES):
            de_buf[...] = val[:, c * LANES:(c + 1) * LANES]
            for r in range(dil):
                piece = de_buf[pl.ds(r, tm // dil, stride=dil), :]
                col = r * stride_cols + lane0 + c * LANES
                ref[0, :, col:col + LANES] = piece.astype(BF16)

    def a_proj(g, part):
        q_ref, kv_ref = ((q1_ref, kv1_ref), (q2_ref, kv2_ref), (q3_ref, kv3_ref))[g]
        if part == 0:
            put(q_ref, 0, proj(g * HW, HW) * Q_SCALE, DILATIONS[g])
        else:
            put(kv_ref, (part - 1) * HW, proj(part * ATTN_W + g * HW, HW), DILATIONS[g])

    fillers = iter([(g, part) for g in range(N_GROUPS) for part in range(3)])
    fill = lambda: a_proj(*next(fillers))

    def proj_halves(lo):
        return jnp.concatenate([proj(lo, HW), proj(lo + HW, HW)], axis=1)

    qc = (proj(3 * ATTN_W + 2 * SG_W, HW) * Q_SCALE).astype(BF16)
    u_pre = proj_halves(3 * ATTN_W)
    v_pre = proj_halves(3 * ATTN_W + SG_W)

    n_chunks = tm // SG_CHUNK
    u, vn = [], []
    for c in range(n_chunks):
        rows = slice(c * SG_CHUNK, (c + 1) * SG_CHUNK)
        fill()
        u.append(_gelu(u_pre[rows]))
        vn.append(_layer_norm(_gelu(v_pre[rows]), lng_ref[...], lnb_ref[...]).astype(BF16))
        pv, _, l = _packed_heads_attention(qc[rows], kvm[:, :HW], kvm[:, HW:], hm_ref,
                                           lambda h: None)
        oc_ref[0, rows, :] = (pv * (1.0 / l)).astype(BF16)
        fill()

    row = lax.broadcasted_iota(jnp.int32, (2 * SG_CHUNK, SG_CHUNK), 0) & (SG_CHUNK - 1)
    col = lax.broadcasted_iota(jnp.int32, (2 * SG_CHUNK, SG_CHUNK), 1)
    low_half = lax.broadcasted_iota(jnp.int32, (SG_CHUNK, SG_CHUNK), 1) < HEAD_DIM
    for j in range(SG_W // SG_CHUNK):
        cols = slice(j * SG_CHUNK, (j + 1) * SG_CHUNK)
        w_pair = jnp.where(row >= col, wsp_ref[j], 0.0).astype(BF16)
        rhs = jnp.concatenate([vn[c][:, cols] for c in range(n_chunks)], axis=1)
        r = jnp.dot(w_pair, rhs, preferred_element_type=F32)
        for c in range(n_chunks):
            rows = slice(c * SG_CHUNK, (c + 1) * SG_CHUNK)
            sv = jnp.where(low_half, r[:SG_CHUNK, rows], r[SG_CHUNK:, rows]) + bsp_ref[j]
            ob_ref[0, rows, cols] = (u[c][:, cols] * sv).astype(BF16)
        if j == 0:
            fill()


def _attn_kernel(q1_ref, q2_ref, q3_ref, kv1_ref, kv2_ref, kv3_ref, kp1_ref, kp2_ref, kp3_ref,
                 bias_ref, hm_ref, oa_ref, pv_nat, m_nat, l_nat, bias_first):
    lane = lax.broadcasted_iota(jnp.int32, (HEADS * Q_BLOCK, 2 * Q_BLOCK), 1)
    no_prev = jnp.where(lane < Q_BLOCK, jnp.where(pl.program_id(1) == 0, NEG, 0.0), 0.0)
    for g in range(N_GROUPS):
        bias_first[g] = bias_ref[g] + no_prev

    groups = ((q1_ref, kv1_ref, kp1_ref), (q2_ref, kv2_ref, kp2_ref), (q3_ref, kv3_ref, kp3_ref))
    for g, (q_ref, kv_ref, kvp_ref) in enumerate(groups):
        dil = DILATIONS[g]
        nblk = SUPER // dil // Q_BLOCK
        for r in range(dil):
            kc = slice(r * 2 * HW, r * 2 * HW + HW)
            vc = slice(r * 2 * HW + HW, (r + 1) * 2 * HW)
            qc = slice(r * HW, (r + 1) * HW)
            for i in range(nblk):
                rq = slice(i * Q_BLOCK, (i + 1) * Q_BLOCK)
                if i == 0:
                    k2 = jnp.concatenate([kvp_ref[0, :, kc], kv_ref[0, rq, kc]], axis=0)
                    v2 = jnp.concatenate([kvp_ref[0, :, vc], kv_ref[0, rq, vc]], axis=0)
                else:
                    rk = slice((i - 1) * Q_BLOCK, (i + 1) * Q_BLOCK)
                    k2 = kv_ref[0, rk, kc]
                    v2 = kv_ref[0, rk, vc]
                table = bias_first if i == 0 else bias_ref
                pv, m, l = _packed_heads_attention(
                    q_ref[0, rq, qc], k2, v2, hm_ref,
                    lambda h, table=table: table[g, h * Q_BLOCK:(h + 1) * Q_BLOCK, :])
                nat = pl.ds(i * Q_BLOCK * dil + r, Q_BLOCK, stride=dil) if dil > 1 else rq
                for c in range(HW // LANES):
                    lanes = slice(c * LANES, (c + 1) * LANES)
                    pv_nat[g, c, nat, :] = pv[:, lanes]
                    m_nat[g, c, nat, :] = m[:, lanes]
                    l_nat[g, c, nat, :] = l[:, lanes]

    for t in range(SUPER // COMBINE_ROWS):
        rows = slice(t * COMBINE_ROWS, (t + 1) * COMBINE_ROWS)
        for c in range(HW // LANES):
            m1, m2, m3 = m_nat[0, c, rows, :], m_nat[1, c, rows, :], m_nat[2, c, rows, :]
            m = jnp.maximum(jnp.maximum(m1, m2), m3)
            e1, e2, e3 = jnp.exp2(m1 - m), jnp.exp2(m2 - m), jnp.exp2(m3 - m)
            num = e1 * pv_nat[0, c, rows, :] + e2 * pv_nat[1, c, rows, :] + e3 * pv_nat[2, c, rows, :]
            den = e1 * l_nat[0, c, rows, :] + e2 * l_nat[1, c, rows, :] + e3 * l_nat[2, c, rows, :]
            oa_ref[0, rows, c * LANES:(c + 1) * LANES] = (num * (1.0 / den)).astype(BF16)


def _merge_kernel(x_ref, oa_ref, ob_ref, oc_ref, wg_ref, bg_ref, wpa_ref, wpb_ref, wpc_ref,
                  wout_ref, lng_ref, lnb_ref, h_ref):
    x = x_ref[...]
    xb = x.astype(BF16)

    def gated(k, o_ref, wp_ref):
        cols = slice(k * D_MODEL, (k + 1) * D_MODEL)
        t = jnp.tanh(jnp.dot(xb, wg_ref[:, cols], preferred_element_type=F32) + bg_ref[:, cols])
        half_p = jnp.dot(o_ref[...], wp_ref[...], preferred_element_type=F32)
        return half_p + half_p * t

    merged = (gated(0, oa_ref, wpa_ref) + gated(1, ob_ref, wpb_ref)
              + gated(2, oc_ref, wpc_ref)).astype(BF16)
    for c in range(x_ref.shape[0] // TAIL_ROWS):
        rows = slice(c * TAIL_ROWS, (c + 1) * TAIL_ROWS)
        y = jnp.dot(merged[rows], wout_ref[...], preferred_element_type=F32)
        h_ref[rows, :] = _layer_norm(ALPHA * x_ref[rows, :] + y, lng_ref[...], lnb_ref[...])


def _ffn_kernel(h_ref, wup_ref, cw_ref, cb_ref, wdn_ref, lng_ref, lnb_ref, o_ref,
                a_buf, carry, g_buf, *, tiles_per_seq):
    tm = h_ref.shape[0]
    n_chunks = D_FF // FF_CHUNK

    @pl.when(pl.program_id(0) % tiles_per_seq == 0)
    def _():
        carry[...] = jnp.zeros(carry.shape, F32)

    h = h_ref[...]
    hb = h.astype(BF16)
    for c in range(n_chunks):
        cols = slice(c * FF_CHUNK, (c + 1) * FF_CHUNK)
        a = jnp.dot(hb, wup_ref[:, cols], preferred_element_type=F32)
        gate = jnp.dot(hb, wup_ref[:, D_FF + c * FF_CHUNK:D_FF + (c + 1) * FF_CHUNK],
                       preferred_element_type=F32)
        a_buf[:CARRY_ROWS] = carry[c]
        a_buf[CARRY_ROWS:] = a
        carry[c] = a[tm - CARRY_ROWS:]
        conv = (cw_ref[0:1, cols] * a_buf[CARRY_ROWS - 2:CARRY_ROWS - 2 + tm]
                + cw_ref[1:2, cols] * a_buf[CARRY_ROWS - 1:CARRY_ROWS - 1 + tm]
                + cw_ref[2:3, cols] * a + cb_ref[:, cols])
        g_buf[:, cols] = (_gelu(conv) * gate).astype(BF16)
    for c in range(tm // TAIL_ROWS):
        rows = slice(c * TAIL_ROWS, (c + 1) * TAIL_ROWS)
        y = jnp.dot(g_buf[rows, :], wdn_ref[...], preferred_element_type=F32)
        o_ref[rows, :] = _layer_norm(ALPHA * h_ref[rows, :] + y, lng_ref[...], lnb_ref[...])


def _t5_bucket(dist):
    max_exact = N_BUCKETS // 2
    n = jnp.maximum(dist, 1).astype(jnp.float32)
    large = max_exact + (jnp.log(n / max_exact) / math.log(MAX_DISTANCE / max_exact)
                         * (N_BUCKETS - max_exact)).astype(jnp.int32)
    large = jnp.minimum(large, N_BUCKETS - 1)
    return jnp.where(dist < max_exact, dist, large)


def _offset_bias(rel_bias):
    steps = WINDOW_STEPS - jnp.arange(2 * Q_BLOCK)
    rows = []
    for g, dil in enumerate(DILATIONS):
        bucket = _t5_bucket(jnp.maximum(steps, 0) * dil)
        bias = rel_bias[:, g * HEADS:(g + 1) * HEADS][bucket].astype(F32)
        rows.append(jnp.where(steps[:, None] >= 0, bias, NEG).T)
    return jnp.stack(rows, 0)[:, :, None, :]


def _cparams(*sem):
    return pltpu.CompilerParams(dimension_semantics=sem, vmem_limit_bytes=VMEM_LIMIT)


def _layer(x, mem, band_bias, hmask, w_in, w_mem_kv, sg_ln_g, sg_ln_b, w_spatial, b_spatial,
           w_proj_a, w_proj_b, w_proj_c, w_gate, b_gate, w_out, ln1_g, ln1_b,
           w_ffn_up, conv_w, conv_b, w_ffn_down, ln2_g, ln2_b):
    bsz, seq, _ = x.shape
    n_tok = bsz * seq
    tm = TOKEN_TILE
    assert seq % SUPER == 0 and SUPER % tm == 0 and tm % (SG_CHUNK * 2) == 0
    assert D_FF % FF_CHUNK == 0 and tm % (16 * DILATIONS[-1]) == 0
    assert seq % MERGE_TILE == 0 and seq % FFN_TILE == 0
    tiles_per_seq = seq // tm
    row = lambda v: v.reshape(1, -1)
    tok = lambda rows, width: pl.BlockSpec((rows, width), lambda i: (i, 0))
    tile3 = lambda rows, width: pl.BlockSpec((1, rows, width), lambda b, t: (b, t, 0))

    wsp = w_spatial.reshape(SG_W // SG_CHUNK, 2 * SG_CHUNK, SG_CHUNK)
    bsp = jnp.repeat(b_spatial.reshape(SG_W // SG_CHUNK, 2, SG_CHUNK).transpose(0, 2, 1),
                     HEAD_DIM, axis=-1)
    qkv_shapes = ([jax.ShapeDtypeStruct((bsz, seq // d, d * HW), BF16) for d in DILATIONS]
                  + [jax.ShapeDtypeStruct((bsz, seq // d, d * 2 * HW), BF16) for d in DILATIONS])
    qkv_specs = ([tile3(tm // d, d * HW) for d in DILATIONS]
                 + [tile3(tm // d, d * 2 * HW) for d in DILATIONS])
    q1, q2, q3, kv1, kv2, kv3, ob, oc = pl.pallas_call(
        _proj_kernel,
        grid=(bsz, tiles_per_seq),
        in_specs=[tile3(tm, D_MODEL), _resident((D_MODEL, IN_W)),
                  pl.BlockSpec((1, MEM_LEN, D_MODEL), lambda b, t: (b, 0, 0)),
                  _resident((D_MODEL, 2 * HW)), _resident((1, SG_W)), _resident((1, SG_W)),
                  _resident(wsp.shape), _resident(bsp.shape), _resident(hmask.shape)],
        out_specs=qkv_specs + [tile3(tm, SG_W), tile3(tm, HW)],
        out_shape=qkv_shapes + [jax.ShapeDtypeStruct((bsz, seq, SG_W), BF16),
                                jax.ShapeDtypeStruct((bsz, seq, HW), BF16)],
        scratch_shapes=[pltpu.VMEM((tm, LANES), F32), pltpu.VMEM((MEM_LEN, 2 * HW), BF16)],
        compiler_params=_cparams("parallel", "arbitrary"),
        name="proj",
    )(x, w_in.astype(BF16), mem, w_mem_kv.astype(BF16), row(sg_ln_g), row(sg_ln_b), wsp, bsp,
      hmask)

    n_super = seq // SUPER
    super3 = lambda d, width: pl.BlockSpec((1, SUPER // d, d * width), lambda b, j: (b, j, 0))

    def prev3(d):
        nblk = SUPER // d // Q_BLOCK
        return pl.BlockSpec((1, Q_BLOCK, d * 2 * HW),
                            lambda b, j: (b, jnp.maximum(j * nblk - 1, 0), 0))

    oa = pl.pallas_call(
        _attn_kernel,
        grid=(bsz, n_super),
        in_specs=[super3(d, HW) for d in DILATIONS] + [super3(d, 2 * HW) for d in DILATIONS]
        + [prev3(d) for d in DILATIONS] + [_resident(band_bias.shape), _resident(hmask.shape)],
        out_specs=pl.BlockSpec((1, SUPER, HW), lambda b, j: (b, j, 0)),
        out_shape=jax.ShapeDtypeStruct((bsz, seq, HW), BF16),
        scratch_shapes=[pltpu.VMEM((N_GROUPS, HW // LANES, SUPER, LANES), F32)] * 3
        + [pltpu.VMEM((N_GROUPS, HEADS * Q_BLOCK, 2 * Q_BLOCK), F32)],
        compiler_params=_cparams("parallel", "parallel"),
        name="attn",
    )(q1, q2, q3, kv1, kv2, kv3, kv1, kv2, kv3, band_bias, hmask)

    h1 = pl.pallas_call(
        _merge_kernel,
        grid=(n_tok // MERGE_TILE,),
        in_specs=[tok(MERGE_TILE, D_MODEL), tok(MERGE_TILE, HW), tok(MERGE_TILE, SG_W),
                  tok(MERGE_TILE, HW),
                  _resident((D_MODEL, 3 * D_MODEL)), _resident((1, 3 * D_MODEL)),
                  _resident((HW, D_MODEL)), _resident((SG_W, D_MODEL)), _resident((HW, D_MODEL)),
                  _resident((D_MODEL, D_MODEL)), _resident((1, D_MODEL)), _resident((1, D_MODEL))],
        out_specs=tok(MERGE_TILE, D_MODEL),
        out_shape=jax.ShapeDtypeStruct((n_tok, D_MODEL), F32),
        compiler_params=_cparams("parallel"),
        name="merge",
    )(x.reshape(n_tok, D_MODEL), oa.reshape(n_tok, HW), ob.reshape(n_tok, SG_W),
      oc.reshape(n_tok, HW), (0.5 * w_gate).astype(BF16), row(0.5 * b_gate),
      (0.5 * w_proj_a).astype(BF16), (0.5 * w_proj_b).astype(BF16), (0.5 * w_proj_c).astype(BF16),
      w_out.astype(BF16), row(ln1_g), row(ln1_b))

    h2 = pl.pallas_call(
        functools.partial(_ffn_kernel, tiles_per_seq=seq // FFN_TILE),
        grid=(n_tok // FFN_TILE,),
        in_specs=[tok(FFN_TILE, D_MODEL), _resident((D_MODEL, 2 * D_FF)), _resident(conv_w.shape),
                  _resident((1, D_FF)), _resident((D_FF, D_MODEL)),
                  _resident((1, D_MODEL)), _resident((1, D_MODEL))],
        out_specs=tok(FFN_TILE, D_MODEL),
        out_shape=jax.ShapeDtypeStruct((n_tok, D_MODEL), F32),
        scratch_shapes=[pltpu.VMEM((CARRY_ROWS + FFN_TILE, FF_CHUNK), F32),
                        pltpu.VMEM((D_FF // FF_CHUNK, CARRY_ROWS, FF_CHUNK), F32),
                        pltpu.VMEM((FFN_TILE, D_FF), BF16)],
        compiler_params=_cparams("arbitrary"),
        name="ffn",
    )(h1, w_ffn_up.astype(BF16), conv_w, row(conv_b), w_ffn_down.astype(BF16),
      row(ln2_g), row(ln2_b))
    return h2.reshape(bsz, seq, D_MODEL)


def kernel(x, mem, rel_bias, w_in, w_mem_kv, sg_ln_g, sg_ln_b, w_spatial, b_spatial, w_proj_a,
           w_proj_b, w_proj_c, w_gate, b_gate, w_out, ln1_g, ln1_b, w_ffn_up, conv_w, conv_b,
           w_ffn_down, ln2_g, ln2_b):
    band_bias = pl.pallas_call(
        _band_bias_kernel,
        grid=(N_GROUPS,),
        in_specs=[pl.BlockSpec((1, HEADS, 1, 2 * Q_BLOCK), lambda g: (g, 0, 0, 0))],
        out_specs=pl.BlockSpec((1, HEADS * Q_BLOCK, 2 * Q_BLOCK), lambda g: (g, 0, 0)),
        out_shape=jax.ShapeDtypeStruct((N_GROUPS, HEADS * Q_BLOCK, 2 * Q_BLOCK), F32),
        name="band_bias",
    )(_offset_bias(rel_bias))
    lane = jnp.arange(HW)[None, None, :] // HEAD_DIM
    hmask = jnp.broadcast_to(lane == jnp.arange(HEADS)[:, None, None],
                             (HEADS, Q_BLOCK, HW)).astype(BF16)
    h = x
    for l in range(w_in.shape[0]):
        h = _layer(h, mem, band_bias, hmask, w_in[l], w_mem_kv[l], sg_ln_g[l], sg_ln_b[l],
                   w_spatial[l], b_spatial[l], w_proj_a[l], w_proj_b[l], w_proj_c[l], w_gate[l],
                   b_gate[l], w_out[l], ln1_g[l], ln1_b[l], w_ffn_up[l], conv_w[l], conv_b[l],
                   w_ffn_down[l], ln2_g[l], ln2_b[l])
    return h
```

```python
import functools
import math

import jax
import jax.numpy as jnp
from jax import lax
from jax.experimental import pallas as pl
from jax.experimental.pallas import tpu as pltpu

F32 = jnp.float32
BF16 = jnp.bfloat16

LANES = 128
BF16_ROWS = 16
D_MODEL = 1024
HEAD_DIM = 64
DILATIONS = (1, 4, 16)
N_GROUPS = 3
HEADS = 4
HW = HEADS * HEAD_DIM
ATTN_W = N_GROUPS * HW
WINDOW_STEPS = 128
Q_BLOCK = 128
SG_CHUNK = 128
SG_W = 512
MEM_LEN = 256
IN_W = 3 * ATTN_W + 2 * SG_W + HW
D_FF = 2816
N_BUCKETS = 32
MAX_DISTANCE = 2048
LN_EPS = 1e-5
DEPTH = 1
ALPHA = (2 * DEPTH) ** 0.25
ATTN_SCALE = HEAD_DIM ** -0.5
LOG2E = math.log2(math.e)
Q_SCALE = ATTN_SCALE * LOG2E
NEG = -1e30

TOKEN_TILE = 512
MERGE_TILE = 1024
FFN_TILE = 512
SUPER = Q_BLOCK * DILATIONS[-1]
TAIL_ROWS = 256
COMBINE_ROWS = 256
FF_CHUNK = 256
CARRY_ROWS = 8
VMEM_LIMIT = 56 * 1024 * 1024

_NT = (((1,), (1,)), ((), ()))


def _gelu(x):
    return 0.5 * x * (1.0 + jnp.tanh(math.sqrt(2.0 / math.pi) * (x + 0.044715 * (x * x * x))))


def _layer_norm(x, g, b):
    mu = jnp.mean(x, axis=-1, keepdims=True)
    xc = x - mu
    var = jnp.mean(xc * xc, axis=-1, keepdims=True)
    return xc * lax.rsqrt(var + LN_EPS) * g + b


def _resident(shape):
    zeros = (0,) * len(shape)
    return pl.BlockSpec(shape, lambda *_: zeros, pipeline_mode=pl.Buffered(1))


def _head_dense(per_head):
    low = lax.broadcasted_iota(jnp.int32, (per_head[0].shape[0], LANES), 1) < HEAD_DIM
    return jnp.concatenate([jnp.where(low, per_head[h], per_head[h + 1])
                            for h in range(0, HEADS, 2)], axis=1)


def _packed_heads_attention(q, k, v, hm_ref, bias_fn):
    m_rows = q.shape[0]
    qs = jnp.concatenate([q * hm_ref[h] for h in range(HEADS)], axis=0)
    s = lax.dot_general(qs, k, _NT, preferred_element_type=F32)
    ps, ms, ls = [], [], []
    for h in range(HEADS):
        sh = s[h * m_rows:(h + 1) * m_rows]
        bias = bias_fn(h)
        if bias is not None:
            sh = sh + bias
        m = jnp.max(sh, axis=-1, keepdims=True)
        p = jnp.exp2(sh - m)
        ps.append(p.astype(BF16))
        ms.append(m)
        ls.append(jnp.sum(p, axis=-1, keepdims=True))
    pv = jnp.dot(jnp.concatenate(ps, axis=0), v, preferred_element_type=F32)
    half = lambda h: slice((h // 2) * LANES, (h // 2 + 1) * LANES)
    pv = _head_dense([pv[h * m_rows:(h + 1) * m_rows, half(h)] for h in range(HEADS)])
    return pv, _head_dense(ms), _head_dense(ls)


def _band_bias_kernel(f_ref, o_ref):
    for h in range(HEADS):
        rep = jnp.broadcast_to(f_ref[0, h] * LOG2E, (Q_BLOCK, 2 * Q_BLOCK))
        o_ref[0, h * Q_BLOCK:(h + 1) * Q_BLOCK] = pltpu.roll(rep, 0, 1, stride=1, stride_axis=0)


def _proj_kernel(*refs, side):
    n_side = len(side)
    x_ref, win_ref, mem_ref, wkv_ref, lng_ref, lnb_ref, wsp_ref, bsp_ref, hm_ref = refs[:9]
    side_in = refs[9:9 + n_side]
    q1_ref, q2_ref, q3_ref, kv1_ref, kv2_ref, kv3_ref, ob_ref, oc_ref = refs[9 + n_side:17 + n_side]
    side_out = refs[17 + n_side:17 + 2 * n_side]
    de_buf, kvm = refs[17 + 2 * n_side:]
    xb = x_ref[0].astype(BF16)
    tm = xb.shape[0]

    step = pl.program_id(0) * pl.num_programs(1) + pl.program_id(1)
    for (scale, n_blocks), w_ref, o_ref in zip(side, side_in, side_out):
        @pl.when(step < n_blocks)
        def _(scale=scale, w_ref=w_ref, o_ref=o_ref):
            o_ref[...] = (w_ref[...] * scale).astype(BF16)

    @pl.when(pl.program_id(1) == 0)
    def _():
        kvm[...] = jnp.dot(mem_ref[0].astype(BF16), wkv_ref[...],
                           preferred_element_type=F32).astype(BF16)

    def proj(lo, width):
        return jnp.dot(xb, win_ref[:, lo:lo + width], preferred_element_type=F32)

    def put(ref, lane0, val, dil):
        if dil == 1:
            ref[0, :, lane0:lane0 + HW] = val.astype(BF16)
            return
        stride_cols = ref.shape[2] // dil
        for c in range(HW // LANES):
            de_buf[...] = val[:, c * LANES:(c + 1) * LANES]
            for r in range(dil):
                piece = de_buf[pl.ds(r, tm // dil, stride=dil), :]
                col = r * stride_cols + lane0 + c * LANES
                ref[0, :, col:col + LANES] = piece.astype(BF16)

    def a_proj(g, part):
        q_ref, kv_ref = ((q1_ref, kv1_ref), (q2_ref, kv2_ref), (q3_ref, kv3_ref))[g]
        if part == 0:
            put(q_ref, 0, proj(g * HW, HW) * Q_SCALE, DILATIONS[g])
        else:
            put(kv_ref, (part - 1) * HW, proj(part * ATTN_W + g * HW, HW), DILATIONS[g])

    fillers = iter([(g, part) for g in range(N_GROUPS) for part in range(3)])
    fill = lambda: a_proj(*next(fillers))

    def proj_halves(lo):
        return jnp.concatenate([proj(lo, HW), proj(lo + HW, HW)], axis=1)

    qc = (proj(3 * ATTN_W + 2 * SG_W, HW) * Q_SCALE).astype(BF16)
    u_pre = proj_halves(3 * ATTN_W)
    v_pre = proj_halves(3 * ATTN_W + SG_W)

    n_chunks = tm // SG_CHUNK
    u, vn = [], []
    for c in range(n_chunks):
        rows = slice(c * SG_CHUNK, (c + 1) * SG_CHUNK)
        fill()
        u.append(_gelu(u_pre[rows]))
        vn.append(_layer_norm(_gelu(v_pre[rows]), lng_ref[...], lnb_ref[...]).astype(BF16))
        pv, _, l = _packed_heads_attention(qc[rows], kvm[:, :HW], kvm[:, HW:], hm_ref,
                                           lambda h: None)
        oc_ref[0, rows, :] = (pv * (1.0 / l)).astype(BF16)
        fill()

    row = lax.broadcasted_iota(jnp.int32, (2 * SG_CHUNK, SG_CHUNK), 0) & (SG_CHUNK - 1)
    col = lax.broadcasted_iota(jnp.int32, (2 * SG_CHUNK, SG_CHUNK), 1)
    low_half = lax.broadcasted_iota(jnp.int32, (SG_CHUNK, SG_CHUNK), 1) < HEAD_DIM
    for j in range(SG_W // SG_CHUNK):
        cols = slice(j * SG_CHUNK, (j + 1) * SG_CHUNK)
        w_pair = jnp.where(row >= col, wsp_ref[j], 0.0).astype(BF16)
        rhs = jnp.concatenate([vn[c][:, cols] for c in range(n_chunks)], axis=1)
        r = jnp.dot(w_pair, rhs, preferred_element_type=F32)
        for c in range(n_chunks):
            rows = slice(c * SG_CHUNK, (c + 1) * SG_CHUNK)
            sv = jnp.where(low_half, r[:SG_CHUNK, rows], r[SG_CHUNK:, rows]) + bsp_ref[j]
            ob_ref[0, rows, cols] = (u[c][:, cols] * sv).astype(BF16)
        if j == 0:
            fill()


def _attn_kernel(q1_ref, q2_ref, q3_ref, kv1_ref, kv2_ref, kv3_ref, kp1_ref, kp2_ref, kp3_ref,
                 bias_ref, hm_ref, oa_ref, pv_nat, m_nat, l_nat, bias_first):
    lane = lax.broadcasted_iota(jnp.int32, (HEADS * Q_BLOCK, 2 * Q_BLOCK), 1)
    no_prev = jnp.where(lane < Q_BLOCK, jnp.where(pl.program_id(1) == 0, NEG, 0.0), 0.0)
    for g in range(N_GROUPS):
        bias_first[g] = bias_ref[g] + no_prev

    groups = ((q1_ref, kv1_ref, kp1_ref), (q2_ref, kv2_ref, kp2_ref), (q3_ref, kv3_ref, kp3_ref))
    for g, (q_ref, kv_ref, kvp_ref) in enumerate(groups):
        dil = DILATIONS[g]
        nblk = SUPER // dil // Q_BLOCK
        for r in range(dil):
            kc = slice(r * 2 * HW, r * 2 * HW + HW)
            vc = slice(r * 2 * HW + HW, (r + 1) * 2 * HW)
            qc = slice(r * HW, (r + 1) * HW)
            for i in range(nblk):
                rq = slice(i * Q_BLOCK, (i + 1) * Q_BLOCK)
                if i == 0:
                    k2 = jnp.concatenate([kvp_ref[0, :, kc], kv_ref[0, rq, kc]], axis=0)
                    v2 = jnp.concatenate([kvp_ref[0, :, vc], kv_ref[0, rq, vc]], axis=0)
                else:
                    rk = slice((i - 1) * Q_BLOCK, (i + 1) * Q_BLOCK)
                    k2 = kv_ref[0, rk, kc]
                    v2 = kv_ref[0, rk, vc]
                table = bias_first if i == 0 else bias_ref
                pv, m, l = _packed_heads_attention(
                    q_ref[0, rq, qc], k2, v2, hm_ref,
                    lambda h, table=table: table[g, h * Q_BLOCK:(h + 1) * Q_BLOCK, :])
                nat = pl.ds(i * Q_BLOCK * dil + r, Q_BLOCK, stride=dil) if dil > 1 else rq
                for c in range(HW // LANES):
                    lanes = slice(c * LANES, (c + 1) * LANES)
                    pv_nat[g, c, nat, :] = pv[:, lanes]
                    m_nat[g, c, nat, :] = m[:, lanes]
                    l_nat[g, c, nat, :] = l[:, lanes]

    for t in range(SUPER // COMBINE_ROWS):
        rows = slice(t * COMBINE_ROWS, (t + 1) * COMBINE_ROWS)
        for c in range(HW // LANES):
            m1, m2, m3 = m_nat[0, c, rows, :], m_nat[1, c, rows, :], m_nat[2, c, rows, :]
            m = jnp.maximum(jnp.maximum(m1, m2), m3)
            e1, e2, e3 = jnp.exp2(m1 - m), jnp.exp2(m2 - m), jnp.exp2(m3 - m)
            num = e1 * pv_nat[0, c, rows, :] + e2 * pv_nat[1, c, rows, :] + e3 * pv_nat[2, c, rows, :]
            den = e1 * l_nat[0, c, rows, :] + e2 * l_nat[1, c, rows, :] + e3 * l_nat[2, c, rows, :]
            oa_ref[0, rows, c * LANES:(c + 1) * LANES] = (num * (1.0 / den)).astype(BF16)


def _merge_kernel(x_ref, oa_ref, ob_ref, oc_ref, wg_ref, bg_ref, wpa_ref, wpb_ref, wpc_ref,
                  wout_ref, lng_ref, lnb_ref, h_ref):
    x = x_ref[...]
    xb = x.astype(BF16)

    def gated(k, o_ref, wp_ref):
        cols = slice(k * D_MODEL, (k + 1) * D_MODEL)
        t = jnp.tanh(jnp.dot(xb, wg_ref[:, cols], preferred_element_type=F32) + bg_ref[:, cols])
        half_p = jnp.dot(o_ref[...], wp_ref[...], preferred_element_type=F32)
        return half_p + half_p * t

    merged = (gated(0, oa_ref, wpa_ref) + gated(1, ob_ref, wpb_ref)
              + gated(2, oc_ref, wpc_ref)).astype(BF16)
    for c in range(x_ref.shape[0] // TAIL_ROWS):
        rows = slice(c * TAIL_ROWS, (c + 1) * TAIL_ROWS)
        y = jnp.dot(merged[rows], wout_ref[...], preferred_element_type=F32)
        h_ref[rows, :] = _layer_norm(ALPHA * x_ref[rows, :] + y, lng_ref[...], lnb_ref[...])


def _ffn_kernel(h_ref, wup_ref, cw_ref, cb_ref, wdn_ref, lng_ref, lnb_ref, o_ref,
                a_buf, carry, g_buf, *, tiles_per_seq):
    tm = h_ref.shape[0]
    n_chunks = D_FF // FF_CHUNK

    @pl.when(pl.program_id(0) % tiles_per_seq == 0)
    def _():
        carry[...] = jnp.zeros(carry.shape, F32)

    h = h_ref[...]
    hb = h.astype(BF16)
    for c in range(n_chunks):
        cols = slice(c * FF_CHUNK, (c + 1) * FF_CHUNK)
        a = jnp.dot(hb, wup_ref[:, cols], preferred_element_type=F32)
        gate = jnp.dot(hb, wup_ref[:, D_FF + c * FF_CHUNK:D_FF + (c + 1) * FF_CHUNK],
                       preferred_element_type=F32)
        a_buf[:CARRY_ROWS] = carry[c]
        a_buf[CARRY_ROWS:] = a
        carry[c] = a[tm - CARRY_ROWS:]
        conv = (cw_ref[0:1, cols] * a_buf[CARRY_ROWS - 2:CARRY_ROWS - 2 + tm]
                + cw_ref[1:2, cols] * a_buf[CARRY_ROWS - 1:CARRY_ROWS - 1 + tm]
                + cw_ref[2:3, cols] * a + cb_ref[:, cols])
        g_buf[:, cols] = (_gelu(conv) * gate).astype(BF16)
    for c in range(tm // TAIL_ROWS):
        rows = slice(c * TAIL_ROWS, (c + 1) * TAIL_ROWS)
        y = jnp.dot(g_buf[rows, :], wdn_ref[...], preferred_element_type=F32)
        o_ref[rows, :] = _layer_norm(ALPHA * h_ref[rows, :] + y, lng_ref[...], lnb_ref[...])


def _t5_bucket(dist):
    max_exact = N_BUCKETS // 2
    n = jnp.maximum(dist, 1).astype(jnp.float32)
    large = max_exact + (jnp.log(n / max_exact) / math.log(MAX_DISTANCE / max_exact)
                         * (N_BUCKETS - max_exact)).astype(jnp.int32)
    large = jnp.minimum(large, N_BUCKETS - 1)
    return jnp.where(dist < max_exact, dist, large)


def _offset_bias(rel_bias):
    steps = WINDOW_STEPS - jnp.arange(2 * Q_BLOCK)
    rows = []
    for g, dil in enumerate(DILATIONS):
        bucket = _t5_bucket(jnp.maximum(steps, 0) * dil)
        bias = rel_bias[:, g * HEADS:(g + 1) * HEADS][bucket].astype(F32)
        rows.append(jnp.where(steps[:, None] >= 0, bias, NEG).T)
    return jnp.stack(rows, 0)[:, :, None, :]


def _side_cast_plan(rows, n_steps):
    block = BF16_ROWS
    while rows % block or rows // block > n_steps:
        block += BF16_ROWS
    return block, rows // block


def _cparams(*sem):
    return pltpu.CompilerParams(dimension_semantics=sem, vmem_limit_bytes=VMEM_LIMIT)


def _layer(x, mem, band_bias, hmask, w_in, w_mem_kv, sg_ln_g, sg_ln_b, w_spatial, b_spatial,
           w_proj_a, w_proj_b, w_proj_c, w_gate, b_gate, w_out, ln1_g, ln1_b,
           w_ffn_up, conv_w, conv_b, w_ffn_down, ln2_g, ln2_b):
    bsz, seq, _ = x.shape
    n_tok = bsz * seq
    tm = TOKEN_TILE
    assert seq % SUPER == 0 and SUPER % tm == 0 and tm % (SG_CHUNK * 2) == 0
    assert D_FF % FF_CHUNK == 0 and tm % (16 * DILATIONS[-1]) == 0
    assert seq % MERGE_TILE == 0 and seq % FFN_TILE == 0
    tiles_per_seq = seq // tm
    row = lambda v: v.reshape(1, -1)
    tok = lambda rows, width: pl.BlockSpec((rows, width), lambda i: (i, 0))
    tile3 = lambda rows, width: pl.BlockSpec((1, rows, width), lambda b, t: (b, t, 0))

    wsp = w_spatial.reshape(SG_W // SG_CHUNK, 2 * SG_CHUNK, SG_CHUNK)
    bsp = jnp.repeat(b_spatial.reshape(SG_W // SG_CHUNK, 2, SG_CHUNK).transpose(0, 2, 1),
                     HEAD_DIM, axis=-1)
    qkv_shapes = ([jax.ShapeDtypeStruct((bsz, seq // d, d * HW), BF16) for d in DILATIONS]
                  + [jax.ShapeDtypeStruct((bsz, seq // d, d * 2 * HW), BF16) for d in DILATIONS])
    qkv_specs = ([tile3(tm // d, d * HW) for d in DILATIONS]
                 + [tile3(tm // d, d * 2 * HW) for d in DILATIONS])
    side_w = ((w_gate, 0.5), (w_proj_a, 0.5), (w_proj_b, 0.5), (w_proj_c, 0.5), (w_out, 1.0),
              (w_ffn_up, 1.0), (w_ffn_down, 1.0))
    n_steps = bsz * tiles_per_seq
    plans = [_side_cast_plan(w.shape[0], n_steps) for w, _ in side_w]

    def side_spec(w, plan):
        rows, n_blocks = plan
        return pl.BlockSpec((rows, w.shape[1]), lambda b, t: (
            jnp.minimum(b * tiles_per_seq + t, n_blocks - 1), 0))

    side_specs = [side_spec(w, plan) for (w, _), plan in zip(side_w, plans)]
    outs = pl.pallas_call(
        functools.partial(_proj_kernel,
                          side=tuple((scale, plan[1]) for (_, scale), plan in zip(side_w, plans))),
        grid=(bsz, tiles_per_seq),
        in_specs=[tile3(tm, D_MODEL), _resident((D_MODEL, IN_W)),
                  pl.BlockSpec((1, MEM_LEN, D_MODEL), lambda b, t: (b, 0, 0)),
                  _resident((D_MODEL, 2 * HW)), _resident((1, SG_W)), _resident((1, SG_W)),
                  _resident(wsp.shape), _resident(bsp.shape), _resident(hmask.shape)] + side_specs,
        out_specs=qkv_specs + [tile3(tm, SG_W), tile3(tm, HW)] + side_specs,
        out_shape=qkv_shapes + [jax.ShapeDtypeStruct((bsz, seq, SG_W), BF16),
                                jax.ShapeDtypeStruct((bsz, seq, HW), BF16)]
        + [jax.ShapeDtypeStruct(w.shape, BF16) for w, _ in side_w],
        scratch_shapes=[pltpu.VMEM((tm, LANES), F32), pltpu.VMEM((MEM_LEN, 2 * HW), BF16)],
        compiler_params=_cparams("arbitrary", "arbitrary"),
        name="proj",
    )(x, w_in.astype(BF16), mem, w_mem_kv.astype(BF16), row(sg_ln_g), row(sg_ln_b), wsp, bsp,
      hmask, *[w for w, _ in side_w])
    q1, q2, q3, kv1, kv2, kv3, ob, oc = outs[:8]
    wg_b, wpa_b, wpb_b, wpc_b, wout_b, wup_b, wdn_b = outs[8:]

    n_super = seq // SUPER
    super3 = lambda d, width: pl.BlockSpec((1, SUPER // d, d * width), lambda b, j: (b, j, 0))

    def prev3(d):
        nblk = SUPER // d // Q_BLOCK
        return pl.BlockSpec((1, Q_BLOCK, d * 2 * HW),
                            lambda b, j: (b, jnp.maximum(j * nblk - 1, 0), 0))

    oa = pl.pallas_call(
        _attn_kernel,
        grid=(bsz, n_super),
        in_specs=[super3(d, HW) for d in DILATIONS] + [super3(d, 2 * HW) for d in DILATIONS]
        + [prev3(d) for d in DILATIONS] + [_resident(band_bias.shape), _resident(hmask.shape)],
        out_specs=pl.BlockSpec((1, SUPER, HW), lambda b, j: (b, j, 0)),
        out_shape=jax.ShapeDtypeStruct((bsz, seq, HW), BF16),
        scratch_shapes=[pltpu.VMEM((N_GROUPS, HW // LANES, SUPER, LANES), F32)] * 3
        + [pltpu.VMEM((N_GROUPS, HEADS * Q_BLOCK, 2 * Q_BLOCK), F32)],
        compiler_params=_cparams("parallel", "parallel"),
        name="attn",
    )(q1, q2, q3, kv1, kv2, kv3, kv1, kv2, kv3, band_bias, hmask)

    h1 = pl.pallas_call(
        _merge_kernel,
        grid=(n_tok // MERGE_TILE,),
        in_specs=[tok(MERGE_TILE, D_MODEL), tok(MERGE_TILE, HW), tok(MERGE_TILE, SG_W),
                  tok(MERGE_TILE, HW),
                  _resident((D_MODEL, 3 * D_MODEL)), _resident((1, 3 * D_MODEL)),
                  _resident((HW, D_MODEL)), _resident((SG_W, D_MODEL)), _resident((HW, D_MODEL)),
                  _resident((D_MODEL, D_MODEL)), _resident((1, D_MODEL)), _resident((1, D_MODEL))],
        out_specs=tok(MERGE_TILE, D_MODEL),
        out_shape=jax.ShapeDtypeStruct((n_tok, D_MODEL), F32),
        compiler_params=_cparams("parallel"),
        name="merge",
    )(x.reshape(n_tok, D_MODEL), oa.reshape(n_tok, HW), ob.reshape(n_tok, SG_W),
      oc.reshape(n_tok, HW), wg_b, row(0.5 * b_gate), wpa_b, wpb_b, wpc_b, wout_b,
      row(ln1_g), row(ln1_b))

    h2 = pl.pallas_call(
        functools.partial(_ffn_kernel, tiles_per_seq=seq // FFN_TILE),
        grid=(n_tok // FFN_TILE,),
        in_specs=[tok(FFN_TILE, D_MODEL), _resident((D_MODEL, 2 * D_FF)), _resident(conv_w.shape),
                  _resident((1, D_FF)), _resident((D_FF, D_MODEL)),
                  _resident((1, D_MODEL)), _resident((1, D_MODEL))],
        out_specs=tok(FFN_TILE, D_MODEL),
        out_shape=jax.ShapeDtypeStruct((n_tok, D_MODEL), F32),
        scratch_shapes=[pltpu.VMEM((CARRY_ROWS + FFN_TILE, FF_CHUNK), F32),
                        pltpu.VMEM((D_FF // FF_CHUNK, CARRY_ROWS, FF_CHUNK), F32),
                        pltpu.VMEM((FFN_TILE, D_FF), BF16)],
        compiler_params=_cparams("arbitrary"),
        name="ffn",
    )(h1, wup_b, conv_w, row(conv_b), wdn_b, row(ln2_g), row(ln2_b))
    return h2.reshape(bsz, seq, D_MODEL)


def kernel(x, mem, rel_bias, w_in, w_mem_kv, sg_ln_g, sg_ln_b, w_spatial, b_spatial, w_proj_a,
           w_proj_b, w_proj_c, w_gate, b_gate, w_out, ln1_g, ln1_b, w_ffn_up, conv_w, conv_b,
           w_ffn_down, ln2_g, ln2_b):
    band_bias = pl.pallas_call(
        _band_bias_kernel,
        grid=(N_GROUPS,),
        in_specs=[pl.BlockSpec((1, HEADS, 1, 2 * Q_BLOCK), lambda g: (g, 0, 0, 0))],
        out_specs=pl.BlockSpec((1, HEADS * Q_BLOCK, 2 * Q_BLOCK), lambda g: (g, 0, 0)),
        out_shape=jax.ShapeDtypeStruct((N_GROUPS, HEADS * Q_BLOCK, 2 * Q_BLOCK), F32),
        name="band_bias",
    )(_offset_bias(rel_bias))
    lane = jnp.arange(HW)[None, None, :] // HEAD_DIM
    hmask = jnp.broadcast_to(lane == jnp.arange(HEADS)[:, None, None],
                             (HEADS, Q_BLOCK, HW)).astype(BF16)
    h = x
    for l in range(w_in.shape[0]):
        h = _layer(h, mem, band_bias, hmask, w_in[l], w_mem_kv[l], sg_ln_g[l], sg_ln_b[l],
                   w_spatial[l], b_spatial[l], w_proj_a[l], w_proj_b[l], w_proj_c[l], w_gate[l],
                   b_gate[l], w_out[l], ln1_g[l], ln1_b[l], w_ffn_up[l], conv_w[l], conv_b[l],
                   w_ffn_down[l], ln2_g[l], ln2_b[l])
    return h
```

```python
import functools
import math

import jax
import jax.numpy as jnp
from jax import lax
from jax.experimental import pallas as pl
from jax.experimental.pallas import tpu as pltpu

F32 = jnp.float32
BF16 = jnp.bfloat16

LANES = 128
BF16_ROWS = 16
D_MODEL = 1024
HEAD_DIM = 64
DILATIONS = (1, 4, 16)
N_GROUPS = 3
HEADS = 4
HW = HEADS * HEAD_DIM
ATTN_W = N_GROUPS * HW
WINDOW_STEPS = 128
Q_BLOCK = 128
SG_CHUNK = 128
SG_W = 512
MEM_LEN = 256
IN_W = 3 * ATTN_W + 2 * SG_W + HW
D_FF = 2816
N_BUCKETS = 32
MAX_DISTANCE = 2048
LN_EPS = 1e-5
DEPTH = 1
ALPHA = (2 * DEPTH) ** 0.25
ATTN_SCALE = HEAD_DIM ** -0.5
LOG2E = math.log2(math.e)
Q_SCALE = ATTN_SCALE * LOG2E
NEG = -1e30

TOKEN_TILE = 512
MERGE_TILE = 1024
FFN_TILE = 512
SUPER = Q_BLOCK * DILATIONS[-1]
SIDE_CAST_BLOCKS = 8
TAIL_ROWS = 256
COMBINE_ROWS = 256
FF_CHUNK = 256
CARRY_ROWS = 8
VMEM_LIMIT = 56 * 1024 * 1024

_NT = (((1,), (1,)), ((), ()))


def _gelu(x):
    return 0.5 * x * (1.0 + jnp.tanh(math.sqrt(2.0 / math.pi) * (x + 0.044715 * (x * x * x))))


def _layer_norm(x, g, b):
    mu = jnp.mean(x, axis=-1, keepdims=True)
    xc = x - mu
    var = jnp.mean(xc * xc, axis=-1, keepdims=True)
    return xc * lax.rsqrt(var + LN_EPS) * g + b


def _resident(shape):
    zeros = (0,) * len(shape)
    return pl.BlockSpec(shape, lambda *_: zeros, pipeline_mode=pl.Buffered(1))


def _head_dense(per_head):
    low = lax.broadcasted_iota(jnp.int32, (per_head[0].shape[0], LANES), 1) < HEAD_DIM
    return jnp.concatenate([jnp.where(low, per_head[h], per_head[h + 1])
                            for h in range(0, HEADS, 2)], axis=1)


def _packed_heads_attention(q, k, v, hm_ref, bias_fn):
    m_rows = q.shape[0]
    qs = jnp.concatenate([q * hm_ref[h] for h in range(HEADS)], axis=0)
    s = lax.dot_general(qs, k, _NT, preferred_element_type=F32)
    ps, ms, ls = [], [], []
    for h in range(HEADS):
        sh = s[h * m_rows:(h + 1) * m_rows]
        bias = bias_fn(h)
        if bias is not None:
            sh = sh + bias
        m = jnp.max(sh, axis=-1, keepdims=True)
        p = jnp.exp2(sh - m)
        ps.append(p.astype(BF16))
        ms.append(m)
        ls.append(jnp.sum(p, axis=-1, keepdims=True))
    pv = jnp.dot(jnp.concatenate(ps, axis=0), v, preferred_element_type=F32)
    half = lambda h: slice((h // 2) * LANES, (h // 2 + 1) * LANES)
    pv = _head_dense([pv[h * m_rows:(h + 1) * m_rows, half(h)] for h in range(HEADS)])
    return pv, _head_dense(ms), _head_dense(ls)


def _band_bias_kernel(f_ref, o_ref):
    for h in range(HEADS):
        rep = jnp.broadcast_to(f_ref[0, h] * LOG2E, (Q_BLOCK, 2 * Q_BLOCK))
        o_ref[0, h * Q_BLOCK:(h + 1) * Q_BLOCK] = pltpu.roll(rep, 0, 1, stride=1, stride_axis=0)


def _proj_kernel(*refs, side):
    n_side = len(side)
    x_ref, win_ref, mem_ref, wkv_ref, lng_ref, lnb_ref, wsp_ref, bsp_ref, hm_ref = refs[:9]
    side_in = refs[9:9 + n_side]
    q1_ref, q2_ref, q3_ref, kv1_ref, kv2_ref, kv3_ref, ob_ref, oc_ref = refs[9 + n_side:17 + n_side]
    side_out = refs[17 + n_side:17 + 2 * n_side]
    de_buf, kvm = refs[17 + 2 * n_side:]
    xb = x_ref[0].astype(BF16)
    tm = xb.shape[0]

    step = pl.program_id(0) * pl.num_programs(1) + pl.program_id(1)
    for (scale, n_blocks), w_ref, o_ref in zip(side, side_in, side_out):
        @pl.when(step < n_blocks)
        def _(scale=scale, w_ref=w_ref, o_ref=o_ref):
            o_ref[...] = (w_ref[...] * scale).astype(BF16)

    @pl.when(pl.program_id(1) == 0)
    def _():
        kvm[...] = jnp.dot(mem_ref[0].astype(BF16), wkv_ref[...],
                           preferred_element_type=F32).astype(BF16)

    def proj(lo, width):
        return jnp.dot(xb, win_ref[:, lo:lo + width], preferred_element_type=F32)

    def put(ref, lane0, val, dil):
        if dil == 1:
            ref[0, :, lane0:lane0 + HW] = val.astype(BF16)
            return
        stride_cols = ref.shape[2] // dil
        for c in range(HW // LANES):
            de_buf[...] = val[:, c * LANES:(c + 1) * LANES]
            for r in range(dil):
                piece = de_buf[pl.ds(r, tm // dil, stride=dil), :]
                col = r * stride_cols + lane0 + c * LANES
                ref[0, :, col:col + LANES] = piece.astype(BF16)

    def a_proj(g, part):
        q_ref, kv_ref = ((q1_ref, kv1_ref), (q2_ref, kv2_ref), (q3_ref, kv3_ref))[g]
        if part == 0:
            put(q_ref, 0, proj(g * HW, HW) * Q_SCALE, DILATIONS[g])
        else:
            put(kv_ref, (part - 1) * HW, proj(part * ATTN_W + g * HW, HW), DILATIONS[g])

    fillers = iter([(g, part) for g in range(N_GROUPS) for part in range(3)])
    fill = lambda: a_proj(*next(fillers))

    def proj_halves(lo):
        return jnp.concatenate([proj(lo, HW), proj(lo + HW, HW)], axis=1)

    qc = (proj(3 * ATTN_W + 2 * SG_W, HW) * Q_SCALE).astype(BF16)
    u_pre = proj_halves(3 * ATTN_W)
    v_pre = proj_halves(3 * ATTN_W + SG_W)

    n_chunks = tm // SG_CHUNK
    u, vn = [], []
    for c in range(n_chunks):
        rows = slice(c * SG_CHUNK, (c + 1) * SG_CHUNK)
        fill()
        u.append(_gelu(u_pre[rows]))
        vn.append(_layer_norm(_gelu(v_pre[rows]), lng_ref[...], lnb_ref[...]).astype(BF16))
        pv, _, l = _packed_heads_attention(qc[rows], kvm[:, :HW], kvm[:, HW:], hm_ref,
                                           lambda h: None)
        oc_ref[0, rows, :] = (pv * (1.0 / l)).astype(BF16)
        fill()

    row = lax.broadcasted_iota(jnp.int32, (2 * SG_CHUNK, SG_CHUNK), 0) & (SG_CHUNK - 1)
    col = lax.broadcasted_iota(jnp.int32, (2 * SG_CHUNK, SG_CHUNK), 1)
    low_half = lax.broadcasted_iota(jnp.int32, (SG_CHUNK, SG_CHUNK), 1) < HEAD_DIM
    for j in range(SG_W // SG_CHUNK):
        cols = slice(j * SG_CHUNK, (j + 1) * SG_CHUNK)
        w_pair = jnp.where(row >= col, wsp_ref[j], 0.0).astype(BF16)
        rhs = jnp.concatenate([vn[c][:, cols] for c in range(n_chunks)], axis=1)
        r = jnp.dot(w_pair, rhs, preferred_element_type=F32)
        for c in range(n_chunks):
            rows = slice(c * SG_CHUNK, (c + 1) * SG_CHUNK)
            sv = jnp.where(low_half, r[:SG_CHUNK, rows], r[SG_CHUNK:, rows]) + bsp_ref[j]
            ob_ref[0, rows, cols] = (u[c][:, cols] * sv).astype(BF16)
        if j == 0:
            fill()


def _attn_kernel(q1_ref, q2_ref, q3_ref, kv1_ref, kv2_ref, kv3_ref, kp1_ref, kp2_ref, kp3_ref,
                 bias_ref, hm_ref, oa_ref, pv_nat, m_nat, l_nat, bias_first):
    lane = lax.broadcasted_iota(jnp.int32, (HEADS * Q_BLOCK, 2 * Q_BLOCK), 1)
    no_prev = jnp.where(lane < Q_BLOCK, jnp.where(pl.program_id(1) == 0, NEG, 0.0), 0.0)
    for g in range(N_GROUPS):
        bias_first[g] = bias_ref[g] + no_prev

    groups = ((q1_ref, kv1_ref, kp1_ref), (q2_ref, kv2_ref, kp2_ref), (q3_ref, kv3_ref, kp3_ref))
    for g, (q_ref, kv_ref, kvp_ref) in enumerate(groups):
        dil = DILATIONS[g]
        nblk = SUPER // dil // Q_BLOCK
        for r in range(dil):
            kc = slice(r * 2 * HW, r * 2 * HW + HW)
            vc = slice(r * 2 * HW + HW, (r + 1) * 2 * HW)
            qc = slice(r * HW, (r + 1) * HW)
            for i in range(nblk):
                rq = slice(i * Q_BLOCK, (i + 1) * Q_BLOCK)
                if i == 0:
                    k2 = jnp.concatenate([kvp_ref[0, :, kc], kv_ref[0, rq, kc]], axis=0)
                    v2 = jnp.concatenate([kvp_ref[0, :, vc], kv_ref[0, rq, vc]], axis=0)
                else:
                    rk = slice((i - 1) * Q_BLOCK, (i + 1) * Q_BLOCK)
                    k2 = kv_ref[0, rk, kc]
                    v2 = kv_ref[0, rk, vc]
                table = bias_first if i == 0 else bias_ref
                pv, m, l = _packed_heads_attention(
                    q_ref[0, rq, qc], k2, v2, hm_ref,
                    lambda h, table=table: table[g, h * Q_BLOCK:(h + 1) * Q_BLOCK, :])
                nat = pl.ds(i * Q_BLOCK * dil + r, Q_BLOCK, stride=dil) if dil > 1 else rq
                for c in range(HW // LANES):
                    lanes = slice(c * LANES, (c + 1) * LANES)
                    pv_nat[g, c, nat, :] = pv[:, lanes]
                    m_nat[g, c, nat, :] = m[:, lanes]
                    l_nat[g, c, nat, :] = l[:, lanes]

    for t in range(SUPER // COMBINE_ROWS):
        rows = slice(t * COMBINE_ROWS, (t + 1) * COMBINE_ROWS)
        for c in range(HW // LANES):
            m1, m2, m3 = m_nat[0, c, rows, :], m_nat[1, c, rows, :], m_nat[2, c, rows, :]
            m = jnp.maximum(jnp.maximum(m1, m2), m3)
            e1, e2, e3 = jnp.exp2(m1 - m), jnp.exp2(m2 - m), jnp.exp2(m3 - m)
            num = e1 * pv_nat[0, c, rows, :] + e2 * pv_nat[1, c, rows, :] + e3 * pv_nat[2, c, rows, :]
            den = e1 * l_nat[0, c, rows, :] + e2 * l_nat[1, c, rows, :] + e3 * l_nat[2, c, rows, :]
            oa_ref[0, rows, c * LANES:(c + 1) * LANES] = (num * (1.0 / den)).astype(BF16)


def _merge_kernel(x_ref, oa_ref, ob_ref, oc_ref, wg_ref, bg_ref, wpa_ref, wpb_ref, wpc_ref,
                  wout_ref, lng_ref, lnb_ref, h_ref):
    x = x_ref[...]
    xb = x.astype(BF16)

    def gated(k, o_ref, wp_ref):
        cols = slice(k * D_MODEL, (k + 1) * D_MODEL)
        t = jnp.tanh(jnp.dot(xb, wg_ref[:, cols], preferred_element_type=F32) + bg_ref[:, cols])
        half_p = jnp.dot(o_ref[...], wp_ref[...], preferred_element_type=F32)
        return half_p + half_p * t

    merged = (gated(0, oa_ref, wpa_ref) + gated(1, ob_ref, wpb_ref)
              + gated(2, oc_ref, wpc_ref)).astype(BF16)
    for c in range(x_ref.shape[0] // TAIL_ROWS):
        rows = slice(c * TAIL_ROWS, (c + 1) * TAIL_ROWS)
        y = jnp.dot(merged[rows], wout_ref[...], preferred_element_type=F32)
        h_ref[rows, :] = _layer_norm(ALPHA * x_ref[rows, :] + y, lng_ref[...], lnb_ref[...])


def _ffn_kernel(h_ref, wup_ref, cw_ref, cb_ref, wdn_ref, lng_ref, lnb_ref, o_ref,
                a_buf, carry, g_buf, *, tiles_per_seq):
    tm = h_ref.shape[0]
    n_chunks = D_FF // FF_CHUNK

    @pl.when(pl.program_id(0) % tiles_per_seq == 0)
    def _():
        carry[...] = jnp.zeros(carry.shape, F32)

    h = h_ref[...]
    hb = h.astype(BF16)
    for c in range(n_chunks):
        cols = slice(c * FF_CHUNK, (c + 1) * FF_CHUNK)
        a = jnp.dot(hb, wup_ref[:, cols], preferred_element_type=F32)
        gate = jnp.dot(hb, wup_ref[:, D_FF + c * FF_CHUNK:D_FF + (c + 1) * FF_CHUNK],
                       preferred_element_type=F32)
        a_buf[:CARRY_ROWS] = carry[c]
        a_buf[CARRY_ROWS:] = a
        carry[c] = a[tm - CARRY_ROWS:]
        conv = (cw_ref[0:1, cols] * a_buf[CARRY_ROWS - 2:CARRY_ROWS - 2 + tm]
                + cw_ref[1:2, cols] * a_buf[CARRY_ROWS - 1:CARRY_ROWS - 1 + tm]
                + cw_ref[2:3, cols] * a + cb_ref[:, cols])
        g_buf[:, cols] = (_gelu(conv) * gate).astype(BF16)
    for c in range(tm // TAIL_ROWS):
        rows = slice(c * TAIL_ROWS, (c + 1) * TAIL_ROWS)
        y = jnp.dot(g_buf[rows, :], wdn_ref[...], preferred_element_type=F32)
        o_ref[rows, :] = _layer_norm(ALPHA * h_ref[rows, :] + y, lng_ref[...], lnb_ref[...])


def _t5_bucket(dist):
    max_exact = N_BUCKETS // 2
    n = jnp.maximum(dist, 1).astype(jnp.float32)
    large = max_exact + (jnp.log(n / max_exact) / math.log(MAX_DISTANCE / max_exact)
                         * (N_BUCKETS - max_exact)).astype(jnp.int32)
    large = jnp.minimum(large, N_BUCKETS - 1)
    return jnp.where(dist < max_exact, dist, large)


def _offset_bias(rel_bias):
    steps = WINDOW_STEPS - jnp.arange(2 * Q_BLOCK)
    rows = []
    for g, dil in enumerate(DILATIONS):
        bucket = _t5_bucket(jnp.maximum(steps, 0) * dil)
        bias = rel_bias[:, g * HEADS:(g + 1) * HEADS][bucket].astype(F32)
        rows.append(jnp.where(steps[:, None] >= 0, bias, NEG).T)
    return jnp.stack(rows, 0)[:, :, None, :]


def _side_cast_plan(rows, max_blocks):
    block = BF16_ROWS
    while rows % block or rows // block > max_blocks:
        block += BF16_ROWS
    return block, rows // block


def _cparams(*sem):
    return pltpu.CompilerParams(dimension_semantics=sem, vmem_limit_bytes=VMEM_LIMIT)


def _layer(x, mem, band_bias, hmask, w_in, w_mem_kv, sg_ln_g, sg_ln_b, w_spatial, b_spatial,
           w_proj_a, w_proj_b, w_proj_c, w_gate, b_gate, w_out, ln1_g, ln1_b,
           w_ffn_up, conv_w, conv_b, w_ffn_down, ln2_g, ln2_b):
    bsz, seq, _ = x.shape
    n_tok = bsz * seq
    tm = TOKEN_TILE
    assert seq % SUPER == 0 and SUPER % tm == 0 and tm % (SG_CHUNK * 2) == 0
    assert D_FF % FF_CHUNK == 0 and tm % (16 * DILATIONS[-1]) == 0
    assert seq % MERGE_TILE == 0 and seq % FFN_TILE == 0
    tiles_per_seq = seq // tm
    row = lambda v: v.reshape(1, -1)
    tok = lambda rows, width: pl.BlockSpec((rows, width), lambda i: (i, 0))
    tile3 = lambda rows, width: pl.BlockSpec((1, rows, width), lambda b, t: (b, t, 0))

    wsp = w_spatial.reshape(SG_W // SG_CHUNK, 2 * SG_CHUNK, SG_CHUNK)
    bsp = jnp.repeat(b_spatial.reshape(SG_W // SG_CHUNK, 2, SG_CHUNK).transpose(0, 2, 1),
                     HEAD_DIM, axis=-1)
    qkv_shapes = ([jax.ShapeDtypeStruct((bsz, seq // d, d * HW), BF16) for d in DILATIONS]
                  + [jax.ShapeDtypeStruct((bsz, seq // d, d * 2 * HW), BF16) for d in DILATIONS])
    qkv_specs = ([tile3(tm // d, d * HW) for d in DILATIONS]
                 + [tile3(tm // d, d * 2 * HW) for d in DILATIONS])
    side_w = ((w_gate, 0.5), (w_proj_a, 0.5), (w_proj_b, 0.5), (w_proj_c, 0.5), (w_out, 1.0),
              (w_ffn_up, 1.0), (w_ffn_down, 1.0))
    assert SIDE_CAST_BLOCKS <= bsz * tiles_per_seq
    plans = [_side_cast_plan(w.shape[0], SIDE_CAST_BLOCKS) for w, _ in side_w]

    def side_spec(w, plan):
        rows, n_blocks = plan
        return pl.BlockSpec((rows, w.shape[1]), lambda b, t: (
            jnp.minimum(b * tiles_per_seq + t, n_blocks - 1), 0))

    side_specs = [side_spec(w, plan) for (w, _), plan in zip(side_w, plans)]
    outs = pl.pallas_call(
        functools.partial(_proj_kernel,
                          side=tuple((scale, plan[1]) for (_, scale), plan in zip(side_w, plans))),
        grid=(bsz, tiles_per_seq),
        in_specs=[tile3(tm, D_MODEL), _resident((D_MODEL, IN_W)),
                  pl.BlockSpec((1, MEM_LEN, D_MODEL), lambda b, t: (b, 0, 0)),
                  _resident((D_MODEL, 2 * HW)), _resident((1, SG_W)), _resident((1, SG_W)),
                  _resident(wsp.shape), _resident(bsp.shape), _resident(hmask.shape)] + side_specs,
        out_specs=qkv_specs + [tile3(tm, SG_W), tile3(tm, HW)] + side_specs,
        out_shape=qkv_shapes + [jax.ShapeDtypeStruct((bsz, seq, SG_W), BF16),
                                jax.ShapeDtypeStruct((bsz, seq, HW), BF16)]
        + [jax.ShapeDtypeStruct(w.shape, BF16) for w, _ in side_w],
        scratch_shapes=[pltpu.VMEM((tm, LANES), F32), pltpu.VMEM((MEM_LEN, 2 * HW), BF16)],
        compiler_params=_cparams("arbitrary", "arbitrary"),
        name="proj",
    )(x, w_in.astype(BF16), mem, w_mem_kv.astype(BF16), row(sg_ln_g), row(sg_ln_b), wsp, bsp,
      hmask, *[w for w, _ in side_w])
    q1, q2, q3, kv1, kv2, kv3, ob, oc = outs[:8]
    wg_b, wpa_b, wpb_b, wpc_b, wout_b, wup_b, wdn_b = outs[8:]

    n_super = seq // SUPER
    super3 = lambda d, width: pl.BlockSpec((1, SUPER // d, d * width), lambda b, j: (b, j, 0))

    def prev3(d):
        nblk = SUPER // d // Q_BLOCK
        return pl.BlockSpec((1, Q_BLOCK, d * 2 * HW),
                            lambda b, j: (b, jnp.maximum(j * nblk - 1, 0), 0))

    oa = pl.pallas_call(
        _attn_kernel,
        grid=(bsz, n_super),
        in_specs=[super3(d, HW) for d in DILATIONS] + [super3(d, 2 * HW) for d in DILATIONS]
        + [prev3(d) for d in DILATIONS] + [_resident(band_bias.shape), _resident(hmask.shape)],
        out_specs=pl.BlockSpec((1, SUPER, HW), lambda b, j: (b, j, 0)),
        out_shape=jax.ShapeDtypeStruct((bsz, seq, HW), BF16),
        scratch_shapes=[pltpu.VMEM((N_GROUPS, HW // LANES, SUPER, LANES), F32)] * 3
        + [pltpu.VMEM((N_GROUPS, HEADS * Q_BLOCK, 2 * Q_BLOCK), F32)],
        compiler_params=_cparams("parallel", "parallel"),
        name="attn",
    )(q1, q2, q3, kv1, kv2, kv3, kv1, kv2, kv3, band_bias, hmask)

    h1 = pl.pallas_call(
        _merge_kernel,
        grid=(n_tok // MERGE_TILE,),
        in_specs=[tok(MERGE_TILE, D_MODEL), tok(MERGE_TILE, HW), tok(MERGE_TILE, SG_W),
                  tok(MERGE_TILE, HW),
                  _resident((D_MODEL, 3 * D_MODEL)), _resident((1, 3 * D_MODEL)),
                  _resident((HW, D_MODEL)), _resident((SG_W, D_MODEL)), _resident((HW, D_MODEL)),
                  _resident((D_MODEL, D_MODEL)), _resident((1, D_MODEL)), _resident((1, D_MODEL))],
        out_specs=tok(MERGE_TILE, D_MODEL),
        out_shape=jax.ShapeDtypeStruct((n_tok, D_MODEL), F32),
        compiler_params=_cparams("parallel"),
        name="merge",
    )(x.reshape(n_tok, D_MODEL), oa.reshape(n_tok, HW), ob.reshape(n_tok, SG_W),
      oc.reshape(n_tok, HW), wg_b, row(0.5 * b_gate), wpa_b, wpb_b, wpc_b, wout_b,
      row(ln1_g), row(ln1_b))

    h2 = pl.pallas_call(
        functools.partial(_ffn_kernel, tiles_per_seq=seq // FFN_TILE),
        grid=(n_tok // FFN_TILE,),
        in_specs=[tok(FFN_TILE, D_MODEL), _resident((D_MODEL, 2 * D_FF)), _resident(conv_w.shape),
                  _resident((1, D_FF)), _resident((D_FF, D_MODEL)),
                  _resident((1, D_MODEL)), _resident((1, D_MODEL))],
        out_specs=tok(FFN_TILE, D_MODEL),
        out_shape=jax.ShapeDtypeStruct((n_tok, D_MODEL), F32),
        scratch_shapes=[pltpu.VMEM((CARRY_ROWS + FFN_TILE, FF_CHUNK), F32),
                        pltpu.VMEM((D_FF // FF_CHUNK, CARRY_ROWS, FF_CHUNK), F32),
                        pltpu.VMEM((FFN_TILE, D_FF), BF16)],
        compiler_params=_cparams("arbitrary"),
        name="ffn",
    )(h1, wup_b, conv_w, row(conv_b), wdn_b, row(ln2_g), row(ln2_b))
    return h2.reshape(bsz, seq, D_MODEL)


def kernel(x, mem, rel_bias, w_in, w_mem_kv, sg_ln_g, sg_ln_b, w_spatial, b_spatial, w_proj_a,
           w_proj_b, w_proj_c, w_gate, b_gate, w_out, ln1_g, ln1_b, w_ffn_up, conv_w, conv_b,
           w_ffn_down, ln2_g, ln2_b):
    band_bias = pl.pallas_call(
        _band_bias_kernel,
        grid=(N_GROUPS,),
        in_specs=[pl.BlockSpec((1, HEADS, 1, 2 * Q_BLOCK), lambda g: (g, 0, 0, 0))],
        out_specs=pl.BlockSpec((1, HEADS * Q_BLOCK, 2 * Q_BLOCK), lambda g: (g, 0, 0)),
        out_shape=jax.ShapeDtypeStruct((N_GROUPS, HEADS * Q_BLOCK, 2 * Q_BLOCK), F32),
        name="band_bias",
    )(_offset_bias(rel_bias))
    lane = jnp.arange(HW)[None, None, :] // HEAD_DIM
    hmask = jnp.broadcast_to(lane == jnp.arange(HEADS)[:, None, None],
                             (HEADS, Q_BLOCK, HW)).astype(BF16)
    h = x
    for l in range(w_in.shape[0]):
        h = _layer(h, mem, band_bias, hmask, w_in[l], w_mem_kv[l], sg_ln_g[l], sg_ln_b[l],
                   w_spatial[l], b_spatial[l], w_proj_a[l], w_proj_b[l], w_proj_c[l], w_gate[l],
                   b_gate[l], w_out[l], ln1_g[l], ln1_b[l], w_ffn_up[l], conv_w[l], conv_b[l],
                   w_ffn_down[l], ln2_g[l], ln2_b[l])
    return h
```

```python
import functools
import math

import jax
import jax.numpy as jnp
from jax import lax
from jax.experimental import pallas as pl
from jax.experimental.pallas import tpu as pltpu

F32 = jnp.float32
BF16 = jnp.bfloat16

LANES = 128
BF16_ROWS = 16
D_MODEL = 1024
HEAD_DIM = 64
DILATIONS = (1, 4, 16)
N_GROUPS = 3
HEADS = 4
HW = HEADS * HEAD_DIM
ATTN_W = N_GROUPS * HW
WINDOW_STEPS = 128
Q_BLOCK = 128
SG_CHUNK = 128
SG_W = 512
MEM_LEN = 256
IN_W = 3 * ATTN_W + 2 * SG_W + HW
D_FF = 2816
N_BUCKETS = 32
MAX_DISTANCE = 2048
LN_EPS = 1e-5
DEPTH = 1
ALPHA = (2 * DEPTH) ** 0.25
ATTN_SCALE = HEAD_DIM ** -0.5
LOG2E = math.log2(math.e)
Q_SCALE = ATTN_SCALE * LOG2E
NEG = -1e30

TOKEN_TILE = 512
MERGE_TILE = 1024
FFN_TILE = 1024
SUPER = Q_BLOCK * DILATIONS[-1]
SIDE_CAST_BLOCKS = 8
TAIL_ROWS = 256
COMBINE_ROWS = 256
FF_CHUNK = 256
CARRY_ROWS = 8
VMEM_LIMIT = 56 * 1024 * 1024

_NT = (((1,), (1,)), ((), ()))


def _gelu(x):
    return 0.5 * x * (1.0 + jnp.tanh(math.sqrt(2.0 / math.pi) * (x + 0.044715 * (x * x * x))))


def _layer_norm(x, g, b):
    mu = jnp.mean(x, axis=-1, keepdims=True)
    xc = x - mu
    var = jnp.mean(xc * xc, axis=-1, keepdims=True)
    return xc * lax.rsqrt(var + LN_EPS) * g + b


def _resident(shape):
    zeros = (0,) * len(shape)
    return pl.BlockSpec(shape, lambda *_: zeros, pipeline_mode=pl.Buffered(1))


def _head_dense(per_head):
    low = lax.broadcasted_iota(jnp.int32, (per_head[0].shape[0], LANES), 1) < HEAD_DIM
    return jnp.concatenate([jnp.where(low, per_head[h], per_head[h + 1])
                            for h in range(0, HEADS, 2)], axis=1)


def _packed_heads_attention(q, k, v, hm_ref, bias_fn):
    m_rows = q.shape[0]
    qs = jnp.concatenate([q * hm_ref[h] for h in range(HEADS)], axis=0)
    s = lax.dot_general(qs, k, _NT, preferred_element_type=F32)
    ps, ms, ls = [], [], []
    for h in range(HEADS):
        sh = s[h * m_rows:(h + 1) * m_rows]
        bias = bias_fn(h)
        if bias is not None:
            sh = sh + bias
        m = jnp.max(sh, axis=-1, keepdims=True)
        p = jnp.exp2(sh - m)
        ps.append(p.astype(BF16))
        ms.append(m)
        ls.append(jnp.sum(p, axis=-1, keepdims=True))
    pv = jnp.dot(jnp.concatenate(ps, axis=0), v, preferred_element_type=F32)
    half = lambda h: slice((h // 2) * LANES, (h // 2 + 1) * LANES)
    pv = _head_dense([pv[h * m_rows:(h + 1) * m_rows, half(h)] for h in range(HEADS)])
    return pv, _head_dense(ms), _head_dense(ls)


def _band_bias_kernel(f_ref, o_ref):
    for h in range(HEADS):
        rep = jnp.broadcast_to(f_ref[0, h] * LOG2E, (Q_BLOCK, 2 * Q_BLOCK))
        o_ref[0, h * Q_BLOCK:(h + 1) * Q_BLOCK] = pltpu.roll(rep, 0, 1, stride=1, stride_axis=0)


def _proj_kernel(*refs, side):
    n_side = len(side)
    x_ref, win_ref, mem_ref, wkv_ref, lng_ref, lnb_ref, wsp_ref, bsp_ref, hm_ref = refs[:9]
    side_in = refs[9:9 + n_side]
    q1_ref, q2_ref, q3_ref, kv1_ref, kv2_ref, kv3_ref, ob_ref, oc_ref = refs[9 + n_side:17 + n_side]
    side_out = refs[17 + n_side:17 + 2 * n_side]
    de_buf, kvm = refs[17 + 2 * n_side:]
    xb = x_ref[0].astype(BF16)
    tm = xb.shape[0]

    step = pl.program_id(0) * pl.num_programs(1) + pl.program_id(1)
    for (scale, n_blocks), w_ref, o_ref in zip(side, side_in, side_out):
        @pl.when(step < n_blocks)
        def _(scale=scale, w_ref=w_ref, o_ref=o_ref):
            o_ref[...] = (w_ref[...] * scale).astype(BF16)

    @pl.when(pl.program_id(1) == 0)
    def _():
        kvm[...] = jnp.dot(mem_ref[0].astype(BF16), wkv_ref[...],
                           preferred_element_type=F32).astype(BF16)

    def proj(lo, width):
        return jnp.dot(xb, win_ref[:, lo:lo + width], preferred_element_type=F32)

    def put(ref, lane0, val, dil):
        if dil == 1:
            ref[0, :, lane0:lane0 + HW] = val.astype(BF16)
            return
        stride_cols = ref.shape[2] // dil
        for c in range(HW // LANES):
            de_buf[...] = val[:, c * LANES:(c + 1) * LANES]
            for r in range(dil):
                piece = de_buf[pl.ds(r, tm // dil, stride=dil), :]
                col = r * stride_cols + lane0 + c * LANES
                ref[0, :, col:col + LANES] = piece.astype(BF16)

    def a_proj(g, part):
        q_ref, kv_ref = ((q1_ref, kv1_ref), (q2_ref, kv2_ref), (q3_ref, kv3_ref))[g]
        if part == 0:
            put(q_ref, 0, proj(g * HW, HW) * Q_SCALE, DILATIONS[g])
        else:
            put(kv_ref, (part - 1) * HW, proj(part * ATTN_W + g * HW, HW), DILATIONS[g])

    fillers = iter([(g, part) for g in range(N_GROUPS) for part in range(3)])
    fill = lambda: a_proj(*next(fillers))

    def proj_halves(lo):
        return jnp.concatenate([proj(lo, HW), proj(lo + HW, HW)], axis=1)

    qc = (proj(3 * ATTN_W + 2 * SG_W, HW) * Q_SCALE).astype(BF16)
    u_pre = proj_halves(3 * ATTN_W)
    v_pre = proj_halves(3 * ATTN_W + SG_W)

    n_chunks = tm // SG_CHUNK
    u, vn = [], []
    for c in range(n_chunks):
        rows = slice(c * SG_CHUNK, (c + 1) * SG_CHUNK)
        fill()
        u.append(_gelu(u_pre[rows]))
        vn.append(_layer_norm(_gelu(v_pre[rows]), lng_ref[...], lnb_ref[...]).astype(BF16))
        pv, _, l = _packed_heads_attention(qc[rows], kvm[:, :HW], kvm[:, HW:], hm_ref,
                                           lambda h: None)
        oc_ref[0, rows, :] = (pv * (1.0 / l)).astype(BF16)
        fill()

    row = lax.broadcasted_iota(jnp.int32, (2 * SG_CHUNK, SG_CHUNK), 0) & (SG_CHUNK - 1)
    col = lax.broadcasted_iota(jnp.int32, (2 * SG_CHUNK, SG_CHUNK), 1)
    low_half = lax.broadcasted_iota(jnp.int32, (SG_CHUNK, SG_CHUNK), 1) < HEAD_DIM
    for j in range(SG_W // SG_CHUNK):
        cols = slice(j * SG_CHUNK, (j + 1) * SG_CHUNK)
        w_pair = jnp.where(row >= col, wsp_ref[j], 0.0).astype(BF16)
        rhs = jnp.concatenate([vn[c][:, cols] for c in range(n_chunks)], axis=1)
        r = jnp.dot(w_pair, rhs, preferred_element_type=F32)
        for c in range(n_chunks):
            rows = slice(c * SG_CHUNK, (c + 1) * SG_CHUNK)
            sv = jnp.where(low_half, r[:SG_CHUNK, rows], r[SG_CHUNK:, rows]) + bsp_ref[j]
            ob_ref[0, rows, cols] = (u[c][:, cols] * sv).astype(BF16)
        if j == 0:
            fill()


def _attn_kernel(q1_ref, q2_ref, q3_ref, kv1_ref, kv2_ref, kv3_ref, kp1_ref, kp2_ref, kp3_ref,
                 bias_ref, hm_ref, oa_ref, pv_nat, m_nat, l_nat, bias_first):
    lane = lax.broadcasted_iota(jnp.int32, (HEADS * Q_BLOCK, 2 * Q_BLOCK), 1)
    no_prev = jnp.where(lane < Q_BLOCK, jnp.where(pl.program_id(1) == 0, NEG, 0.0), 0.0)
    for g in range(N_GROUPS):
        bias_first[g] = bias_ref[g] + no_prev

    groups = ((q1_ref, kv1_ref, kp1_ref), (q2_ref, kv2_ref, kp2_ref), (q3_ref, kv3_ref, kp3_ref))
    for g, (q_ref, kv_ref, kvp_ref) in enumerate(groups):
        dil = DILATIONS[g]
        nblk = SUPER // dil // Q_BLOCK
        for r in range(dil):
            kc = slice(r * 2 * HW, r * 2 * HW + HW)
            vc = slice(r * 2 * HW + HW, (r + 1) * 2 * HW)
            qc = slice(r * HW, (r + 1) * HW)
            for i in range(nblk):
                rq = slice(i * Q_BLOCK, (i + 1) * Q_BLOCK)
                if i == 0:
                    k2 = jnp.concatenate([kvp_ref[0, :, kc], kv_ref[0, rq, kc]], axis=0)
                    v2 = jnp.concatenate([kvp_ref[0, :, vc], kv_ref[0, rq, vc]], axis=0)
                else:
                    rk = slice((i - 1) * Q_BLOCK, (i + 1) * Q_BLOCK)
                    k2 = kv_ref[0, rk, kc]
                    v2 = kv_ref[0, rk, vc]
                table = bias_first if i == 0 else bias_ref
                pv, m, l = _packed_heads_attention(
                    q_ref[0, rq, qc], k2, v2, hm_ref,
                    lambda h, table=table: table[g, h * Q_BLOCK:(h + 1) * Q_BLOCK, :])
                nat = pl.ds(i * Q_BLOCK * dil + r, Q_BLOCK, stride=dil) if dil > 1 else rq
                for c in range(HW // LANES):
                    lanes = slice(c * LANES, (c + 1) * LANES)
                    pv_nat[g, c, nat, :] = pv[:, lanes]
                    m_nat[g, c, nat, :] = m[:, lanes]
                    l_nat[g, c, nat, :] = l[:, lanes]

    for t in range(SUPER // COMBINE_ROWS):
        rows = slice(t * COMBINE_ROWS, (t + 1) * COMBINE_ROWS)
        for c in range(HW // LANES):
            m1, m2, m3 = m_nat[0, c, rows, :], m_nat[1, c, rows, :], m_nat[2, c, rows, :]
            m = jnp.maximum(jnp.maximum(m1, m2), m3)
            e1, e2, e3 = jnp.exp2(m1 - m), jnp.exp2(m2 - m), jnp.exp2(m3 - m)
            num = e1 * pv_nat[0, c, rows, :] + e2 * pv_nat[1, c, rows, :] + e3 * pv_nat[2, c, rows, :]
            den = e1 * l_nat[0, c, rows, :] + e2 * l_nat[1, c, rows, :] + e3 * l_nat[2, c, rows, :]
            oa_ref[0, rows, c * LANES:(c + 1) * LANES] = (num * (1.0 / den)).astype(BF16)


def _merge_kernel(x_ref, oa_ref, ob_ref, oc_ref, wg_ref, bg_ref, wpa_ref, wpb_ref, wpc_ref,
                  wout_ref, lng_ref, lnb_ref, h_ref):
    x = x_ref[...]
    xb = x.astype(BF16)

    def gated(k, o_ref, wp_ref):
        cols = slice(k * D_MODEL, (k + 1) * D_MODEL)
        t = jnp.tanh(jnp.dot(xb, wg_ref[:, cols], preferred_element_type=F32) + bg_ref[:, cols])
        half_p = jnp.dot(o_ref[...], wp_ref[...], preferred_element_type=F32)
        return half_p + half_p * t

    merged = (gated(0, oa_ref, wpa_ref) + gated(1, ob_ref, wpb_ref)
              + gated(2, oc_ref, wpc_ref)).astype(BF16)
    for c in range(x_ref.shape[0] // TAIL_ROWS):
        rows = slice(c * TAIL_ROWS, (c + 1) * TAIL_ROWS)
        y = jnp.dot(merged[rows], wout_ref[...], preferred_element_type=F32)
        h_ref[rows, :] = _layer_norm(ALPHA * x_ref[rows, :] + y, lng_ref[...], lnb_ref[...])


def _ffn_kernel(h_ref, wup_ref, cw_ref, cb_ref, wdn_ref, lng_ref, lnb_ref, o_ref,
                a_buf, carry, g_buf, *, tiles_per_seq):
    tm = h_ref.shape[0]
    n_chunks = D_FF // FF_CHUNK

    @pl.when(pl.program_id(0) % tiles_per_seq == 0)
    def _():
        carry[...] = jnp.zeros(carry.shape, F32)

    h = h_ref[...]
    hb = h.astype(BF16)
    for c in range(n_chunks):
        cols = slice(c * FF_CHUNK, (c + 1) * FF_CHUNK)
        a = jnp.dot(hb, wup_ref[:, cols], preferred_element_type=F32)
        gate = jnp.dot(hb, wup_ref[:, D_FF + c * FF_CHUNK:D_FF + (c + 1) * FF_CHUNK],
                       preferred_element_type=F32)
        a_buf[:CARRY_ROWS] = carry[c]
        a_buf[CARRY_ROWS:] = a
        carry[c] = a[tm - CARRY_ROWS:]
        conv = (cw_ref[0:1, cols] * a_buf[CARRY_ROWS - 2:CARRY_ROWS - 2 + tm]
                + cw_ref[1:2, cols] * a_buf[CARRY_ROWS - 1:CARRY_ROWS - 1 + tm]
                + cw_ref[2:3, cols] * a + cb_ref[:, cols])
        g_buf[:, cols] = (_gelu(conv) * gate).astype(BF16)
    for c in range(tm // TAIL_ROWS):
        rows = slice(c * TAIL_ROWS, (c + 1) * TAIL_ROWS)
        y = jnp.dot(g_buf[rows, :], wdn_ref[...], preferred_element_type=F32)
        o_ref[rows, :] = _layer_norm(ALPHA * h_ref[rows, :] + y, lng_ref[...], lnb_ref[...])


def _t5_bucket(dist):
    max_exact = N_BUCKETS // 2
    n = jnp.maximum(dist, 1).astype(jnp.float32)
    large = max_exact + (jnp.log(n / max_exact) / math.log(MAX_DISTANCE / max_exact)
                         * (N_BUCKETS - max_exact)).astype(jnp.int32)
    large = jnp.minimum(large, N_BUCKETS - 1)
    return jnp.where(dist < max_exact, dist, large)


def _offset_bias(rel_bias):
    steps = WINDOW_STEPS - jnp.arange(2 * Q_BLOCK)
    rows = []
    for g, dil in enumerate(DILATIONS):
        bucket = _t5_bucket(jnp.maximum(steps, 0) * dil)
        bias = rel_bias[:, g * HEADS:(g + 1) * HEADS][bucket].astype(F32)
        rows.append(jnp.where(steps[:, None] >= 0, bias, NEG).T)
    return jnp.stack(rows, 0)[:, :, None, :]


def _side_cast_plan(rows, max_blocks):
    block = BF16_ROWS
    while rows % block or rows // block > max_blocks:
        block += BF16_ROWS
    return block, rows // block


def _cparams(*sem):
    return pltpu.CompilerParams(dimension_semantics=sem, vmem_limit_bytes=VMEM_LIMIT)


def _layer(x, mem, band_bias, hmask, w_in, w_mem_kv, sg_ln_g, sg_ln_b, w_spatial, b_spatial,
           w_proj_a, w_proj_b, w_proj_c, w_gate, b_gate, w_out, ln1_g, ln1_b,
           w_ffn_up, conv_w, conv_b, w_ffn_down, ln2_g, ln2_b):
    bsz, seq, _ = x.shape
    n_tok = bsz * seq
    tm = TOKEN_TILE
    assert seq % SUPER == 0 and SUPER % tm == 0 and tm % (SG_CHUNK * 2) == 0
    assert D_FF % FF_CHUNK == 0 and tm % (16 * DILATIONS[-1]) == 0
    assert seq % MERGE_TILE == 0 and seq % FFN_TILE == 0
    tiles_per_seq = seq // tm
    row = lambda v: v.reshape(1, -1)
    tok = lambda rows, width: pl.BlockSpec((rows, width), lambda i: (i, 0))
    tile3 = lambda rows, width: pl.BlockSpec((1, rows, width), lambda b, t: (b, t, 0))

    wsp = w_spatial.reshape(SG_W // SG_CHUNK, 2 * SG_CHUNK, SG_CHUNK)
    bsp = jnp.repeat(b_spatial.reshape(SG_W // SG_CHUNK, 2, SG_CHUNK).transpose(0, 2, 1),
                     HEAD_DIM, axis=-1)
    qkv_shapes = ([jax.ShapeDtypeStruct((bsz, seq // d, d * HW), BF16) for d in DILATIONS]
                  + [jax.ShapeDtypeStruct((bsz, seq // d, d * 2 * HW), BF16) for d in DILATIONS])
    qkv_specs = ([tile3(tm // d, d * HW) for d in DILATIONS]
                 + [tile3(tm // d, d * 2 * HW) for d in DILATIONS])
    side_w = ((w_gate, 0.5), (w_proj_a, 0.5), (w_proj_b, 0.5), (w_proj_c, 0.5), (w_out, 1.0),
              (w_ffn_up, 1.0), (w_ffn_down, 1.0))
    assert SIDE_CAST_BLOCKS <= bsz * tiles_per_seq
    plans = [_side_cast_plan(w.shape[0], SIDE_CAST_BLOCKS) for w, _ in side_w]

    def side_spec(w, plan):
        rows, n_blocks = plan
        return pl.BlockSpec((rows, w.shape[1]), lambda b, t: (
            jnp.minimum(b * tiles_per_seq + t, n_blocks - 1), 0))

    side_specs = [side_spec(w, plan) for (w, _), plan in zip(side_w, plans)]
    outs = pl.pallas_call(
        functools.partial(_proj_kernel,
                          side=tuple((scale, plan[1]) for (_, scale), plan in zip(side_w, plans))),
        grid=(bsz, tiles_per_seq),
        in_specs=[tile3(tm, D_MODEL), _resident((D_MODEL, IN_W)),
                  pl.BlockSpec((1, MEM_LEN, D_MODEL), lambda b, t: (b, 0, 0)),
                  _resident((D_MODEL, 2 * HW)), _resident((1, SG_W)), _resident((1, SG_W)),
                  _resident(wsp.shape), _resident(bsp.shape), _resident(hmask.shape)] + side_specs,
        out_specs=qkv_specs + [tile3(tm, SG_W), tile3(tm, HW)] + side_specs,
        out_shape=qkv_shapes + [jax.ShapeDtypeStruct((bsz, seq, SG_W), BF16),
                                jax.ShapeDtypeStruct((bsz, seq, HW), BF16)]
        + [jax.ShapeDtypeStruct(w.shape, BF16) for w, _ in side_w],
        scratch_shapes=[pltpu.VMEM((tm, LANES), F32), pltpu.VMEM((MEM_LEN, 2 * HW), BF16)],
        compiler_params=_cparams("arbitrary", "arbitrary"),
        name="proj",
    )(x, w_in.astype(BF16), mem, w_mem_kv.astype(BF16), row(sg_ln_g), row(sg_ln_b), wsp, bsp,
      hmask, *[w for w, _ in side_w])
    q1, q2, q3, kv1, kv2, kv3, ob, oc = outs[:8]
    wg_b, wpa_b, wpb_b, wpc_b, wout_b, wup_b, wdn_b = outs[8:]

    n_super = seq // SUPER
    super3 = lambda d, width: pl.BlockSpec((1, SUPER // d, d * width), lambda b, j: (b, j, 0))

    def prev3(d):
        nblk = SUPER // d // Q_BLOCK
        return pl.BlockSpec((1, Q_BLOCK, d * 2 * HW),
                            lambda b, j: (b, jnp.maximum(j * nblk - 1, 0), 0))

    oa = pl.pallas_call(
        _attn_kernel,
        grid=(bsz, n_super),
        in_specs=[super3(d, HW) for d in DILATIONS] + [super3(d, 2 * HW) for d in DILATIONS]
        + [prev3(d) for d in DILATIONS] + [_resident(band_bias.shape), _resident(hmask.shape)],
        out_specs=pl.BlockSpec((1, SUPER, HW), lambda b, j: (b, j, 0)),
        out_shape=jax.ShapeDtypeStruct((bsz, seq, HW), BF16),
        scratch_shapes=[pltpu.VMEM((N_GROUPS, HW // LANES, SUPER, LANES), F32)] * 3
        + [pltpu.VMEM((N_GROUPS, HEADS * Q_BLOCK, 2 * Q_BLOCK), F32)],
        compiler_params=_cparams("parallel", "parallel"),
        name="attn",
    )(q1, q2, q3, kv1, kv2, kv3, kv1, kv2, kv3, band_bias, hmask)

    h1 = pl.pallas_call(
        _merge_kernel,
        grid=(n_tok // MERGE_TILE,),
        in_specs=[tok(MERGE_TILE, D_MODEL), tok(MERGE_TILE, HW), tok(MERGE_TILE, SG_W),
                  tok(MERGE_TILE, HW),
                  _resident((D_MODEL, 3 * D_MODEL)), _resident((1, 3 * D_MODEL)),
                  _resident((HW, D_MODEL)), _resident((SG_W, D_MODEL)), _resident((HW, D_MODEL)),
                  _resident((D_MODEL, D_MODEL)), _resident((1, D_MODEL)), _resident((1, D_MODEL))],
        out_specs=tok(MERGE_TILE, D_MODEL),
        out_shape=jax.ShapeDtypeStruct((n_tok, D_MODEL), F32),
        compiler_params=_cparams("parallel"),
        name="merge",
    )(x.reshape(n_tok, D_MODEL), oa.reshape(n_tok, HW), ob.reshape(n_tok, SG_W),
      oc.reshape(n_tok, HW), wg_b, row(0.5 * b_gate), wpa_b, wpb_b, wpc_b, wout_b,
      row(ln1_g), row(ln1_b))

    h2 = pl.pallas_call(
        functools.partial(_ffn_kernel, tiles_per_seq=seq // FFN_TILE),
        grid=(n_tok // FFN_TILE,),
        in_specs=[tok(FFN_TILE, D_MODEL), _resident((D_MODEL, 2 * D_FF)), _resident(conv_w.shape),
                  _resident((1, D_FF)), _resident((D_FF, D_MODEL)),
                  _resident((1, D_MODEL)), _resident((1, D_MODEL))],
        out_specs=tok(FFN_TILE, D_MODEL),
        out_shape=jax.ShapeDtypeStruct((n_tok, D_MODEL), F32),
        scratch_shapes=[pltpu.VMEM((CARRY_ROWS + FFN_TILE, FF_CHUNK), F32),
                        pltpu.VMEM((D_FF // FF_CHUNK, CARRY_ROWS, FF_CHUNK), F32),
                        pltpu.VMEM((FFN_TILE, D_FF), BF16)],
        compiler_params=_cparams("arbitrary"),
        name="ffn",
    )(h1, wup_b, conv_w, row(conv_b), wdn_b, row(ln2_g), row(ln2_b))
    return h2.reshape(bsz, seq, D_MODEL)


def kernel(x, mem, rel_bias, w_in, w_mem_kv, sg_ln_g, sg_ln_b, w_spatial, b_spatial, w_proj_a,
           w_proj_b, w_proj_c, w_gate, b_gate, w_out, ln1_g, ln1_b, w_ffn_up, conv_w, conv_b,
           w_ffn_down, ln2_g, ln2_b):
    band_bias = pl.pallas_call(
        _band_bias_kernel,
        grid=(N_GROUPS,),
        in_specs=[pl.BlockSpec((1, HEADS, 1, 2 * Q_BLOCK), lambda g: (g, 0, 0, 0))],
        out_specs=pl.BlockSpec((1, HEADS * Q_BLOCK, 2 * Q_BLOCK), lambda g: (g, 0, 0)),
        out_shape=jax.ShapeDtypeStruct((N_GROUPS, HEADS * Q_BLOCK, 2 * Q_BLOCK), F32),
        name="band_bias",
    )(_offset_bias(rel_bias))
    lane = jnp.arange(HW)[None, None, :] // HEAD_DIM
    hmask = jnp.broadcast_to(lane == jnp.arange(HEADS)[:, None, None],
                             (HEADS, Q_BLOCK, HW)).astype(BF16)
    h = x
    for l in range(w_in.shape[0]):
        h = _layer(h, mem, band_bias, hmask, w_in[l], w_mem_kv[l], sg_ln_g[l], sg_ln_b[l],
                   w_spatial[l], b_spatial[l], w_proj_a[l], w_proj_b[l], w_proj_c[l], w_gate[l],
                   b_gate[l], w_out[l], ln1_g[l], ln1_b[l], w_ffn_up[l], conv_w[l], conv_b[l],
                   w_ffn_down[l], ln2_g[l], ln2_b[l])
    return h
```

```python
import functools
import math

import jax
import jax.numpy as jnp
from jax import lax
from jax.experimental import pallas as pl
from jax.experimental.pallas import tpu as pltpu

F32 = jnp.float32
BF16 = jnp.bfloat16

LANES = 128
BF16_ROWS = 16
D_MODEL = 1024
HEAD_DIM = 64
DILATIONS = (1, 4, 16)
N_GROUPS = 3
HEADS = 4
HW = HEADS * HEAD_DIM
ATTN_W = N_GROUPS * HW
WINDOW_STEPS = 128
Q_BLOCK = 128
SG_CHUNK = 128
SG_W = 512
MEM_LEN = 256
IN_W = 3 * ATTN_W + 2 * SG_W + HW
D_FF = 2816
N_BUCKETS = 32
MAX_DISTANCE = 2048
LN_EPS = 1e-5
DEPTH = 1
ALPHA = (2 * DEPTH) ** 0.25
ATTN_SCALE = HEAD_DIM ** -0.5
LOG2E = math.log2(math.e)
Q_SCALE = ATTN_SCALE * LOG2E
NEG = -1e30

TOKEN_TILE = 512
MERGE_TILE = 1024
FFN_TILE = 1024
SUPER = Q_BLOCK * DILATIONS[-1]
SIDE_CAST_BLOCKS = 8
TAIL_ROWS = 256
COMBINE_ROWS = 256
FF_CHUNK = 256
CARRY_ROWS = 8
VMEM_LIMIT = 56 * 1024 * 1024

_NT = (((1,), (1,)), ((), ()))


def _gelu(x):
    return 0.5 * x * (1.0 + jnp.tanh(math.sqrt(2.0 / math.pi) * (x + 0.044715 * (x * x * x))))


def _layer_norm(x, g, b):
    mu = jnp.mean(x, axis=-1, keepdims=True)
    xc = x - mu
    var = jnp.mean(xc * xc, axis=-1, keepdims=True)
    return xc * lax.rsqrt(var + LN_EPS) * g + b


def _resident(shape):
    zeros = (0,) * len(shape)
    return pl.BlockSpec(shape, lambda *_: zeros, pipeline_mode=pl.Buffered(1))


def _head_dense(per_head):
    low = lax.broadcasted_iota(jnp.int32, (per_head[0].shape[0], LANES), 1) < HEAD_DIM
    return jnp.concatenate([jnp.where(low, per_head[h], per_head[h + 1])
                            for h in range(0, HEADS, 2)], axis=1)


def _packed_heads_attention(q, k, v, hm_ref, bias_fn):
    m_rows = q.shape[0]
    qs = jnp.concatenate([q * hm_ref[h] for h in range(HEADS)], axis=0)
    s = lax.dot_general(qs, k, _NT, preferred_element_type=F32)
    ps, ms, ls = [], [], []
    for h in range(HEADS):
        sh = s[h * m_rows:(h + 1) * m_rows]
        bias = bias_fn(h)
        if bias is not None:
            sh = sh + bias
        m = jnp.max(sh, axis=-1, keepdims=True)
        p = jnp.exp2(sh - m)
        ps.append(p.astype(BF16))
        ms.append(m)
        ls.append(jnp.sum(p, axis=-1, keepdims=True))
    pv = jnp.dot(jnp.concatenate(ps, axis=0), v, preferred_element_type=F32)
    half = lambda h: slice((h // 2) * LANES, (h // 2 + 1) * LANES)
    pv = _head_dense([pv[h * m_rows:(h + 1) * m_rows, half(h)] for h in range(HEADS)])
    return pv, _head_dense(ms), _head_dense(ls)


def _band_bias_kernel(f_ref, o_ref):
    lane = lax.broadcasted_iota(jnp.int32, (Q_BLOCK, 2 * Q_BLOCK), 1)
    no_prev = jnp.where(lane < Q_BLOCK, NEG, 0.0)
    for h in range(HEADS):
        rep = jnp.broadcast_to(f_ref[0, h] * LOG2E, (Q_BLOCK, 2 * Q_BLOCK))
        band = pltpu.roll(rep, 0, 1, stride=1, stride_axis=0)
        o_ref[0, 0, h * Q_BLOCK:(h + 1) * Q_BLOCK] = band
        o_ref[0, 1, h * Q_BLOCK:(h + 1) * Q_BLOCK] = band + no_prev


def _proj_kernel(*refs, side):
    n_side = len(side)
    x_ref, win_ref, mem_ref, wkv_ref, lng_ref, lnb_ref, wsp_ref, bsp_ref, hm_ref = refs[:9]
    side_in = refs[9:9 + n_side]
    q1_ref, q2_ref, q3_ref, kv1_ref, kv2_ref, kv3_ref, ob_ref, oc_ref = refs[9 + n_side:17 + n_side]
    side_out = refs[17 + n_side:17 + 2 * n_side]
    de_buf, kvm = refs[17 + 2 * n_side:]
    xb = x_ref[0].astype(BF16)
    tm = xb.shape[0]

    step = pl.program_id(0) * pl.num_programs(1) + pl.program_id(1)
    for (scale, n_blocks), w_ref, o_ref in zip(side, side_in, side_out):
        @pl.when(step < n_blocks)
        def _(scale=scale, w_ref=w_ref, o_ref=o_ref):
            o_ref[...] = (w_ref[...] * scale).astype(BF16)

    @pl.when(pl.program_id(1) == 0)
    def _():
        kvm[...] = jnp.dot(mem_ref[0].astype(BF16), wkv_ref[...],
                           preferred_element_type=F32).astype(BF16)

    def proj(lo, width):
        return jnp.dot(xb, win_ref[:, lo:lo + width], preferred_element_type=F32)

    def put(ref, lane0, val, dil):
        if dil == 1:
            ref[0, :, lane0:lane0 + HW] = val.astype(BF16)
            return
        stride_cols = ref.shape[2] // dil
        for c in range(HW // LANES):
            de_buf[...] = val[:, c * LANES:(c + 1) * LANES]
            for r in range(dil):
                piece = de_buf[pl.ds(r, tm // dil, stride=dil), :]
                col = r * stride_cols + lane0 + c * LANES
                ref[0, :, col:col + LANES] = piece.astype(BF16)

    def a_proj(g, part):
        q_ref, kv_ref = ((q1_ref, kv1_ref), (q2_ref, kv2_ref), (q3_ref, kv3_ref))[g]
        if part == 0:
            put(q_ref, 0, proj(g * HW, HW) * Q_SCALE, DILATIONS[g])
        else:
            put(kv_ref, (part - 1) * HW, proj(part * ATTN_W + g * HW, HW), DILATIONS[g])

    fillers = iter([(g, part) for g in range(N_GROUPS) for part in range(3)])
    fill = lambda: a_proj(*next(fillers))

    def proj_halves(lo):
        return jnp.concatenate([proj(lo, HW), proj(lo + HW, HW)], axis=1)

    qc = (proj(3 * ATTN_W + 2 * SG_W, HW) * Q_SCALE).astype(BF16)
    u_pre = proj_halves(3 * ATTN_W)
    v_pre = proj_halves(3 * ATTN_W + SG_W)

    n_chunks = tm // SG_CHUNK
    u, vn = [], []
    for c in range(n_chunks):
        rows = slice(c * SG_CHUNK, (c + 1) * SG_CHUNK)
        fill()
        u.append(_gelu(u_pre[rows]))
        vn.append(_layer_norm(_gelu(v_pre[rows]), lng_ref[...], lnb_ref[...]).astype(BF16))
        pv, _, l = _packed_heads_attention(qc[rows], kvm[:, :HW], kvm[:, HW:], hm_ref,
                                           lambda h: None)
        oc_ref[0, rows, :] = (pv * (1.0 / l)).astype(BF16)
        fill()

    row = lax.broadcasted_iota(jnp.int32, (2 * SG_CHUNK, SG_CHUNK), 0) & (SG_CHUNK - 1)
    col = lax.broadcasted_iota(jnp.int32, (2 * SG_CHUNK, SG_CHUNK), 1)
    low_half = lax.broadcasted_iota(jnp.int32, (SG_CHUNK, SG_CHUNK), 1) < HEAD_DIM
    for j in range(SG_W // SG_CHUNK):
        cols = slice(j * SG_CHUNK, (j + 1) * SG_CHUNK)
        w_pair = jnp.where(row >= col, wsp_ref[j], 0.0).astype(BF16)
        rhs = jnp.concatenate([vn[c][:, cols] for c in range(n_chunks)], axis=1)
        r = jnp.dot(w_pair, rhs, preferred_element_type=F32)
        for c in range(n_chunks):
            rows = slice(c * SG_CHUNK, (c + 1) * SG_CHUNK)
            sv = jnp.where(low_half, r[:SG_CHUNK, rows], r[SG_CHUNK:, rows]) + bsp_ref[j]
            ob_ref[0, rows, cols] = (u[c][:, cols] * sv).astype(BF16)
        if j == 0:
            fill()


def _attn_kernel(q1_ref, q2_ref, q3_ref, kv1_ref, kv2_ref, kv3_ref, kp1_ref, kp2_ref, kp3_ref,
                 bias_ref, hm_ref, oa_ref, pv_nat, m_nat, l_nat):
    lead = (pl.program_id(1) == 0).astype(jnp.int32)

    groups = ((q1_ref, kv1_ref, kp1_ref), (q2_ref, kv2_ref, kp2_ref), (q3_ref, kv3_ref, kp3_ref))
    for g, (q_ref, kv_ref, kvp_ref) in enumerate(groups):
        dil = DILATIONS[g]
        nblk = SUPER // dil // Q_BLOCK
        for r in range(dil):
            kc = slice(r * 2 * HW, r * 2 * HW + HW)
            vc = slice(r * 2 * HW + HW, (r + 1) * 2 * HW)
            qc = slice(r * HW, (r + 1) * HW)
            for i in range(nblk):
                rq = slice(i * Q_BLOCK, (i + 1) * Q_BLOCK)
                if i == 0:
                    k2 = jnp.concatenate([kvp_ref[0, :, kc], kv_ref[0, rq, kc]], axis=0)
                    v2 = jnp.concatenate([kvp_ref[0, :, vc], kv_ref[0, rq, vc]], axis=0)
                else:
                    rk = slice((i - 1) * Q_BLOCK, (i + 1) * Q_BLOCK)
                    k2 = kv_ref[0, rk, kc]
                    v2 = kv_ref[0, rk, vc]
                table = lead if i == 0 else 0
                pv, m, l = _packed_heads_attention(
                    q_ref[0, rq, qc], k2, v2, hm_ref,
                    lambda h, table=table: bias_ref[g, table, h * Q_BLOCK:(h + 1) * Q_BLOCK, :])
                nat = pl.ds(i * Q_BLOCK * dil + r, Q_BLOCK, stride=dil) if dil > 1 else rq
                for c in range(HW // LANES):
                    lanes = slice(c * LANES, (c + 1) * LANES)
                    pv_nat[g, c, nat, :] = pv[:, lanes]
                    m_nat[g, c, nat, :] = m[:, lanes]
                    l_nat[g, c, nat, :] = l[:, lanes]

    for t in range(SUPER // COMBINE_ROWS):
        rows = slice(t * COMBINE_ROWS, (t + 1) * COMBINE_ROWS)
        for c in range(HW // LANES):
            m1, m2, m3 = m_nat[0, c, rows, :], m_nat[1, c, rows, :], m_nat[2, c, rows, :]
            m = jnp.maximum(jnp.maximum(m1, m2), m3)
            e1, e2, e3 = jnp.exp2(m1 - m), jnp.exp2(m2 - m), jnp.exp2(m3 - m)
            num = e1 * pv_nat[0, c, rows, :] + e2 * pv_nat[1, c, rows, :] + e3 * pv_nat[2, c, rows, :]
            den = e1 * l_nat[0, c, rows, :] + e2 * l_nat[1, c, rows, :] + e3 * l_nat[2, c, rows, :]
            oa_ref[0, rows, c * LANES:(c + 1) * LANES] = (num * (1.0 / den)).astype(BF16)


def _merge_kernel(x_ref, oa_ref, ob_ref, oc_ref, wg_ref, bg_ref, wpa_ref, wpb_ref, wpc_ref,
                  wout_ref, lng_ref, lnb_ref, h_ref):
    x = x_ref[...]
    xb = x.astype(BF16)

    def gated(k, o_ref, wp_ref):
        cols = slice(k * D_MODEL, (k + 1) * D_MODEL)
        t = jnp.tanh(jnp.dot(xb, wg_ref[:, cols], preferred_element_type=F32) + bg_ref[:, cols])
        half_p = jnp.dot(o_ref[...], wp_ref[...], preferred_element_type=F32)
        return half_p + half_p * t

    merged = (gated(0, oa_ref, wpa_ref) + gated(1, ob_ref, wpb_ref)
              + gated(2, oc_ref, wpc_ref)).astype(BF16)
    for c in range(x_ref.shape[0] // TAIL_ROWS):
        rows = slice(c * TAIL_ROWS, (c + 1) * TAIL_ROWS)
        y = jnp.dot(merged[rows], wout_ref[...], preferred_element_type=F32)
        h_ref[rows, :] = _layer_norm(ALPHA * x_ref[rows, :] + y, lng_ref[...], lnb_ref[...])


def _ffn_kernel(h_ref, wup_ref, cw_ref, cb_ref, wdn_ref, lng_ref, lnb_ref, o_ref,
                a_buf, carry, g_buf, *, tiles_per_seq):
    tm = h_ref.shape[0]
    n_chunks = D_FF // FF_CHUNK

    @pl.when(pl.program_id(0) % tiles_per_seq == 0)
    def _():
        carry[...] = jnp.zeros(carry.shape, F32)

    h = h_ref[...]
    hb = h.astype(BF16)
    for c in range(n_chunks):
        cols = slice(c * FF_CHUNK, (c + 1) * FF_CHUNK)
        a = jnp.dot(hb, wup_ref[:, cols], preferred_element_type=F32)
        gate = jnp.dot(hb, wup_ref[:, D_FF + c * FF_CHUNK:D_FF + (c + 1) * FF_CHUNK],
                       preferred_element_type=F32)
        a_buf[:CARRY_ROWS] = carry[c]
        a_buf[CARRY_ROWS:] = a
        carry[c] = a[tm - CARRY_ROWS:]
        conv = (cw_ref[0:1, cols] * a_buf[CARRY_ROWS - 2:CARRY_ROWS - 2 + tm]
                + cw_ref[1:2, cols] * a_buf[CARRY_ROWS - 1:CARRY_ROWS - 1 + tm]
                + cw_ref[2:3, cols] * a + cb_ref[:, cols])
        g_buf[:, cols] = (_gelu(conv) * gate).astype(BF16)
    for c in range(tm // TAIL_ROWS):
        rows = slice(c * TAIL_ROWS, (c + 1) * TAIL_ROWS)
        y = jnp.dot(g_buf[rows, :], wdn_ref[...], preferred_element_type=F32)
        o_ref[rows, :] = _layer_norm(ALPHA * h_ref[rows, :] + y, lng_ref[...], lnb_ref[...])


def _t5_bucket(dist):
    max_exact = N_BUCKETS // 2
    n = jnp.maximum(dist, 1).astype(jnp.float32)
    large = max_exact + (jnp.log(n / max_exact) / math.log(MAX_DISTANCE / max_exact)
                         * (N_BUCKETS - max_exact)).astype(jnp.int32)
    large = jnp.minimum(large, N_BUCKETS - 1)
    return jnp.where(dist < max_exact, dist, large)


def _offset_bias(rel_bias):
    steps = WINDOW_STEPS - jnp.arange(2 * Q_BLOCK)
    rows = []
    for g, dil in enumerate(DILATIONS):
        bucket = _t5_bucket(jnp.maximum(steps, 0) * dil)
        bias = rel_bias[:, g * HEADS:(g + 1) * HEADS][bucket].astype(F32)
        rows.append(jnp.where(steps[:, None] >= 0, bias, NEG).T)
    return jnp.stack(rows, 0)[:, :, None, :]


def _side_cast_plan(rows, max_blocks):
    block = BF16_ROWS
    while rows % block or rows // block > max_blocks:
        block += BF16_ROWS
    return block, rows // block


def _cparams(*sem):
    return pltpu.CompilerParams(dimension_semantics=sem, vmem_limit_bytes=VMEM_LIMIT)


def _layer(x, mem, band_bias, hmask, w_in, w_mem_kv, sg_ln_g, sg_ln_b, w_spatial, b_spatial,
           w_proj_a, w_proj_b, w_proj_c, w_gate, b_gate, w_out, ln1_g, ln1_b,
           w_ffn_up, conv_w, conv_b, w_ffn_down, ln2_g, ln2_b):
    bsz, seq, _ = x.shape
    n_tok = bsz * seq
    tm = TOKEN_TILE
    assert seq % SUPER == 0 and SUPER % tm == 0 and tm % (SG_CHUNK * 2) == 0
    assert D_FF % FF_CHUNK == 0 and tm % (16 * DILATIONS[-1]) == 0
    assert seq % MERGE_TILE == 0 and seq % FFN_TILE == 0
    tiles_per_seq = seq // tm
    row = lambda v: v.reshape(1, -1)
    tok = lambda rows, width: pl.BlockSpec((rows, width), lambda i: (i, 0))
    tile3 = lambda rows, width: pl.BlockSpec((1, rows, width), lambda b, t: (b, t, 0))

    wsp = w_spatial.reshape(SG_W // SG_CHUNK, 2 * SG_CHUNK, SG_CHUNK)
    bsp = jnp.repeat(b_spatial.reshape(SG_W // SG_CHUNK, 2, SG_CHUNK).transpose(0, 2, 1),
                     HEAD_DIM, axis=-1)
    qkv_shapes = ([jax.ShapeDtypeStruct((bsz, seq // d, d * HW), BF16) for d in DILATIONS]
                  + [jax.ShapeDtypeStruct((bsz, seq // d, d * 2 * HW), BF16) for d in DILATIONS])
    qkv_specs = ([tile3(tm // d, d * HW) for d in DILATIONS]
                 + [tile3(tm // d, d * 2 * HW) for d in DILATIONS])
    side_w = ((w_gate, 0.5), (w_proj_a, 0.5), (w_proj_b, 0.5), (w_proj_c, 0.5), (w_out, 1.0),
              (w_ffn_up, 1.0), (w_ffn_down, 1.0))
    assert SIDE_CAST_BLOCKS <= bsz * tiles_per_seq
    plans = [_side_cast_plan(w.shape[0], SIDE_CAST_BLOCKS) for w, _ in side_w]

    def side_spec(w, plan):
        rows, n_blocks = plan
        return pl.BlockSpec((rows, w.shape[1]), lambda b, t: (
            jnp.minimum(b * tiles_per_seq + t, n_blocks - 1), 0))

    side_specs = [side_spec(w, plan) for (w, _), plan in zip(side_w, plans)]
    outs = pl.pallas_call(
        functools.partial(_proj_kernel,
                          side=tuple((scale, plan[1]) for (_, scale), plan in zip(side_w, plans))),
        grid=(bsz, tiles_per_seq),
        in_specs=[tile3(tm, D_MODEL), _resident((D_MODEL, IN_W)),
                  pl.BlockSpec((1, MEM_LEN, D_MODEL), lambda b, t: (b, 0, 0)),
                  _resident((D_MODEL, 2 * HW)), _resident((1, SG_W)), _resident((1, SG_W)),
                  _resident(wsp.shape), _resident(bsp.shape), _resident(hmask.shape)] + side_specs,
        out_specs=qkv_specs + [tile3(tm, SG_W), tile3(tm, HW)] + side_specs,
        out_shape=qkv_shapes + [jax.ShapeDtypeStruct((bsz, seq, SG_W), BF16),
                                jax.ShapeDtypeStruct((bsz, seq, HW), BF16)]
        + [jax.ShapeDtypeStruct(w.shape, BF16) for w, _ in side_w],
        scratch_shapes=[pltpu.VMEM((tm, LANES), F32), pltpu.VMEM((MEM_LEN, 2 * HW), BF16)],
        compiler_params=_cparams("arbitrary", "arbitrary"),
        name="proj",
    )(x, w_in.astype(BF16), mem, w_mem_kv.astype(BF16), row(sg_ln_g), row(sg_ln_b), wsp, bsp,
      hmask, *[w for w, _ in side_w])
    q1, q2, q3, kv1, kv2, kv3, ob, oc = outs[:8]
    wg_b, wpa_b, wpb_b, wpc_b, wout_b, wup_b, wdn_b = outs[8:]

    n_super = seq // SUPER
    super3 = lambda d, width: pl.BlockSpec((1, SUPER // d, d * width), lambda b, j: (b, j, 0))

    def prev3(d):
        nblk = SUPER // d // Q_BLOCK
        return pl.BlockSpec((1, Q_BLOCK, d * 2 * HW),
                            lambda b, j: (b, jnp.maximum(j * nblk - 1, 0), 0))

    oa = pl.pallas_call(
        _attn_kernel,
        grid=(bsz, n_super),
        in_specs=[super3(d, HW) for d in DILATIONS] + [super3(d, 2 * HW) for d in DILATIONS]
        + [prev3(d) for d in DILATIONS] + [_resident(band_bias.shape), _resident(hmask.shape)],
        out_specs=pl.BlockSpec((1, SUPER, HW), lambda b, j: (b, j, 0)),
        out_shape=jax.ShapeDtypeStruct((bsz, seq, HW), BF16),
        scratch_shapes=[pltpu.VMEM((N_GROUPS, HW // LANES, SUPER, LANES), F32)] * 3,
        compiler_params=_cparams("parallel", "parallel"),
        name="attn",
    )(q1, q2, q3, kv1, kv2, kv3, kv1, kv2, kv3, band_bias, hmask)

    h1 = pl.pallas_call(
        _merge_kernel,
        grid=(n_tok // MERGE_TILE,),
        in_specs=[tok(MERGE_TILE, D_MODEL), tok(MERGE_TILE, HW), tok(MERGE_TILE, SG_W),
                  tok(MERGE_TILE, HW),
                  _resident((D_MODEL, 3 * D_MODEL)), _resident((1, 3 * D_MODEL)),
                  _resident((HW, D_MODEL)), _resident((SG_W, D_MODEL)), _resident((HW, D_MODEL)),
                  _resident((D_MODEL, D_MODEL)), _resident((1, D_MODEL)), _resident((1, D_MODEL))],
        out_specs=tok(MERGE_TILE, D_MODEL),
        out_shape=jax.ShapeDtypeStruct((n_tok, D_MODEL), F32),
        compiler_params=_cparams("parallel"),
        name="merge",
    )(x.reshape(n_tok, D_MODEL), oa.reshape(n_tok, HW), ob.reshape(n_tok, SG_W),
      oc.reshape(n_tok, HW), wg_b, row(0.5 * b_gate), wpa_b, wpb_b, wpc_b, wout_b,
      row(ln1_g), row(ln1_b))

    h2 = pl.pallas_call(
        functools.partial(_ffn_kernel, tiles_per_seq=seq // FFN_TILE),
        grid=(n_tok // FFN_TILE,),
        in_specs=[tok(FFN_TILE, D_MODEL), _resident((D_MODEL, 2 * D_FF)), _resident(conv_w.shape),
                  _resident((1, D_FF)), _resident((D_FF, D_MODEL)),
                  _resident((1, D_MODEL)), _resident((1, D_MODEL))],
        out_specs=tok(FFN_TILE, D_MODEL),
        out_shape=jax.ShapeDtypeStruct((n_tok, D_MODEL), F32),
        scratch_shapes=[pltpu.VMEM((CARRY_ROWS + FFN_TILE, FF_CHUNK), F32),
                        pltpu.VMEM((D_FF // FF_CHUNK, CARRY_ROWS, FF_CHUNK), F32),
                        pltpu.VMEM((FFN_TILE, D_FF), BF16)],
        compiler_params=_cparams("arbitrary"),
        name="ffn",
    )(h1, wup_b, conv_w, row(conv_b), wdn_b, row(ln2_g), row(ln2_b))
    return h2.reshape(bsz, seq, D_MODEL)


def kernel(x, mem, rel_bias, w_in, w_mem_kv, sg_ln_g, sg_ln_b, w_spatial, b_spatial, w_proj_a,
           w_proj_b, w_proj_c, w_gate, b_gate, w_out, ln1_g, ln1_b, w_ffn_up, conv_w, conv_b,
           w_ffn_down, ln2_g, ln2_b):
    band_bias = pl.pallas_call(
        _band_bias_kernel,
        grid=(N_GROUPS,),
        in_specs=[pl.BlockSpec((1, HEADS, 1, 2 * Q_BLOCK), lambda g: (g, 0, 0, 0))],
        out_specs=pl.BlockSpec((1, 2, HEADS * Q_BLOCK, 2 * Q_BLOCK), lambda g: (g, 0, 0, 0)),
        out_shape=jax.ShapeDtypeStruct((N_GROUPS, 2, HEADS * Q_BLOCK, 2 * Q_BLOCK), F32),
        name="band_bias",
    )(_offset_bias(rel_bias))
    lane = jnp.arange(HW)[None, None, :] // HEAD_DIM
    hmask = jnp.broadcast_to(lane == jnp.arange(HEADS)[:, None, None],
                             (HEADS, Q_BLOCK, HW)).astype(BF16)
    h = x
    for l in range(w_in.shape[0]):
        h = _layer(h, mem, band_bias, hmask, w_in[l], w_mem_kv[l], sg_ln_g[l], sg_ln_b[l],
                   w_spatial[l], b_spatial[l], w_proj_a[l], w_proj_b[l], w_proj_c[l], w_gate[l],
                   b_gate[l], w_out[l], ln1_g[l], ln1_b[l], w_ffn_up[l], conv_w[l], conv_b[l],
                   w_ffn_down[l], ln2_g[l], ln2_b[l])
    return h
```

```python
import functools
import math

import jax
import jax.numpy as jnp
from jax import lax
from jax.experimental import pallas as pl
from jax.experimental.pallas import tpu as pltpu

F32 = jnp.float32
BF16 = jnp.bfloat16

LANES = 128
BF16_ROWS = 16
D_MODEL = 1024
HEAD_DIM = 64
DILATIONS = (1, 4, 16)
N_GROUPS = 3
HEADS = 4
HW = HEADS * HEAD_DIM
ATTN_W = N_GROUPS * HW
WINDOW_STEPS = 128
Q_BLOCK = 128
SG_CHUNK = 128
SG_W = 512
MEM_LEN = 256
IN_W = 3 * ATTN_W + 2 * SG_W + HW
D_FF = 2816
N_BUCKETS = 32
MAX_DISTANCE = 2048
LN_EPS = 1e-5
DEPTH = 1
ALPHA = (2 * DEPTH) ** 0.25
ATTN_SCALE = HEAD_DIM ** -0.5
LOG2E = math.log2(math.e)
Q_SCALE = ATTN_SCALE * LOG2E
NEG = -1e30

TOKEN_TILE = 512
MERGE_TILE = 1024
FFN_TILE = 1024
SUPER = Q_BLOCK * DILATIONS[-1]
PADDED_DIL = DILATIONS[-1]
PAD_PITCH = 24
SIDE_CAST_BLOCKS = 8
TAIL_ROWS = 256
COMBINE_ROWS = 256
FF_CHUNK = 256
CARRY_ROWS = 8
VMEM_LIMIT = 56 * 1024 * 1024

_NT = (((1,), (1,)), ((), ()))


def _gelu(x):
    return 0.5 * x * (1.0 + jnp.tanh(math.sqrt(2.0 / math.pi) * (x + 0.044715 * (x * x * x))))


def _layer_norm(x, g, b):
    mu = jnp.mean(x, axis=-1, keepdims=True)
    xc = x - mu
    var = jnp.mean(xc * xc, axis=-1, keepdims=True)
    return xc * lax.rsqrt(var + LN_EPS) * g + b


def _resident(shape):
    zeros = (0,) * len(shape)
    return pl.BlockSpec(shape, lambda *_: zeros, pipeline_mode=pl.Buffered(1))


def _head_dense(per_head):
    low = lax.broadcasted_iota(jnp.int32, (per_head[0].shape[0], LANES), 1) < HEAD_DIM
    return jnp.concatenate([jnp.where(low, per_head[h], per_head[h + 1])
                            for h in range(0, HEADS, 2)], axis=1)


def _packed_heads_attention(q, k, v, hm_ref, bias_fn):
    m_rows = q.shape[0]
    qs = jnp.concatenate([q * hm_ref[h] for h in range(HEADS)], axis=0)
    s = lax.dot_general(qs, k, _NT, preferred_element_type=F32)
    ps, ms, ls = [], [], []
    for h in range(HEADS):
        sh = s[h * m_rows:(h + 1) * m_rows]
        bias = bias_fn(h)
        if bias is not None:
            sh = sh + bias
        m = jnp.max(sh, axis=-1, keepdims=True)
        p = jnp.exp2(sh - m)
        ps.append(p.astype(BF16))
        ms.append(m)
        ls.append(jnp.sum(p, axis=-1, keepdims=True))
    pv = jnp.dot(jnp.concatenate(ps, axis=0), v, preferred_element_type=F32)
    half = lambda h: slice((h // 2) * LANES, (h // 2 + 1) * LANES)
    pv = _head_dense([pv[h * m_rows:(h + 1) * m_rows, half(h)] for h in range(HEADS)])
    return pv, _head_dense(ms), _head_dense(ls)


def _band_bias_kernel(f_ref, o_ref):
    lane = lax.broadcasted_iota(jnp.int32, (Q_BLOCK, 2 * Q_BLOCK), 1)
    no_prev = jnp.where(lane < Q_BLOCK, NEG, 0.0)
    for h in range(HEADS):
        rep = jnp.broadcast_to(f_ref[0, h] * LOG2E, (Q_BLOCK, 2 * Q_BLOCK))
        band = pltpu.roll(rep, 0, 1, stride=1, stride_axis=0)
        o_ref[0, 0, h * Q_BLOCK:(h + 1) * Q_BLOCK] = band
        o_ref[0, 1, h * Q_BLOCK:(h + 1) * Q_BLOCK] = band + no_prev


def _proj_kernel(*refs, side):
    n_side = len(side)
    x_ref, win_ref, mem_ref, wkv_ref, lng_ref, lnb_ref, wsp_ref, bsp_ref, hm_ref = refs[:9]
    side_in = refs[9:9 + n_side]
    q1_ref, q2_ref, q3_ref, kv1_ref, kv2_ref, kv3_ref, ob_ref, oc_ref = refs[9 + n_side:17 + n_side]
    side_out = refs[17 + n_side:17 + 2 * n_side]
    de_buf, kvm = refs[17 + 2 * n_side:]
    xb = x_ref[0].astype(BF16)
    tm = xb.shape[0]

    step = pl.program_id(0) * pl.num_programs(1) + pl.program_id(1)
    for (scale, n_blocks), w_ref, o_ref in zip(side, side_in, side_out):
        @pl.when(step < n_blocks)
        def _(scale=scale, w_ref=w_ref, o_ref=o_ref):
            o_ref[...] = (w_ref[...] * scale).astype(BF16)

    @pl.when(pl.program_id(1) == 0)
    def _():
        kvm[...] = jnp.dot(mem_ref[0].astype(BF16), wkv_ref[...],
                           preferred_element_type=F32).astype(BF16)

    def proj(lo, width):
        return jnp.dot(xb, win_ref[:, lo:lo + width], preferred_element_type=F32)

    def put(ref, lane0, val, dil):
        if dil == 1:
            ref[0, :, lane0:lane0 + HW] = val.astype(BF16)
            return
        stride_cols = ref.shape[2] // dil
        for c in range(HW // LANES):
            de_buf[...] = val[:, c * LANES:(c + 1) * LANES]
            for r in range(dil):
                piece = de_buf[pl.ds(r, tm // dil, stride=dil), :]
                col = r * stride_cols + lane0 + c * LANES
                ref[0, :, col:col + LANES] = piece.astype(BF16)

    def a_proj(g, part):
        q_ref, kv_ref = ((q1_ref, kv1_ref), (q2_ref, kv2_ref), (q3_ref, kv3_ref))[g]
        if part == 0:
            put(q_ref, 0, proj(g * HW, HW) * Q_SCALE, DILATIONS[g])
        else:
            put(kv_ref, (part - 1) * HW, proj(part * ATTN_W + g * HW, HW), DILATIONS[g])

    fillers = iter([(g, part) for g in range(N_GROUPS) for part in range(3)])
    fill = lambda: a_proj(*next(fillers))

    def proj_halves(lo):
        return jnp.concatenate([proj(lo, HW), proj(lo + HW, HW)], axis=1)

    qc = (proj(3 * ATTN_W + 2 * SG_W, HW) * Q_SCALE).astype(BF16)
    u_pre = proj_halves(3 * ATTN_W)
    v_pre = proj_halves(3 * ATTN_W + SG_W)

    n_chunks = tm // SG_CHUNK
    u, vn = [], []
    for c in range(n_chunks):
        rows = slice(c * SG_CHUNK, (c + 1) * SG_CHUNK)
        fill()
        u.append(_gelu(u_pre[rows]))
        vn.append(_layer_norm(_gelu(v_pre[rows]), lng_ref[...], lnb_ref[...]).astype(BF16))
        pv, _, l = _packed_heads_attention(qc[rows], kvm[:, :HW], kvm[:, HW:], hm_ref,
                                           lambda h: None)
        oc_ref[0, rows, :] = (pv * (1.0 / l)).astype(BF16)
        fill()

    row = lax.broadcasted_iota(jnp.int32, (2 * SG_CHUNK, SG_CHUNK), 0) & (SG_CHUNK - 1)
    col = lax.broadcasted_iota(jnp.int32, (2 * SG_CHUNK, SG_CHUNK), 1)
    low_half = lax.broadcasted_iota(jnp.int32, (SG_CHUNK, SG_CHUNK), 1) < HEAD_DIM
    for j in range(SG_W // SG_CHUNK):
        cols = slice(j * SG_CHUNK, (j + 1) * SG_CHUNK)
        w_pair = jnp.where(row >= col, wsp_ref[j], 0.0).astype(BF16)
        rhs = jnp.concatenate([vn[c][:, cols] for c in range(n_chunks)], axis=1)
        r = jnp.dot(w_pair, rhs, preferred_element_type=F32)
        for c in range(n_chunks):
            rows = slice(c * SG_CHUNK, (c + 1) * SG_CHUNK)
            sv = jnp.where(low_half, r[:SG_CHUNK, rows], r[SG_CHUNK:, rows]) + bsp_ref[j]
            ob_ref[0, rows, cols] = (u[c][:, cols] * sv).astype(BF16)
        if j == 0:
            fill()


def _attn_kernel(q1_ref, q2_ref, q3_ref, kv1_ref, kv2_ref, kv3_ref, kp1_ref, kp2_ref, kp3_ref,
                 bias_ref, hm_ref, oa_ref, pv_nat, m_nat, l_nat, pv_pad, m_pad, l_pad):
    lead = (pl.program_id(1) == 0).astype(jnp.int32)

    groups = ((q1_ref, kv1_ref, kp1_ref), (q2_ref, kv2_ref, kp2_ref), (q3_ref, kv3_ref, kp3_ref))
    for g, (q_ref, kv_ref, kvp_ref) in enumerate(groups):
        dil = DILATIONS[g]
        nblk = SUPER // dil // Q_BLOCK
        for r in range(dil):
            kc = slice(r * 2 * HW, r * 2 * HW + HW)
            vc = slice(r * 2 * HW + HW, (r + 1) * 2 * HW)
            qc = slice(r * HW, (r + 1) * HW)
            for i in range(nblk):
                rq = slice(i * Q_BLOCK, (i + 1) * Q_BLOCK)
                if i == 0:
                    k2 = jnp.concatenate([kvp_ref[0, :, kc], kv_ref[0, rq, kc]], axis=0)
                    v2 = jnp.concatenate([kvp_ref[0, :, vc], kv_ref[0, rq, vc]], axis=0)
                else:
                    rk = slice((i - 1) * Q_BLOCK, (i + 1) * Q_BLOCK)
                    k2 = kv_ref[0, rk, kc]
                    v2 = kv_ref[0, rk, vc]
                table = lead if i == 0 else 0
                pv, m, l = _packed_heads_attention(
                    q_ref[0, rq, qc], k2, v2, hm_ref,
                    lambda h, table=table: bias_ref[g, table, h * Q_BLOCK:(h + 1) * Q_BLOCK, :])
                for c in range(HW // LANES):
                    lanes = slice(c * LANES, (c + 1) * LANES)
                    if dil == PADDED_DIL:
                        at = (c, pl.ds(i * Q_BLOCK * PAD_PITCH + r, Q_BLOCK, stride=PAD_PITCH))
                        dst = (pv_pad, m_pad, l_pad)
                    else:
                        nat = pl.ds(i * Q_BLOCK * dil + r, Q_BLOCK, stride=dil) if dil > 1 else rq
                        at = (g, c, nat)
                        dst = (pv_nat, m_nat, l_nat)
                    for ref, val in zip(dst, (pv, m, l)):
                        ref[at + (slice(None),)] = val[:, lanes]

    def padded_rows(ref, c, t):
        runs = COMBINE_ROWS // PADDED_DIL
        return jnp.concatenate([ref[c, pl.ds((t * runs + k) * PAD_PITCH, PADDED_DIL), :]
                                for k in range(runs)], axis=0)

    for t in range(SUPER // COMBINE_ROWS):
        rows = slice(t * COMBINE_ROWS, (t + 1) * COMBINE_ROWS)
        for c in range(HW // LANES):
            m1, m2, m3 = m_nat[0, c, rows, :], m_nat[1, c, rows, :], padded_rows(m_pad, c, t)
            m = jnp.maximum(jnp.maximum(m1, m2), m3)
            e1, e2, e3 = jnp.exp2(m1 - m), jnp.exp2(m2 - m), jnp.exp2(m3 - m)
            num = (e1 * pv_nat[0, c, rows, :] + e2 * pv_nat[1, c, rows, :]
                   + e3 * padded_rows(pv_pad, c, t))
            den = (e1 * l_nat[0, c, rows, :] + e2 * l_nat[1, c, rows, :]
                   + e3 * padded_rows(l_pad, c, t))
            oa_ref[0, rows, c * LANES:(c + 1) * LANES] = (num * (1.0 / den)).astype(BF16)


def _merge_kernel(x_ref, oa_ref, ob_ref, oc_ref, wg_ref, bg_ref, wpa_ref, wpb_ref, wpc_ref,
                  wout_ref, lng_ref, lnb_ref, h_ref):
    x = x_ref[...]
    xb = x.astype(BF16)

    def gated(k, o_ref, wp_ref):
        cols = slice(k * D_MODEL, (k + 1) * D_MODEL)
        t = jnp.tanh(jnp.dot(xb, wg_ref[:, cols], preferred_element_type=F32) + bg_ref[:, cols])
        half_p = jnp.dot(o_ref[...], wp_ref[...], preferred_element_type=F32)
        return half_p + half_p * t

    merged = (gated(0, oa_ref, wpa_ref) + gated(1, ob_ref, wpb_ref)
              + gated(2, oc_ref, wpc_ref)).astype(BF16)
    for c in range(x_ref.shape[0] // TAIL_ROWS):
        rows = slice(c * TAIL_ROWS, (c + 1) * TAIL_ROWS)
        y = jnp.dot(merged[rows], wout_ref[...], preferred_element_type=F32)
        h_ref[rows, :] = _layer_norm(ALPHA * x_ref[rows, :] + y, lng_ref[...], lnb_ref[...])


def _ffn_kernel(h_ref, wup_ref, cw_ref, cb_ref, wdn_ref, lng_ref, lnb_ref, o_ref,
                a_buf, carry, g_buf, *, tiles_per_seq):
    tm = h_ref.shape[0]
    n_chunks = D_FF // FF_CHUNK

    @pl.when(pl.program_id(0) % tiles_per_seq == 0)
    def _():
        carry[...] = jnp.zeros(carry.shape, F32)

    h = h_ref[...]
    hb = h.astype(BF16)
    for c in range(n_chunks):
        cols = slice(c * FF_CHUNK, (c + 1) * FF_CHUNK)
        a = jnp.dot(hb, wup_ref[:, cols], preferred_element_type=F32)
        gate = jnp.dot(hb, wup_ref[:, D_FF + c * FF_CHUNK:D_FF + (c + 1) * FF_CHUNK],
                       preferred_element_type=F32)
        a_buf[:CARRY_ROWS] = carry[c]
        a_buf[CARRY_ROWS:] = a
        carry[c] = a[tm - CARRY_ROWS:]
        conv = (cw_ref[0:1, cols] * a_buf[CARRY_ROWS - 2:CARRY_ROWS - 2 + tm]
                + cw_ref[1:2, cols] * a_buf[CARRY_ROWS - 1:CARRY_ROWS - 1 + tm]
                + cw_ref[2:3, cols] * a + cb_ref[:, cols])
        g_buf[:, cols] = (_gelu(conv) * gate).astype(BF16)
    for c in range(tm // TAIL_ROWS):
        rows = slice(c * TAIL_ROWS, (c + 1) * TAIL_ROWS)
        y = jnp.dot(g_buf[rows, :], wdn_ref[...], preferred_element_type=F32)
        o_ref[rows, :] = _layer_norm(ALPHA * h_ref[rows, :] + y, lng_ref[...], lnb_ref[...])


def _t5_bucket(dist):
    max_exact = N_BUCKETS // 2
    n = jnp.maximum(dist, 1).astype(jnp.float32)
    large = max_exact + (jnp.log(n / max_exact) / math.log(MAX_DISTANCE / max_exact)
                         * (N_BUCKETS - max_exact)).astype(jnp.int32)
    large = jnp.minimum(large, N_BUCKETS - 1)
    return jnp.where(dist < max_exact, dist, large)


def _offset_bias(rel_bias):
    steps = WINDOW_STEPS - jnp.arange(2 * Q_BLOCK)
    rows = []
    for g, dil in enumerate(DILATIONS):
        bucket = _t5_bucket(jnp.maximum(steps, 0) * dil)
        bias = rel_bias[:, g * HEADS:(g + 1) * HEADS][bucket].astype(F32)
        rows.append(jnp.where(steps[:, None] >= 0, bias, NEG).T)
    return jnp.stack(rows, 0)[:, :, None, :]


def _side_cast_plan(rows, max_blocks):
    block = BF16_ROWS
    while rows % block or rows // block > max_blocks:
        block += BF16_ROWS
    return block, rows // block


def _cparams(*sem):
    return pltpu.CompilerParams(dimension_semantics=sem, vmem_limit_bytes=VMEM_LIMIT)


def _layer(x, mem, band_bias, hmask, w_in, w_mem_kv, sg_ln_g, sg_ln_b, w_spatial, b_spatial,
           w_proj_a, w_proj_b, w_proj_c, w_gate, b_gate, w_out, ln1_g, ln1_b,
           w_ffn_up, conv_w, conv_b, w_ffn_down, ln2_g, ln2_b):
    bsz, seq, _ = x.shape
    n_tok = bsz * seq
    tm = TOKEN_TILE
    assert seq % SUPER == 0 and SUPER % tm == 0 and tm % (SG_CHUNK * 2) == 0
    assert D_FF % FF_CHUNK == 0 and tm % (16 * DILATIONS[-1]) == 0
    assert seq % MERGE_TILE == 0 and seq % FFN_TILE == 0
    tiles_per_seq = seq // tm
    row = lambda v: v.reshape(1, -1)
    tok = lambda rows, width: pl.BlockSpec((rows, width), lambda i: (i, 0))
    tile3 = lambda rows, width: pl.BlockSpec((1, rows, width), lambda b, t: (b, t, 0))

    wsp = w_spatial.reshape(SG_W // SG_CHUNK, 2 * SG_CHUNK, SG_CHUNK)
    bsp = jnp.repeat(b_spatial.reshape(SG_W // SG_CHUNK, 2, SG_CHUNK).transpose(0, 2, 1),
                     HEAD_DIM, axis=-1)
    qkv_shapes = ([jax.ShapeDtypeStruct((bsz, seq // d, d * HW), BF16) for d in DILATIONS]
                  + [jax.ShapeDtypeStruct((bsz, seq // d, d * 2 * HW), BF16) for d in DILATIONS])
    qkv_specs = ([tile3(tm // d, d * HW) for d in DILATIONS]
                 + [tile3(tm // d, d * 2 * HW) for d in DILATIONS])
    side_w = ((w_gate, 0.5), (w_proj_a, 0.5), (w_proj_b, 0.5), (w_proj_c, 0.5), (w_out, 1.0),
              (w_ffn_up, 1.0), (w_ffn_down, 1.0))
    assert SIDE_CAST_BLOCKS <= bsz * tiles_per_seq
    plans = [_side_cast_plan(w.shape[0], SIDE_CAST_BLOCKS) for w, _ in side_w]

    def side_spec(w, plan):
        rows, n_blocks = plan
        return pl.BlockSpec((rows, w.shape[1]), lambda b, t: (
            jnp.minimum(b * tiles_per_seq + t, n_blocks - 1), 0))

    side_specs = [side_spec(w, plan) for (w, _), plan in zip(side_w, plans)]
    outs = pl.pallas_call(
        functools.partial(_proj_kernel,
                          side=tuple((scale, plan[1]) for (_, scale), plan in zip(side_w, plans))),
        grid=(bsz, tiles_per_seq),
        in_specs=[tile3(tm, D_MODEL), _resident((D_MODEL, IN_W)),
                  pl.BlockSpec((1, MEM_LEN, D_MODEL), lambda b, t: (b, 0, 0)),
                  _resident((D_MODEL, 2 * HW)), _resident((1, SG_W)), _resident((1, SG_W)),
                  _resident(wsp.shape), _resident(bsp.shape), _resident(hmask.shape)] + side_specs,
        out_specs=qkv_specs + [tile3(tm, SG_W), tile3(tm, HW)] + side_specs,
        out_shape=qkv_shapes + [jax.ShapeDtypeStruct((bsz, seq, SG_W), BF16),
                                jax.ShapeDtypeStruct((bsz, seq, HW), BF16)]
        + [jax.ShapeDtypeStruct(w.shape, BF16) for w, _ in side_w],
        scratch_shapes=[pltpu.VMEM((tm, LANES), F32), pltpu.VMEM((MEM_LEN, 2 * HW), BF16)],
        compiler_params=_cparams("arbitrary", "arbitrary"),
        name="proj",
    )(x, w_in.astype(BF16), mem, w_mem_kv.astype(BF16), row(sg_ln_g), row(sg_ln_b), wsp, bsp,
      hmask, *[w for w, _ in side_w])
    q1, q2, q3, kv1, kv2, kv3, ob, oc = outs[:8]
    wg_b, wpa_b, wpb_b, wpc_b, wout_b, wup_b, wdn_b = outs[8:]

    n_super = seq // SUPER
    super3 = lambda d, width: pl.BlockSpec((1, SUPER // d, d * width), lambda b, j: (b, j, 0))

    def prev3(d):
        nblk = SUPER // d // Q_BLOCK
        return pl.BlockSpec((1, Q_BLOCK, d * 2 * HW),
                            lambda b, j: (b, jnp.maximum(j * nblk - 1, 0), 0))

    oa = pl.pallas_call(
        _attn_kernel,
        grid=(bsz, n_super),
        in_specs=[super3(d, HW) for d in DILATIONS] + [super3(d, 2 * HW) for d in DILATIONS]
        + [prev3(d) for d in DILATIONS] + [_resident(band_bias.shape), _resident(hmask.shape)],
        out_specs=pl.BlockSpec((1, SUPER, HW), lambda b, j: (b, j, 0)),
        out_shape=jax.ShapeDtypeStruct((bsz, seq, HW), BF16),
        scratch_shapes=[pltpu.VMEM((N_GROUPS - 1, HW // LANES, SUPER, LANES), F32)] * 3
        + [pltpu.VMEM((HW // LANES, SUPER // PADDED_DIL * PAD_PITCH, LANES), F32)] * 3,
        compiler_params=_cparams("parallel", "parallel"),
        name="attn",
    )(q1, q2, q3, kv1, kv2, kv3, kv1, kv2, kv3, band_bias, hmask)

    h1 = pl.pallas_call(
        _merge_kernel,
        grid=(n_tok // MERGE_TILE,),
        in_specs=[tok(MERGE_TILE, D_MODEL), tok(MERGE_TILE, HW), tok(MERGE_TILE, SG_W),
                  tok(MERGE_TILE, HW),
                  _resident((D_MODEL, 3 * D_MODEL)), _resident((1, 3 * D_MODEL)),
                  _resident((HW, D_MODEL)), _resident((SG_W, D_MODEL)), _resident((HW, D_MODEL)),
                  _resident((D_MODEL, D_MODEL)), _resident((1, D_MODEL)), _resident((1, D_MODEL))],
        out_specs=tok(MERGE_TILE, D_MODEL),
        out_shape=jax.ShapeDtypeStruct((n_tok, D_MODEL), F32),
        compiler_params=_cparams("parallel"),
        name="merge",
    )(x.reshape(n_tok, D_MODEL), oa.reshape(n_tok, HW), ob.reshape(n_tok, SG_W),
      oc.reshape(n_tok, HW), wg_b, row(0.5 * b_gate), wpa_b, wpb_b, wpc_b, wout_b,
      row(ln1_g), row(ln1_b))

    h2 = pl.pallas_call(
        functools.partial(_ffn_kernel, tiles_per_seq=seq // FFN_TILE),
        grid=(n_tok // FFN_TILE,),
        in_specs=[tok(FFN_TILE, D_MODEL), _resident((D_MODEL, 2 * D_FF)), _resident(conv_w.shape),
                  _resident((1, D_FF)), _resident((D_FF, D_MODEL)),
                  _resident((1, D_MODEL)), _resident((1, D_MODEL))],
        out_specs=tok(FFN_TILE, D_MODEL),
        out_shape=jax.ShapeDtypeStruct((n_tok, D_MODEL), F32),
        scratch_shapes=[pltpu.VMEM((CARRY_ROWS + FFN_TILE, FF_CHUNK), F32),
                        pltpu.VMEM((D_FF // FF_CHUNK, CARRY_ROWS, FF_CHUNK), F32),
                        pltpu.VMEM((FFN_TILE, D_FF), BF16)],
        compiler_params=_cparams("arbitrary"),
        name="ffn",
    )(h1, wup_b, conv_w, row(conv_b), wdn_b, row(ln2_g), row(ln2_b))
    return h2.reshape(bsz, seq, D_MODEL)


def kernel(x, mem, rel_bias, w_in, w_mem_kv, sg_ln_g, sg_ln_b, w_spatial, b_spatial, w_proj_a,
           w_proj_b, w_proj_c, w_gate, b_gate, w_out, ln1_g, ln1_b, w_ffn_up, conv_w, conv_b,
           w_ffn_down, ln2_g, ln2_b):
    band_bias = pl.pallas_call(
        _band_bias_kernel,
        grid=(N_GROUPS,),
        in_specs=[pl.BlockSpec((1, HEADS, 1, 2 * Q_BLOCK), lambda g: (g, 0, 0, 0))],
        out_specs=pl.BlockSpec((1, 2, HEADS * Q_BLOCK, 2 * Q_BLOCK), lambda g: (g, 0, 0, 0)),
        out_shape=jax.ShapeDtypeStruct((N_GROUPS, 2, HEADS * Q_BLOCK, 2 * Q_BLOCK), F32),
        name="band_bias",
    )(_offset_bias(rel_bias))
    lane = jnp.arange(HW)[None, None, :] // HEAD_DIM
    hmask = jnp.broadcast_to(lane == jnp.arange(HEADS)[:, None, None],
                             (HEADS, Q_BLOCK, HW)).astype(BF16)
    h = x
    for l in range(w_in.shape[0]):
        h = _layer(h, mem, band_bias, hmask, w_in[l], w_mem_kv[l], sg_ln_g[l], sg_ln_b[l],
                   w_spatial[l], b_spatial[l], w_proj_a[l], w_proj_b[l], w_proj_c[l], w_gate[l],
                   b_gate[l], w_out[l], ln1_g[l], ln1_b[l], w_ffn_up[l], conv_w[l], conv_b[l],
                   w_ffn_down[l], ln2_g[l], ln2_b[l])
    return h
```

```python
import functools
import math

import jax
import jax.numpy as jnp
from jax import lax
from jax.experimental import pallas as pl
from jax.experimental.pallas import tpu as pltpu

F32 = jnp.float32
BF16 = jnp.bfloat16

LANES = 128
BF16_ROWS = 16
D_MODEL = 1024
HEAD_DIM = 64
DILATIONS = (1, 4, 16)
N_GROUPS = 3
HEADS = 4
HW = HEADS * HEAD_DIM
ATTN_W = N_GROUPS * HW
WINDOW_STEPS = 128
Q_BLOCK = 128
SG_CHUNK = 128
SG_W = 512
MEM_LEN = 256
IN_W = 3 * ATTN_W + 2 * SG_W + HW
D_FF = 2816
N_BUCKETS = 32
MAX_DISTANCE = 2048
LN_EPS = 1e-5
DEPTH = 1
ALPHA = (2 * DEPTH) ** 0.25
ATTN_SCALE = HEAD_DIM ** -0.5
LOG2E = math.log2(math.e)
Q_SCALE = ATTN_SCALE * LOG2E
NEG = -1e30

TOKEN_TILE = 512
MERGE_TILE = 1024
FFN_TILE = 1024
SUPER = Q_BLOCK * DILATIONS[-1]
PADDED_DIL = DILATIONS[-1]
PAD_PITCH = 24
SIDE_CAST_BLOCKS = 8
TAIL_ROWS = 256
COMBINE_ROWS = 256
FF_CHUNK = 256
CARRY_ROWS = 8
VMEM_LIMIT = 56 * 1024 * 1024

_NT = (((1,), (1,)), ((), ()))


def _gelu(x):
    return 0.5 * x * (1.0 + jnp.tanh(math.sqrt(2.0 / math.pi) * (x + 0.044715 * (x * x * x))))


def _layer_norm(x, g, b):
    mu = jnp.mean(x, axis=-1, keepdims=True)
    xc = x - mu
    var = jnp.mean(xc * xc, axis=-1, keepdims=True)
    return xc * lax.rsqrt(var + LN_EPS) * g + b


def _resident(shape):
    zeros = (0,) * len(shape)
    return pl.BlockSpec(shape, lambda *_: zeros, pipeline_mode=pl.Buffered(1))


def _head_dense(per_head):
    low = lax.broadcasted_iota(jnp.int32, (per_head[0].shape[0], LANES), 1) < HEAD_DIM
    return jnp.concatenate([jnp.where(low, per_head[h], per_head[h + 1])
                            for h in range(0, HEADS, 2)], axis=1)


def _packed_heads_attention(q, k, v, hm_ref, bias_fn):
    m_rows = q.shape[0]
    qs = jnp.concatenate([q * hm_ref[h] for h in range(HEADS)], axis=0)
    s = lax.dot_general(qs, k, _NT, preferred_element_type=F32)
    ps, ms, ls = [], [], []
    for h in range(HEADS):
        sh = s[h * m_rows:(h + 1) * m_rows]
        bias = bias_fn(h)
        if bias is not None:
            sh = sh + bias
        m = jnp.max(sh, axis=-1, keepdims=True)
        p = jnp.exp2(sh - m)
        ps.append(p.astype(BF16))
        ms.append(m)
        ls.append(jnp.sum(p, axis=-1, keepdims=True))
    pv = jnp.dot(jnp.concatenate(ps, axis=0), v, preferred_element_type=F32)
    half = lambda h: slice((h // 2) * LANES, (h // 2 + 1) * LANES)
    pv = _head_dense([pv[h * m_rows:(h + 1) * m_rows, half(h)] for h in range(HEADS)])
    return pv, _head_dense(ms), _head_dense(ls)


def _band_bias_kernel(f_ref, o_ref):
    lane = lax.broadcasted_iota(jnp.int32, (Q_BLOCK, 2 * Q_BLOCK), 1)
    no_prev = jnp.where(lane < Q_BLOCK, NEG, 0.0)
    for h in range(HEADS):
        rep = jnp.broadcast_to(f_ref[0, h] * LOG2E, (Q_BLOCK, 2 * Q_BLOCK))
        band = pltpu.roll(rep, 0, 1, stride=1, stride_axis=0)
        o_ref[0, 0, h * Q_BLOCK:(h + 1) * Q_BLOCK] = band
        o_ref[0, 1, h * Q_BLOCK:(h + 1) * Q_BLOCK] = band + no_prev


def _proj_kernel(*refs, side):
    n_side = len(side)
    x_ref, win_ref, mem_ref, wkv_ref, lng_ref, lnb_ref, wsp_ref, bsp_ref, hm_ref = refs[:9]
    side_in = refs[9:9 + n_side]
    q1_ref, q2_ref, q3_ref, kv1_ref, kv2_ref, kv3_ref, ob_ref, oc_ref = refs[9 + n_side:17 + n_side]
    side_out = refs[17 + n_side:17 + 2 * n_side]
    de_buf, kvm = refs[17 + 2 * n_side:]
    xb = x_ref[0].astype(BF16)
    tm = xb.shape[0]

    step = pl.program_id(0) * pl.num_programs(1) + pl.program_id(1)
    for (scale, n_blocks), w_ref, o_ref in zip(side, side_in, side_out):
        @pl.when(step < n_blocks)
        def _(scale=scale, w_ref=w_ref, o_ref=o_ref):
            o_ref[...] = (w_ref[...] * scale).astype(BF16)

    @pl.when(pl.program_id(1) == 0)
    def _():
        kvm[...] = jnp.dot(mem_ref[0].astype(BF16), wkv_ref[...],
                           preferred_element_type=F32).astype(BF16)

    def proj(lo, width):
        return jnp.dot(xb, win_ref[:, lo:lo + width], preferred_element_type=F32)

    def put(ref, lane0, val, dil):
        if dil == 1:
            ref[0, :, lane0:lane0 + HW] = val.astype(BF16)
            return
        stride_cols = ref.shape[2] // dil
        pitch = PAD_PITCH if dil == PADDED_DIL else dil
        for c in range(HW // LANES):
            half = val[:, c * LANES:(c + 1) * LANES]
            if pitch == dil:
                de_buf[:tm] = half
            else:
                for i in range(tm // dil):
                    de_buf[i * pitch:i * pitch + dil] = half[i * dil:(i + 1) * dil]
            for r in range(dil):
                piece = de_buf[pl.ds(r, tm // dil, stride=pitch), :]
                col = r * stride_cols + lane0 + c * LANES
                ref[0, :, col:col + LANES] = piece.astype(BF16)

    def a_proj(g, part):
        q_ref, kv_ref = ((q1_ref, kv1_ref), (q2_ref, kv2_ref), (q3_ref, kv3_ref))[g]
        if part == 0:
            put(q_ref, 0, proj(g * HW, HW) * Q_SCALE, DILATIONS[g])
        else:
            put(kv_ref, (part - 1) * HW, proj(part * ATTN_W + g * HW, HW), DILATIONS[g])

    fillers = iter([(g, part) for g in range(N_GROUPS) for part in range(3)])
    fill = lambda: a_proj(*next(fillers))

    def proj_halves(lo):
        return jnp.concatenate([proj(lo, HW), proj(lo + HW, HW)], axis=1)

    qc = (proj(3 * ATTN_W + 2 * SG_W, HW) * Q_SCALE).astype(BF16)
    u_pre = proj_halves(3 * ATTN_W)
    v_pre = proj_halves(3 * ATTN_W + SG_W)

    n_chunks = tm // SG_CHUNK
    u, vn = [], []
    for c in range(n_chunks):
        rows = slice(c * SG_CHUNK, (c + 1) * SG_CHUNK)
        fill()
        u.append(_gelu(u_pre[rows]))
        vn.append(_layer_norm(_gelu(v_pre[rows]), lng_ref[...], lnb_ref[...]).astype(BF16))
        pv, _, l = _packed_heads_attention(qc[rows], kvm[:, :HW], kvm[:, HW:], hm_ref,
                                           lambda h: None)
        oc_ref[0, rows, :] = (pv * (1.0 / l)).astype(BF16)
        fill()

    row = lax.broadcasted_iota(jnp.int32, (2 * SG_CHUNK, SG_CHUNK), 0) & (SG_CHUNK - 1)
    col = lax.broadcasted_iota(jnp.int32, (2 * SG_CHUNK, SG_CHUNK), 1)
    low_half = lax.broadcasted_iota(jnp.int32, (SG_CHUNK, SG_CHUNK), 1) < HEAD_DIM
    for j in range(SG_W // SG_CHUNK):
        cols = slice(j * SG_CHUNK, (j + 1) * SG_CHUNK)
        w_pair = jnp.where(row >= col, wsp_ref[j], 0.0).astype(BF16)
        rhs = jnp.concatenate([vn[c][:, cols] for c in range(n_chunks)], axis=1)
        r = jnp.dot(w_pair, rhs, preferred_element_type=F32)
        for c in range(n_chunks):
            rows = slice(c * SG_CHUNK, (c + 1) * SG_CHUNK)
            sv = jnp.where(low_half, r[:SG_CHUNK, rows], r[SG_CHUNK:, rows]) + bsp_ref[j]
            ob_ref[0, rows, cols] = (u[c][:, cols] * sv).astype(BF16)
        if j == 0:
            fill()


def _attn_kernel(q1_ref, q2_ref, q3_ref, kv1_ref, kv2_ref, kv3_ref, kp1_ref, kp2_ref, kp3_ref,
                 bias_ref, hm_ref, oa_ref, pv_nat, m_nat, l_nat, pv_pad, m_pad, l_pad):
    lead = (pl.program_id(1) == 0).astype(jnp.int32)

    groups = ((q1_ref, kv1_ref, kp1_ref), (q2_ref, kv2_ref, kp2_ref), (q3_ref, kv3_ref, kp3_ref))
    for g, (q_ref, kv_ref, kvp_ref) in enumerate(groups):
        dil = DILATIONS[g]
        nblk = SUPER // dil // Q_BLOCK
        for r in range(dil):
            kc = slice(r * 2 * HW, r * 2 * HW + HW)
            vc = slice(r * 2 * HW + HW, (r + 1) * 2 * HW)
            qc = slice(r * HW, (r + 1) * HW)
            for i in range(nblk):
                rq = slice(i * Q_BLOCK, (i + 1) * Q_BLOCK)
                if i == 0:
                    k2 = jnp.concatenate([kvp_ref[0, :, kc], kv_ref[0, rq, kc]], axis=0)
                    v2 = jnp.concatenate([kvp_ref[0, :, vc], kv_ref[0, rq, vc]], axis=0)
                else:
                    rk = slice((i - 1) * Q_BLOCK, (i + 1) * Q_BLOCK)
                    k2 = kv_ref[0, rk, kc]
                    v2 = kv_ref[0, rk, vc]
                table = lead if i == 0 else 0
                pv, m, l = _packed_heads_attention(
                    q_ref[0, rq, qc], k2, v2, hm_ref,
                    lambda h, table=table: bias_ref[g, table, h * Q_BLOCK:(h + 1) * Q_BLOCK, :])
                for c in range(HW // LANES):
                    lanes = slice(c * LANES, (c + 1) * LANES)
                    if dil == PADDED_DIL:
                        at = (c, pl.ds(i * Q_BLOCK * PAD_PITCH + r, Q_BLOCK, stride=PAD_PITCH))
                        dst = (pv_pad, m_pad, l_pad)
                    else:
                        nat = pl.ds(i * Q_BLOCK * dil + r, Q_BLOCK, stride=dil) if dil > 1 else rq
                        at = (g, c, nat)
                        dst = (pv_nat, m_nat, l_nat)
                    for ref, val in zip(dst, (pv, m, l)):
                        ref[at + (slice(None),)] = val[:, lanes]

    def padded_rows(ref, c, t):
        runs = COMBINE_ROWS // PADDED_DIL
        return jnp.concatenate([ref[c, pl.ds((t * runs + k) * PAD_PITCH, PADDED_DIL), :]
                                for k in range(runs)], axis=0)

    for t in range(SUPER // COMBINE_ROWS):
        rows = slice(t * COMBINE_ROWS, (t + 1) * COMBINE_ROWS)
        for c in range(HW // LANES):
            m1, m2, m3 = m_nat[0, c, rows, :], m_nat[1, c, rows, :], padded_rows(m_pad, c, t)
            m = jnp.maximum(jnp.maximum(m1, m2), m3)
            e1, e2, e3 = jnp.exp2(m1 - m), jnp.exp2(m2 - m), jnp.exp2(m3 - m)
            num = (e1 * pv_nat[0, c, rows, :] + e2 * pv_nat[1, c, rows, :]
                   + e3 * padded_rows(pv_pad, c, t))
            den = (e1 * l_nat[0, c, rows, :] + e2 * l_nat[1, c, rows, :]
                   + e3 * padded_rows(l_pad, c, t))
            oa_ref[0, rows, c * LANES:(c + 1) * LANES] = (num * (1.0 / den)).astype(BF16)


def _merge_kernel(x_ref, oa_ref, ob_ref, oc_ref, wg_ref, bg_ref, wpa_ref, wpb_ref, wpc_ref,
                  wout_ref, lng_ref, lnb_ref, h_ref):
    x = x_ref[...]
    xb = x.astype(BF16)

    def gated(k, o_ref, wp_ref):
        cols = slice(k * D_MODEL, (k + 1) * D_MODEL)
        t = jnp.tanh(jnp.dot(xb, wg_ref[:, cols], preferred_element_type=F32) + bg_ref[:, cols])
        half_p = jnp.dot(o_ref[...], wp_ref[...], preferred_element_type=F32)
        return half_p + half_p * t

    merged = (gated(0, oa_ref, wpa_ref) + gated(1, ob_ref, wpb_ref)
              + gated(2, oc_ref, wpc_ref)).astype(BF16)
    for c in range(x_ref.shape[0] // TAIL_ROWS):
        rows = slice(c * TAIL_ROWS, (c + 1) * TAIL_ROWS)
        y = jnp.dot(merged[rows], wout_ref[...], preferred_element_type=F32)
        h_ref[rows, :] = _layer_norm(ALPHA * x_ref[rows, :] + y, lng_ref[...], lnb_ref[...])


def _ffn_kernel(h_ref, wup_ref, cw_ref, cb_ref, wdn_ref, lng_ref, lnb_ref, o_ref,
                a_buf, carry, g_buf, *, tiles_per_seq):
    tm = h_ref.shape[0]
    n_chunks = D_FF // FF_CHUNK

    @pl.when(pl.program_id(0) % tiles_per_seq == 0)
    def _():
        carry[...] = jnp.zeros(carry.shape, F32)

    h = h_ref[...]
    hb = h.astype(BF16)
    for c in range(n_chunks):
        cols = slice(c * FF_CHUNK, (c + 1) * FF_CHUNK)
        a = jnp.dot(hb, wup_ref[:, cols], preferred_element_type=F32)
        gate = jnp.dot(hb, wup_ref[:, D_FF + c * FF_CHUNK:D_FF + (c + 1) * FF_CHUNK],
                       preferred_element_type=F32)
        a_buf[:CARRY_ROWS] = carry[c]
        a_buf[CARRY_ROWS:] = a
        carry[c] = a[tm - CARRY_ROWS:]
        conv = (cw_ref[0:1, cols] * a_buf[CARRY_ROWS - 2:CARRY_ROWS - 2 + tm]
                + cw_ref[1:2, cols] * a_buf[CARRY_ROWS - 1:CARRY_ROWS - 1 + tm]
                + cw_ref[2:3, cols] * a + cb_ref[:, cols])
        g_buf[:, cols] = (_gelu(conv) * gate).astype(BF16)
    for c in range(tm // TAIL_ROWS):
        rows = slice(c * TAIL_ROWS, (c + 1) * TAIL_ROWS)
        y = jnp.dot(g_buf[rows, :], wdn_ref[...], preferred_element_type=F32)
        o_ref[rows, :] = _layer_norm(ALPHA * h_ref[rows, :] + y, lng_ref[...], lnb_ref[...])


def _t5_bucket(dist):
    max_exact = N_BUCKETS // 2
    n = jnp.maximum(dist, 1).astype(jnp.float32)
    large = max_exact + (jnp.log(n / max_exact) / math.log(MAX_DISTANCE / max_exact)
                         * (N_BUCKETS - max_exact)).astype(jnp.int32)
    large = jnp.minimum(large, N_BUCKETS - 1)
    return jnp.where(dist < max_exact, dist, large)


def _offset_bias(rel_bias):
    steps = WINDOW_STEPS - jnp.arange(2 * Q_BLOCK)
    rows = []
    for g, dil in enumerate(DILATIONS):
        bucket = _t5_bucket(jnp.maximum(steps, 0) * dil)
        bias = rel_bias[:, g * HEADS:(g + 1) * HEADS][bucket].astype(F32)
        rows.append(jnp.where(steps[:, None] >= 0, bias, NEG).T)
    return jnp.stack(rows, 0)[:, :, None, :]


def _side_cast_plan(rows, max_blocks):
    block = BF16_ROWS
    while rows % block or rows // block > max_blocks:
        block += BF16_ROWS
    return block, rows // block


def _cparams(*sem):
    return pltpu.CompilerParams(dimension_semantics=sem, vmem_limit_bytes=VMEM_LIMIT)


def _layer(x, mem, band_bias, hmask, w_in, w_mem_kv, sg_ln_g, sg_ln_b, w_spatial, b_spatial,
           w_proj_a, w_proj_b, w_proj_c, w_gate, b_gate, w_out, ln1_g, ln1_b,
           w_ffn_up, conv_w, conv_b, w_ffn_down, ln2_g, ln2_b):
    bsz, seq, _ = x.shape
    n_tok = bsz * seq
    tm = TOKEN_TILE
    assert seq % SUPER == 0 and SUPER % tm == 0 and tm % (SG_CHUNK * 2) == 0
    assert D_FF % FF_CHUNK == 0 and tm % (16 * DILATIONS[-1]) == 0
    assert seq % MERGE_TILE == 0 and seq % FFN_TILE == 0
    tiles_per_seq = seq // tm
    row = lambda v: v.reshape(1, -1)
    tok = lambda rows, width: pl.BlockSpec((rows, width), lambda i: (i, 0))
    tile3 = lambda rows, width: pl.BlockSpec((1, rows, width), lambda b, t: (b, t, 0))

    wsp = w_spatial.reshape(SG_W // SG_CHUNK, 2 * SG_CHUNK, SG_CHUNK)
    bsp = jnp.repeat(b_spatial.reshape(SG_W // SG_CHUNK, 2, SG_CHUNK).transpose(0, 2, 1),
                     HEAD_DIM, axis=-1)
    qkv_shapes = ([jax.ShapeDtypeStruct((bsz, seq // d, d * HW), BF16) for d in DILATIONS]
                  + [jax.ShapeDtypeStruct((bsz, seq // d, d * 2 * HW), BF16) for d in DILATIONS])
    qkv_specs = ([tile3(tm // d, d * HW) for d in DILATIONS]
                 + [tile3(tm // d, d * 2 * HW) for d in DILATIONS])
    side_w = ((w_gate, 0.5), (w_proj_a, 0.5), (w_proj_b, 0.5), (w_proj_c, 0.5), (w_out, 1.0),
              (w_ffn_up, 1.0), (w_ffn_down, 1.0))
    assert SIDE_CAST_BLOCKS <= bsz * tiles_per_seq
    plans = [_side_cast_plan(w.shape[0], SIDE_CAST_BLOCKS) for w, _ in side_w]

    def side_spec(w, plan):
        rows, n_blocks = plan
        return pl.BlockSpec((rows, w.shape[1]), lambda b, t: (
            jnp.minimum(b * tiles_per_seq + t, n_blocks - 1), 0))

    side_specs = [side_spec(w, plan) for (w, _), plan in zip(side_w, plans)]
    outs = pl.pallas_call(
        functools.partial(_proj_kernel,
                          side=tuple((scale, plan[1]) for (_, scale), plan in zip(side_w, plans))),
        grid=(bsz, tiles_per_seq),
        in_specs=[tile3(tm, D_MODEL), _resident((D_MODEL, IN_W)),
                  pl.BlockSpec((1, MEM_LEN, D_MODEL), lambda b, t: (b, 0, 0)),
                  _resident((D_MODEL, 2 * HW)), _resident((1, SG_W)), _resident((1, SG_W)),
                  _resident(wsp.shape), _resident(bsp.shape), _resident(hmask.shape)] + side_specs,
        out_specs=qkv_specs + [tile3(tm, SG_W), tile3(tm, HW)] + side_specs,
        out_shape=qkv_shapes + [jax.ShapeDtypeStruct((bsz, seq, SG_W), BF16),
                                jax.ShapeDtypeStruct((bsz, seq, HW), BF16)]
        + [jax.ShapeDtypeStruct(w.shape, BF16) for w, _ in side_w],
        scratch_shapes=[pltpu.VMEM((tm // PADDED_DIL * PAD_PITCH, LANES), F32),
                        pltpu.VMEM((MEM_LEN, 2 * HW), BF16)],
        compiler_params=_cparams("arbitrary", "arbitrary"),
        name="proj",
    )(x, w_in.astype(BF16), mem, w_mem_kv.astype(BF16), row(sg_ln_g), row(sg_ln_b), wsp, bsp,
      hmask, *[w for w, _ in side_w])
    q1, q2, q3, kv1, kv2, kv3, ob, oc = outs[:8]
    wg_b, wpa_b, wpb_b, wpc_b, wout_b, wup_b, wdn_b = outs[8:]

    n_super = seq // SUPER
    super3 = lambda d, width: pl.BlockSpec((1, SUPER // d, d * width), lambda b, j: (b, j, 0))

    def prev3(d):
        nblk = SUPER // d // Q_BLOCK
        return pl.BlockSpec((1, Q_BLOCK, d * 2 * HW),
                            lambda b, j: (b, jnp.maximum(j * nblk - 1, 0), 0))

    oa = pl.pallas_call(
        _attn_kernel,
        grid=(bsz, n_super),
        in_specs=[super3(d, HW) for d in DILATIONS] + [super3(d, 2 * HW) for d in DILATIONS]
        + [prev3(d) for d in DILATIONS] + [_resident(band_bias.shape), _resident(hmask.shape)],
        out_specs=pl.BlockSpec((1, SUPER, HW), lambda b, j: (b, j, 0)),
        out_shape=jax.ShapeDtypeStruct((bsz, seq, HW), BF16),
        scratch_shapes=[pltpu.VMEM((N_GROUPS - 1, HW // LANES, SUPER, LANES), F32)] * 3
        + [pltpu.VMEM((HW // LANES, SUPER // PADDED_DIL * PAD_PITCH, LANES), F32)] * 3,
        compiler_params=_cparams("parallel", "parallel"),
        name="attn",
    )(q1, q2, q3, kv1, kv2, kv3, kv1, kv2, kv3, band_bias, hmask)

    h1 = pl.pallas_call(
        _merge_kernel,
        grid=(n_tok // MERGE_TILE,),
        in_specs=[tok(MERGE_TILE, D_MODEL), tok(MERGE_TILE, HW), tok(MERGE_TILE, SG_W),
                  tok(MERGE_TILE, HW),
                  _resident((D_MODEL, 3 * D_MODEL)), _resident((1, 3 * D_MODEL)),
                  _resident((HW, D_MODEL)), _resident((SG_W, D_MODEL)), _resident((HW, D_MODEL)),
                  _resident((D_MODEL, D_MODEL)), _resident((1, D_MODEL)), _resident((1, D_MODEL))],
        out_specs=tok(MERGE_TILE, D_MODEL),
        out_shape=jax.ShapeDtypeStruct((n_tok, D_MODEL), F32),
        compiler_params=_cparams("parallel"),
        name="merge",
    )(x.reshape(n_tok, D_MODEL), oa.reshape(n_tok, HW), ob.reshape(n_tok, SG_W),
      oc.reshape(n_tok, HW), wg_b, row(0.5 * b_gate), wpa_b, wpb_b, wpc_b, wout_b,
      row(ln1_g), row(ln1_b))

    h2 = pl.pallas_call(
        functools.partial(_ffn_kernel, tiles_per_seq=seq // FFN_TILE),
        grid=(n_tok // FFN_TILE,),
        in_specs=[tok(FFN_TILE, D_MODEL), _resident((D_MODEL, 2 * D_FF)), _resident(conv_w.shape),
                  _resident((1, D_FF)), _resident((D_FF, D_MODEL)),
                  _resident((1, D_MODEL)), _resident((1, D_MODEL))],
        out_specs=tok(FFN_TILE, D_MODEL),
        out_shape=jax.ShapeDtypeStruct((n_tok, D_MODEL), F32),
        scratch_shapes=[pltpu.VMEM((CARRY_ROWS + FFN_TILE, FF_CHUNK), F32),
                        pltpu.VMEM((D_FF // FF_CHUNK, CARRY_ROWS, FF_CHUNK), F32),
                        pltpu.VMEM((FFN_TILE, D_FF), BF16)],
        compiler_params=_cparams("arbitrary"),
        name="ffn",
    )(h1, wup_b, conv_w, row(conv_b), wdn_b, row(ln2_g), row(ln2_b))
    return h2.reshape(bsz, seq, D_MODEL)


def kernel(x, mem, rel_bias, w_in, w_mem_kv, sg_ln_g, sg_ln_b, w_spatial, b_spatial, w_proj_a,
           w_proj_b, w_proj_c, w_gate, b_gate, w_out, ln1_g, ln1_b, w_ffn_up, conv_w, conv_b,
           w_ffn_down, ln2_g, ln2_b):
    band_bias = pl.pallas_call(
        _band_bias_kernel,
        grid=(N_GROUPS,),
        in_specs=[pl.BlockSpec((1, HEADS, 1, 2 * Q_BLOCK), lambda g: (g, 0, 0, 0))],
        out_specs=pl.BlockSpec((1, 2, HEADS * Q_BLOCK, 2 * Q_BLOCK), lambda g: (g, 0, 0, 0)),
        out_shape=jax.ShapeDtypeStruct((N_GROUPS, 2, HEADS * Q_BLOCK, 2 * Q_BLOCK), F32),
        name="band_bias",
    )(_offset_bias(rel_bias))
    lane = jnp.arange(HW)[None, None, :] // HEAD_DIM
    hmask = jnp.broadcast_to(lane == jnp.arange(HEADS)[:, None, None],
                             (HEADS, Q_BLOCK, HW)).astype(BF16)
    h = x
    for l in range(w_in.shape[0]):
        h = _layer(h, mem, band_bias, hmask, w_in[l], w_mem_kv[l], sg_ln_g[l], sg_ln_b[l],
                   w_spatial[l], b_spatial[l], w_proj_a[l], w_proj_b[l], w_proj_c[l], w_gate[l],
                   b_gate[l], w_out[l], ln1_g[l], ln1_b[l], w_ffn_up[l], conv_w[l], conv_b[l],
                   w_ffn_down[l], ln2_g[l], ln2_b[l])
    return h
```

```python
import functools
import math

import jax
import jax.numpy as jnp
from jax import lax
from jax.experimental import pallas as pl
from jax.experimental.pallas import tpu as pltpu

F32 = jnp.float32
BF16 = jnp.bfloat16

LANES = 128
BF16_ROWS = 16
D_MODEL = 1024
HEAD_DIM = 64
DILATIONS = (1, 4, 16)
N_GROUPS = 3
HEADS = 4
HW = HEADS * HEAD_DIM
ATTN_W = N_GROUPS * HW
WINDOW_STEPS = 128
Q_BLOCK = 128
SG_CHUNK = 128
SG_W = 512
MEM_LEN = 256
IN_W = 3 * ATTN_W + 2 * SG_W + HW
D_FF = 2816
N_BUCKETS = 32
MAX_DISTANCE = 2048
LN_EPS = 1e-5
DEPTH = 1
ALPHA = (2 * DEPTH) ** 0.25
ATTN_SCALE = HEAD_DIM ** -0.5
LOG2E = math.log2(math.e)
Q_SCALE = ATTN_SCALE * LOG2E
NEG = -1e30

TOKEN_TILE = 512
MERGE_TILE = 1024
FFN_TILE = 1024
SUPER = Q_BLOCK * DILATIONS[-1]
PADDED_DIL = DILATIONS[-1]
PAD_PITCH = 24
SIDE_CAST_BLOCKS = 8
TAIL_ROWS = 256
COMBINE_ROWS = 256
FF_CHUNK = 256
CARRY_ROWS = 8
VMEM_LIMIT = 56 * 1024 * 1024

_NT = (((1,), (1,)), ((), ()))


def _gelu(x):
    return 0.5 * x * (1.0 + jnp.tanh(math.sqrt(2.0 / math.pi) * (x + 0.044715 * (x * x * x))))


def _layer_norm(x, g, b):
    mu = jnp.mean(x, axis=-1, keepdims=True)
    xc = x - mu
    var = jnp.mean(xc * xc, axis=-1, keepdims=True)
    return xc * lax.rsqrt(var + LN_EPS) * g + b


def _resident(shape):
    zeros = (0,) * len(shape)
    return pl.BlockSpec(shape, lambda *_: zeros, pipeline_mode=pl.Buffered(1))


def _head_dense(per_head):
    low = lax.broadcasted_iota(jnp.int32, (per_head[0].shape[0], LANES), 1) < HEAD_DIM
    return jnp.concatenate([jnp.where(low, per_head[h], per_head[h + 1])
                            for h in range(0, HEADS, 2)], axis=1)


def _packed_heads_attention(q, k, v, hm_ref, bias_fn):
    m_rows = q.shape[0]
    qs = jnp.concatenate([q * hm_ref[h] for h in range(HEADS)], axis=0)
    s = lax.dot_general(qs, k, _NT, preferred_element_type=F32)
    ps, ms, ls = [], [], []
    for h in range(HEADS):
        sh = s[h * m_rows:(h + 1) * m_rows]
        bias = bias_fn(h)
        if bias is not None:
            sh = sh + bias
        m = jnp.max(sh, axis=-1, keepdims=True)
        p = jnp.exp2(sh - m)
        ps.append(p.astype(BF16))
        ms.append(m)
        ls.append(jnp.sum(p, axis=-1, keepdims=True))
    pv = jnp.dot(jnp.concatenate(ps, axis=0), v, preferred_element_type=F32)
    half = lambda h: slice((h // 2) * LANES, (h // 2 + 1) * LANES)
    pv = _head_dense([pv[h * m_rows:(h + 1) * m_rows, half(h)] for h in range(HEADS)])
    return pv, _head_dense(ms), _head_dense(ls)


def _band_bias_kernel(f_ref, o_ref):
    lane = lax.broadcasted_iota(jnp.int32, (Q_BLOCK, 2 * Q_BLOCK), 1)
    no_prev = jnp.where(lane < Q_BLOCK, NEG, 0.0)
    for h in range(HEADS):
        rep = jnp.broadcast_to(f_ref[0, h] * LOG2E, (Q_BLOCK, 2 * Q_BLOCK))
        band = pltpu.roll(rep, 0, 1, stride=1, stride_axis=0)
        o_ref[0, 0, h * Q_BLOCK:(h + 1) * Q_BLOCK] = band
        o_ref[0, 1, h * Q_BLOCK:(h + 1) * Q_BLOCK] = band + no_prev


def _proj_kernel(*refs, side):
    n_side = len(side)
    x_ref, win_ref, mem_ref, wkv_ref, lng_ref, lnb_ref, wsp_ref, bsp_ref, hm_ref = refs[:9]
    side_in = refs[9:9 + n_side]
    q1_ref, q2_ref, q3_ref, kv1_ref, kv2_ref, kv3_ref, ob_ref, oc_ref = refs[9 + n_side:17 + n_side]
    side_out = refs[17 + n_side:17 + 2 * n_side]
    de_buf, kvm = refs[17 + 2 * n_side:]
    xb = x_ref[0].astype(BF16)
    tm = xb.shape[0]

    step = pl.program_id(0) * pl.num_programs(1) + pl.program_id(1)
    for (scale, n_blocks), w_ref, o_ref in zip(side, side_in, side_out):
        @pl.when(step < n_blocks)
        def _(scale=scale, w_ref=w_ref, o_ref=o_ref):
            o_ref[...] = (w_ref[...] * scale).astype(BF16)

    @pl.when(pl.program_id(1) == 0)
    def _():
        kvm[...] = jnp.dot(mem_ref[0].astype(BF16), wkv_ref[...],
                           preferred_element_type=F32).astype(BF16)

    def proj(lo, width):
        return jnp.dot(xb, win_ref[:, lo:lo + width], preferred_element_type=F32)

    def put(ref, lane0, val, dil):
        if dil == 1:
            ref[0, :, lane0:lane0 + HW] = val.astype(BF16)
            return
        stride_cols = ref.shape[2] // dil
        pitch = PAD_PITCH if dil == PADDED_DIL else dil
        for c in range(HW // LANES):
            half = val[:, c * LANES:(c + 1) * LANES]
            if pitch == dil:
                de_buf[:tm] = half
            else:
                for i in range(tm // dil):
                    de_buf[i * pitch:i * pitch + dil] = half[i * dil:(i + 1) * dil]
            for r in range(dil):
                piece = de_buf[pl.ds(r, tm // dil, stride=pitch), :]
                col = r * stride_cols + lane0 + c * LANES
                ref[0, :, col:col + LANES] = piece.astype(BF16)

    def a_proj(g, part):
        q_ref, kv_ref = ((q1_ref, kv1_ref), (q2_ref, kv2_ref), (q3_ref, kv3_ref))[g]
        if part == 0:
            put(q_ref, 0, proj(g * HW, HW) * Q_SCALE, DILATIONS[g])
        else:
            put(kv_ref, (part - 1) * HW, proj(part * ATTN_W + g * HW, HW), DILATIONS[g])

    fillers = iter([(g, part) for g in range(N_GROUPS) for part in range(3)])
    fill = lambda: a_proj(*next(fillers))

    def proj_halves(lo):
        return jnp.concatenate([proj(lo, HW), proj(lo + HW, HW)], axis=1)

    qc = (proj(3 * ATTN_W + 2 * SG_W, HW) * Q_SCALE).astype(BF16)
    u_pre = proj_halves(3 * ATTN_W)
    v_pre = proj_halves(3 * ATTN_W + SG_W)

    n_chunks = tm // SG_CHUNK
    u, vn = [], []
    for c in range(n_chunks):
        rows = slice(c * SG_CHUNK, (c + 1) * SG_CHUNK)
        fill()
        u.append(_gelu(u_pre[rows]))
        vn.append(_layer_norm(_gelu(v_pre[rows]), lng_ref[...], lnb_ref[...]).astype(BF16))
        pv, _, l = _packed_heads_attention(qc[rows], kvm[:, :HW], kvm[:, HW:], hm_ref,
                                           lambda h: None)
        oc_ref[0, rows, :] = (pv * (1.0 / l)).astype(BF16)
        fill()

    row = lax.broadcasted_iota(jnp.int32, (2 * SG_CHUNK, SG_CHUNK), 0) & (SG_CHUNK - 1)
    col = lax.broadcasted_iota(jnp.int32, (2 * SG_CHUNK, SG_CHUNK), 1)
    low_half = lax.broadcasted_iota(jnp.int32, (SG_CHUNK, SG_CHUNK), 1) < HEAD_DIM
    for j in range(SG_W // SG_CHUNK):
        cols = slice(j * SG_CHUNK, (j + 1) * SG_CHUNK)
        w_pair = jnp.where(row >= col, wsp_ref[j], 0.0).astype(BF16)
        rhs = jnp.concatenate([vn[c][:, cols] for c in range(n_chunks)], axis=1)
        r = jnp.dot(w_pair, rhs, preferred_element_type=F32)
        for c in range(n_chunks):
            rows = slice(c * SG_CHUNK, (c + 1) * SG_CHUNK)
            sv = jnp.where(low_half, r[:SG_CHUNK, rows], r[SG_CHUNK:, rows]) + bsp_ref[j]
            ob_ref[0, rows, cols] = (u[c][:, cols] * sv).astype(BF16)
        if j == 0:
            fill()


def _attn_kernel(q1_ref, q2_ref, q3_ref, kv1_ref, kv2_ref, kv3_ref, kp1_ref, kp2_ref, kp3_ref,
                 bias_ref, hm_ref, oa_ref, pv_nat, m_nat, l_nat, pv_pad, m_pad, l_pad):
    lead = (pl.program_id(1) == 0).astype(jnp.int32)

    groups = ((q1_ref, kv1_ref, kp1_ref), (q2_ref, kv2_ref, kp2_ref), (q3_ref, kv3_ref, kp3_ref))

    def unit(g, r, i):
        q_ref, kv_ref, kvp_ref = groups[g]
        dil = DILATIONS[g]
        kc = slice(r * 2 * HW, r * 2 * HW + HW)
        vc = slice(r * 2 * HW + HW, (r + 1) * 2 * HW)
        rq = slice(i * Q_BLOCK, (i + 1) * Q_BLOCK)
        if i == 0:
            k2 = jnp.concatenate([kvp_ref[0, :, kc], kv_ref[0, rq, kc]], axis=0)
            v2 = jnp.concatenate([kvp_ref[0, :, vc], kv_ref[0, rq, vc]], axis=0)
        else:
            rk = slice((i - 1) * Q_BLOCK, (i + 1) * Q_BLOCK)
            k2 = kv_ref[0, rk, kc]
            v2 = kv_ref[0, rk, vc]
        table = lead if i == 0 else 0
        pv, m, l = _packed_heads_attention(
            q_ref[0, rq, r * HW:(r + 1) * HW], k2, v2, hm_ref,
            lambda h: bias_ref[g, table, h * Q_BLOCK:(h + 1) * Q_BLOCK, :])
        for c in range(HW // LANES):
            lanes = slice(c * LANES, (c + 1) * LANES)
            if dil == PADDED_DIL:
                at = (c, pl.ds(i * Q_BLOCK * PAD_PITCH + r, Q_BLOCK, stride=PAD_PITCH))
                dst = (pv_pad, m_pad, l_pad)
            else:
                nat = pl.ds(i * Q_BLOCK * dil + r, Q_BLOCK, stride=dil) if dil > 1 else rq
                at = (g, c, nat)
                dst = (pv_nat, m_nat, l_nat)
            for ref, val in zip(dst, (pv, m, l)):
                ref[at + (slice(None),)] = val[:, lanes]

    per_group = [[(g, r, i) for r in range(DILATIONS[g])
                  for i in range(SUPER // DILATIONS[g] // Q_BLOCK)] for g in range(N_GROUPS)]
    for turn in zip(*per_group, strict=True):
        for u in turn:
            unit(*u)

    def padded_rows(ref, c, t):
        runs = COMBINE_ROWS // PADDED_DIL
        return jnp.concatenate([ref[c, pl.ds((t * runs + k) * PAD_PITCH, PADDED_DIL), :]
                                for k in range(runs)], axis=0)

    for t in range(SUPER // COMBINE_ROWS):
        rows = slice(t * COMBINE_ROWS, (t + 1) * COMBINE_ROWS)
        for c in range(HW // LANES):
            m1, m2, m3 = m_nat[0, c, rows, :], m_nat[1, c, rows, :], padded_rows(m_pad, c, t)
            m = jnp.maximum(jnp.maximum(m1, m2), m3)
            e1, e2, e3 = jnp.exp2(m1 - m), jnp.exp2(m2 - m), jnp.exp2(m3 - m)
            num = (e1 * pv_nat[0, c, rows, :] + e2 * pv_nat[1, c, rows, :]
                   + e3 * padded_rows(pv_pad, c, t))
            den = (e1 * l_nat[0, c, rows, :] + e2 * l_nat[1, c, rows, :]
                   + e3 * padded_rows(l_pad, c, t))
            oa_ref[0, rows, c * LANES:(c + 1) * LANES] = (num * (1.0 / den)).astype(BF16)


def _merge_kernel(x_ref, oa_ref, ob_ref, oc_ref, wg_ref, bg_ref, wpa_ref, wpb_ref, wpc_ref,
                  wout_ref, lng_ref, lnb_ref, h_ref):
    x = x_ref[...]
    xb = x.astype(BF16)

    def gated(k, o_ref, wp_ref):
        cols = slice(k * D_MODEL, (k + 1) * D_MODEL)
        t = jnp.tanh(jnp.dot(xb, wg_ref[:, cols], preferred_element_type=F32) + bg_ref[:, cols])
        half_p = jnp.dot(o_ref[...], wp_ref[...], preferred_element_type=F32)
        return half_p + half_p * t

    merged = (gated(0, oa_ref, wpa_ref) + gated(1, ob_ref, wpb_ref)
              + gated(2, oc_ref, wpc_ref)).astype(BF16)
    for c in range(x_ref.shape[0] // TAIL_ROWS):
        rows = slice(c * TAIL_ROWS, (c + 1) * TAIL_ROWS)
        y = jnp.dot(merged[rows], wout_ref[...], preferred_element_type=F32)
        h_ref[rows, :] = _layer_norm(ALPHA * x_ref[rows, :] + y, lng_ref[...], lnb_ref[...])


def _ffn_kernel(h_ref, wup_ref, cw_ref, cb_ref, wdn_ref, lng_ref, lnb_ref, o_ref,
                a_buf, carry, g_buf, *, tiles_per_seq):
    tm = h_ref.shape[0]
    n_chunks = D_FF // FF_CHUNK

    @pl.when(pl.program_id(0) % tiles_per_seq == 0)
    def _():
        carry[...] = jnp.zeros(carry.shape, F32)

    h = h_ref[...]
    hb = h.astype(BF16)
    for c in range(n_chunks):
        cols = slice(c * FF_CHUNK, (c + 1) * FF_CHUNK)
        a = jnp.dot(hb, wup_ref[:, cols], preferred_element_type=F32)
        gate = jnp.dot(hb, wup_ref[:, D_FF + c * FF_CHUNK:D_FF + (c + 1) * FF_CHUNK],
                       preferred_element_type=F32)
        a_buf[:CARRY_ROWS] = carry[c]
        a_buf[CARRY_ROWS:] = a
        carry[c] = a[tm - CARRY_ROWS:]
        conv = (cw_ref[0:1, cols] * a_buf[CARRY_ROWS - 2:CARRY_ROWS - 2 + tm]
                + cw_ref[1:2, cols] * a_buf[CARRY_ROWS - 1:CARRY_ROWS - 1 + tm]
                + cw_ref[2:3, cols] * a + cb_ref[:, cols])
        g_buf[:, cols] = (_gelu(conv) * gate).astype(BF16)
    for c in range(tm // TAIL_ROWS):
        rows = slice(c * TAIL_ROWS, (c + 1) * TAIL_ROWS)
        y = jnp.dot(g_buf[rows, :], wdn_ref[...], preferred_element_type=F32)
        o_ref[rows, :] = _layer_norm(ALPHA * h_ref[rows, :] + y, lng_ref[...], lnb_ref[...])


def _t5_bucket(dist):
    max_exact = N_BUCKETS // 2
    n = jnp.maximum(dist, 1).astype(jnp.float32)
    large = max_exact + (jnp.log(n / max_exact) / math.log(MAX_DISTANCE / max_exact)
                         * (N_BUCKETS - max_exact)).astype(jnp.int32)
    large = jnp.minimum(large, N_BUCKETS - 1)
    return jnp.where(dist < max_exact, dist, large)


def _offset_bias(rel_bias):
    steps = WINDOW_STEPS - jnp.arange(2 * Q_BLOCK)
    rows = []
    for g, dil in enumerate(DILATIONS):
        bucket = _t5_bucket(jnp.maximum(steps, 0) * dil)
        bias = rel_bias[:, g * HEADS:(g + 1) * HEADS][bucket].astype(F32)
        rows.append(jnp.where(steps[:, None] >= 0, bias, NEG).T)
    return jnp.stack(rows, 0)[:, :, None, :]


def _side_cast_plan(rows, max_blocks):
    block = BF16_ROWS
    while rows % block or rows // block > max_blocks:
        block += BF16_ROWS
    return block, rows // block


def _cparams(*sem):
    return pltpu.CompilerParams(dimension_semantics=sem, vmem_limit_bytes=VMEM_LIMIT)


def _layer(x, mem, band_bias, hmask, w_in, w_mem_kv, sg_ln_g, sg_ln_b, w_spatial, b_spatial,
           w_proj_a, w_proj_b, w_proj_c, w_gate, b_gate, w_out, ln1_g, ln1_b,
           w_ffn_up, conv_w, conv_b, w_ffn_down, ln2_g, ln2_b):
    bsz, seq, _ = x.shape
    n_tok = bsz * seq
    tm = TOKEN_TILE
    assert seq % SUPER == 0 and SUPER % tm == 0 and tm % (SG_CHUNK * 2) == 0
    assert D_FF % FF_CHUNK == 0 and tm % (16 * DILATIONS[-1]) == 0
    assert seq % MERGE_TILE == 0 and seq % FFN_TILE == 0
    tiles_per_seq = seq // tm
    row = lambda v: v.reshape(1, -1)
    tok = lambda rows, width: pl.BlockSpec((rows, width), lambda i: (i, 0))
    tile3 = lambda rows, width: pl.BlockSpec((1, rows, width), lambda b, t: (b, t, 0))

    wsp = w_spatial.reshape(SG_W // SG_CHUNK, 2 * SG_CHUNK, SG_CHUNK)
    bsp = jnp.repeat(b_spatial.reshape(SG_W // SG_CHUNK, 2, SG_CHUNK).transpose(0, 2, 1),
                     HEAD_DIM, axis=-1)
    qkv_shapes = ([jax.ShapeDtypeStruct((bsz, seq // d, d * HW), BF16) for d in DILATIONS]
                  + [jax.ShapeDtypeStruct((bsz, seq // d, d * 2 * HW), BF16) for d in DILATIONS])
    qkv_specs = ([tile3(tm // d, d * HW) for d in DILATIONS]
                 + [tile3(tm // d, d * 2 * HW) for d in DILATIONS])
    side_w = ((w_gate, 0.5), (w_proj_a, 0.5), (w_proj_b, 0.5), (w_proj_c, 0.5), (w_out, 1.0),
              (w_ffn_up, 1.0), (w_ffn_down, 1.0))
    assert SIDE_CAST_BLOCKS <= bsz * tiles_per_seq
    plans = [_side_cast_plan(w.shape[0], SIDE_CAST_BLOCKS) for w, _ in side_w]

    def side_spec(w, plan):
        rows, n_blocks = plan
        return pl.BlockSpec((rows, w.shape[1]), lambda b, t: (
            jnp.minimum(b * tiles_per_seq + t, n_blocks - 1), 0))

    side_specs = [side_spec(w, plan) for (w, _), plan in zip(side_w, plans)]
    outs = pl.pallas_call(
        functools.partial(_proj_kernel,
                          side=tuple((scale, plan[1]) for (_, scale), plan in zip(side_w, plans))),
        grid=(bsz, tiles_per_seq),
        in_specs=[tile3(tm, D_MODEL), _resident((D_MODEL, IN_W)),
                  pl.BlockSpec((1, MEM_LEN, D_MODEL), lambda b, t: (b, 0, 0)),
                  _resident((D_MODEL, 2 * HW)), _resident((1, SG_W)), _resident((1, SG_W)),
                  _resident(wsp.shape), _resident(bsp.shape), _resident(hmask.shape)] + side_specs,
        out_specs=qkv_specs + [tile3(tm, SG_W), tile3(tm, HW)] + side_specs,
        out_shape=qkv_shapes + [jax.ShapeDtypeStruct((bsz, seq, SG_W), BF16),
                                jax.ShapeDtypeStruct((bsz, seq, HW), BF16)]
        + [jax.ShapeDtypeStruct(w.shape, BF16) for w, _ in side_w],
        scratch_shapes=[pltpu.VMEM((tm // PADDED_DIL * PAD_PITCH, LANES), F32),
                        pltpu.VMEM((MEM_LEN, 2 * HW), BF16)],
        compiler_params=_cparams("arbitrary", "arbitrary"),
        name="proj",
    )(x, w_in.astype(BF16), mem, w_mem_kv.astype(BF16), row(sg_ln_g), row(sg_ln_b), wsp, bsp,
      hmask, *[w for w, _ in side_w])
    q1, q2, q3, kv1, kv2, kv3, ob, oc = outs[:8]
    wg_b, wpa_b, wpb_b, wpc_b, wout_b, wup_b, wdn_b = outs[8:]

    n_super = seq // SUPER
    super3 = lambda d, width: pl.BlockSpec((1, SUPER // d, d * width), lambda b, j: (b, j, 0))

    def prev3(d):
        nblk = SUPER // d // Q_BLOCK
        return pl.BlockSpec((1, Q_BLOCK, d * 2 * HW),
                            lambda b, j: (b, jnp.maximum(j * nblk - 1, 0), 0))

    oa = pl.pallas_call(
        _attn_kernel,
        grid=(bsz, n_super),
        in_specs=[super3(d, HW) for d in DILATIONS] + [super3(d, 2 * HW) for d in DILATIONS]
        + [prev3(d) for d in DILATIONS] + [_resident(band_bias.shape), _resident(hmask.shape)],
        out_specs=pl.BlockSpec((1, SUPER, HW), lambda b, j: (b, j, 0)),
        out_shape=jax.ShapeDtypeStruct((bsz, seq, HW), BF16),
        scratch_shapes=[pltpu.VMEM((N_GROUPS - 1, HW // LANES, SUPER, LANES), F32)] * 3
        + [pltpu.VMEM((HW // LANES, SUPER // PADDED_DIL * PAD_PITCH, LANES), F32)] * 3,
        compiler_params=_cparams("parallel", "parallel"),
        name="attn",
    )(q1, q2, q3, kv1, kv2, kv3, kv1, kv2, kv3, band_bias, hmask)

    h1 = pl.pallas_call(
        _merge_kernel,
        grid=(n_tok // MERGE_TILE,),
        in_specs=[tok(MERGE_TILE, D_MODEL), tok(MERGE_TILE, HW), tok(MERGE_TILE, SG_W),
                  tok(MERGE_TILE, HW),
                  _resident((D_MODEL, 3 * D_MODEL)), _resident((1, 3 * D_MODEL)),
                  _resident((HW, D_MODEL)), _resident((SG_W, D_MODEL)), _resident((HW, D_MODEL)),
                  _resident((D_MODEL, D_MODEL)), _resident((1, D_MODEL)), _resident((1, D_MODEL))],
        out_specs=tok(MERGE_TILE, D_MODEL),
        out_shape=jax.ShapeDtypeStruct((n_tok, D_MODEL), F32),
        compiler_params=_cparams("parallel"),
        name="merge",
    )(x.reshape(n_tok, D_MODEL), oa.reshape(n_tok, HW), ob.reshape(n_tok, SG_W),
      oc.reshape(n_tok, HW), wg_b, row(0.5 * b_gate), wpa_b, wpb_b, wpc_b, wout_b,
      row(ln1_g), row(ln1_b))

    h2 = pl.pallas_call(
        functools.partial(_ffn_kernel, tiles_per_seq=seq // FFN_TILE),
        grid=(n_tok // FFN_TILE,),
        in_specs=[tok(FFN_TILE, D_MODEL), _resident((D_MODEL, 2 * D_FF)), _resident(conv_w.shape),
                  _resident((1, D_FF)), _resident((D_FF, D_MODEL)),
                  _resident((1, D_MODEL)), _resident((1, D_MODEL))],
        out_specs=tok(FFN_TILE, D_MODEL),
        out_shape=jax.ShapeDtypeStruct((n_tok, D_MODEL), F32),
        scratch_shapes=[pltpu.VMEM((CARRY_ROWS + FFN_TILE, FF_CHUNK), F32),
                        pltpu.VMEM((D_FF // FF_CHUNK, CARRY_ROWS, FF_CHUNK), F32),
                        pltpu.VMEM((FFN_TILE, D_FF), BF16)],
        compiler_params=_cparams("arbitrary"),
        name="ffn",
    )(h1, wup_b, conv_w, row(conv_b), wdn_b, row(ln2_g), row(ln2_b))
    return h2.reshape(bsz, seq, D_MODEL)


def kernel(x, mem, rel_bias, w_in, w_mem_kv, sg_ln_g, sg_ln_b, w_spatial, b_spatial, w_proj_a,
           w_proj_b, w_proj_c, w_gate, b_gate, w_out, ln1_g, ln1_b, w_ffn_up, conv_w, conv_b,
           w_ffn_down, ln2_g, ln2_b):
    band_bias = pl.pallas_call(
        _band_bias_kernel,
        grid=(N_GROUPS,),
        in_specs=[pl.BlockSpec((1, HEADS, 1, 2 * Q_BLOCK), lambda g: (g, 0, 0, 0))],
        out_specs=pl.BlockSpec((1, 2, HEADS * Q_BLOCK, 2 * Q_BLOCK), lambda g: (g, 0, 0, 0)),
        out_shape=jax.ShapeDtypeStruct((N_GROUPS, 2, HEADS * Q_BLOCK, 2 * Q_BLOCK), F32),
        name="band_bias",
    )(_offset_bias(rel_bias))
    lane = jnp.arange(HW)[None, None, :] // HEAD_DIM
    hmask = jnp.broadcast_to(lane == jnp.arange(HEADS)[:, None, None],
                             (HEADS, Q_BLOCK, HW)).astype(BF16)
    h = x
    for l in range(w_in.shape[0]):
        h = _layer(h, mem, band_bias, hmask, w_in[l], w_mem_kv[l], sg_ln_g[l], sg_ln_b[l],
                   w_spatial[l], b_spatial[l], w_proj_a[l], w_proj_b[l], w_proj_c[l], w_gate[l],
                   b_gate[l], w_out[l], ln1_g[l], ln1_b[l], w_ffn_up[l], conv_w[l], conv_b[l],
                   w_ffn_down[l], ln2_g[l], ln2_b[l])
    return h
```

```python
import functools
import math

import jax
import jax.numpy as jnp
from jax import lax
from jax.experimental import pallas as pl
from jax.experimental.pallas import tpu as pltpu

F32 = jnp.float32
BF16 = jnp.bfloat16

LANES = 128
BF16_ROWS = 16
D_MODEL = 1024
HEAD_DIM = 64
DILATIONS = (1, 4, 16)
N_GROUPS = 3
HEADS = 4
HW = HEADS * HEAD_DIM
ATTN_W = N_GROUPS * HW
WINDOW_STEPS = 128
Q_BLOCK = 128
SG_CHUNK = 128
SG_W = 512
MEM_LEN = 256
IN_W = 3 * ATTN_W + 2 * SG_W + HW
D_FF = 2816
N_BUCKETS = 32
MAX_DISTANCE = 2048
LN_EPS = 1e-5
DEPTH = 1
ALPHA = (2 * DEPTH) ** 0.25
ATTN_SCALE = HEAD_DIM ** -0.5
LOG2E = math.log2(math.e)
Q_SCALE = ATTN_SCALE * LOG2E
NEG = -1e30

TOKEN_TILE = 512
MERGE_TILE = 1024
FFN_TILE = 1024
SUPER = Q_BLOCK * DILATIONS[-1]
PADDED_DIL = DILATIONS[-1]
PAD_PITCH = 24
SIDE_CAST_BLOCKS = 8
TAIL_ROWS = 256
COMBINE_ROWS = 512
FF_CHUNK = 256
CARRY_ROWS = 8
VMEM_LIMIT = 56 * 1024 * 1024

_NT = (((1,), (1,)), ((), ()))


def _gelu(x):
    return 0.5 * x * (1.0 + jnp.tanh(math.sqrt(2.0 / math.pi) * (x + 0.044715 * (x * x * x))))


_GELU_C0 = math.sqrt(2.0 / math.pi)
_GELU_C1 = _GELU_C0 * 0.044715


def _gelu_short(x):
    half_x = 0.5 * x
    return half_x + half_x * jnp.tanh(x * (_GELU_C0 + _GELU_C1 * (x * x)))


def _layer_norm(x, g, b):
    mu = jnp.mean(x, axis=-1, keepdims=True)
    xc = x - mu
    var = jnp.mean(xc * xc, axis=-1, keepdims=True)
    return xc * lax.rsqrt(var + LN_EPS) * g + b


def _resident(shape):
    zeros = (0,) * len(shape)
    return pl.BlockSpec(shape, lambda *_: zeros, pipeline_mode=pl.Buffered(1))


def _head_dense(per_head):
    low = lax.broadcasted_iota(jnp.int32, (per_head[0].shape[0], LANES), 1) < HEAD_DIM
    return jnp.concatenate([jnp.where(low, per_head[h], per_head[h + 1])
                            for h in range(0, HEADS, 2)], axis=1)


def _packed_heads_attention(q, k, v, hm_ref, bias_fn):
    m_rows = q.shape[0]
    qs = jnp.concatenate([q * hm_ref[h] for h in range(HEADS)], axis=0)
    s = lax.dot_general(qs, k, _NT, preferred_element_type=F32)
    ps, ms, ls = [], [], []
    for h in range(HEADS):
        sh = s[h * m_rows:(h + 1) * m_rows]
        bias = bias_fn(h)
        if bias is not None:
            sh = sh + bias
        m = jnp.max(sh, axis=-1, keepdims=True)
        p = jnp.exp2(sh - m)
        ps.append(p.astype(BF16))
        ms.append(m)
        ls.append(jnp.sum(p, axis=-1, keepdims=True))
    pv = jnp.dot(jnp.concatenate(ps, axis=0), v, preferred_element_type=F32)
    half = lambda h: slice((h // 2) * LANES, (h // 2 + 1) * LANES)
    pv = _head_dense([pv[h * m_rows:(h + 1) * m_rows, half(h)] for h in range(HEADS)])
    return pv, _head_dense(ms), _head_dense(ls)


def _band_bias_kernel(f_ref, o_ref):
    lane = lax.broadcasted_iota(jnp.int32, (Q_BLOCK, 2 * Q_BLOCK), 1)
    no_prev = jnp.where(lane < Q_BLOCK, NEG, 0.0)
    for h in range(HEADS):
        rep = jnp.broadcast_to(f_ref[0, h] * LOG2E, (Q_BLOCK, 2 * Q_BLOCK))
        band = pltpu.roll(rep, 0, 1, stride=1, stride_axis=0)
        o_ref[0, 0, h * Q_BLOCK:(h + 1) * Q_BLOCK] = band
        o_ref[0, 1, h * Q_BLOCK:(h + 1) * Q_BLOCK] = band + no_prev


def _proj_kernel(*refs, side):
    n_side = len(side)
    x_ref, win_ref, mem_ref, wkv_ref, lng_ref, lnb_ref, wsp_ref, bsp_ref, hm_ref = refs[:9]
    side_in = refs[9:9 + n_side]
    q1_ref, q2_ref, q3_ref, kv1_ref, kv2_ref, kv3_ref, ob_ref, oc_ref = refs[9 + n_side:17 + n_side]
    side_out = refs[17 + n_side:17 + 2 * n_side]
    de_buf, kvm = refs[17 + 2 * n_side:]
    xb = x_ref[0].astype(BF16)
    tm = xb.shape[0]

    step = pl.program_id(0) * pl.num_programs(1) + pl.program_id(1)
    for (scale, n_blocks), w_ref, o_ref in zip(side, side_in, side_out):
        @pl.when(step < n_blocks)
        def _(scale=scale, w_ref=w_ref, o_ref=o_ref):
            o_ref[...] = (w_ref[...] * scale).astype(BF16)

    @pl.when(pl.program_id(1) == 0)
    def _():
        kvm[...] = jnp.dot(mem_ref[0].astype(BF16), wkv_ref[...],
                           preferred_element_type=F32).astype(BF16)

    def proj(lo, width):
        return jnp.dot(xb, win_ref[:, lo:lo + width], preferred_element_type=F32)

    def put(ref, lane0, val, dil):
        if dil == 1:
            ref[0, :, lane0:lane0 + HW] = val.astype(BF16)
            return
        stride_cols = ref.shape[2] // dil
        pitch = PAD_PITCH if dil == PADDED_DIL else dil
        for c in range(HW // LANES):
            half = val[:, c * LANES:(c + 1) * LANES]
            if pitch == dil:
                de_buf[:tm] = half
            else:
                for i in range(tm // dil):
                    de_buf[i * pitch:i * pitch + dil] = half[i * dil:(i + 1) * dil]
            for r in range(dil):
                piece = de_buf[pl.ds(r, tm // dil, stride=pitch), :]
                col = r * stride_cols + lane0 + c * LANES
                ref[0, :, col:col + LANES] = piece.astype(BF16)

    def a_proj(g, part):
        q_ref, kv_ref = ((q1_ref, kv1_ref), (q2_ref, kv2_ref), (q3_ref, kv3_ref))[g]
        if part == 0:
            put(q_ref, 0, proj(g * HW, HW) * Q_SCALE, DILATIONS[g])
        else:
            put(kv_ref, (part - 1) * HW, proj(part * ATTN_W + g * HW, HW), DILATIONS[g])

    fillers = iter([(g, part) for g in range(N_GROUPS) for part in range(3)])
    fill = lambda: a_proj(*next(fillers))

    def proj_halves(lo):
        return jnp.concatenate([proj(lo, HW), proj(lo + HW, HW)], axis=1)

    qc = (proj(3 * ATTN_W + 2 * SG_W, HW) * Q_SCALE).astype(BF16)
    u_pre = proj_halves(3 * ATTN_W)
    v_pre = proj_halves(3 * ATTN_W + SG_W)

    n_chunks = tm // SG_CHUNK
    u, vn = [], []
    for c in range(n_chunks):
        rows = slice(c * SG_CHUNK, (c + 1) * SG_CHUNK)
        fill()
        u.append(_gelu_short(u_pre[rows]))
        vn.append(_layer_norm(_gelu_short(v_pre[rows]), lng_ref[...], lnb_ref[...]).astype(BF16))
        pv, _, l = _packed_heads_attention(qc[rows], kvm[:, :HW], kvm[:, HW:], hm_ref,
                                           lambda h: None)
        oc_ref[0, rows, :] = (pv * (1.0 / l)).astype(BF16)
        fill()

    row = lax.broadcasted_iota(jnp.int32, (2 * SG_CHUNK, SG_CHUNK), 0) & (SG_CHUNK - 1)
    col = lax.broadcasted_iota(jnp.int32, (2 * SG_CHUNK, SG_CHUNK), 1)
    low_half = lax.broadcasted_iota(jnp.int32, (SG_CHUNK, SG_CHUNK), 1) < HEAD_DIM
    for j in range(SG_W // SG_CHUNK):
        cols = slice(j * SG_CHUNK, (j + 1) * SG_CHUNK)
        w_pair = jnp.where(row >= col, wsp_ref[j], 0.0).astype(BF16)
        rhs = jnp.concatenate([vn[c][:, cols] for c in range(n_chunks)], axis=1)
        r = jnp.dot(w_pair, rhs, preferred_element_type=F32)
        for c in range(n_chunks):
            rows = slice(c * SG_CHUNK, (c + 1) * SG_CHUNK)
            sv = jnp.where(low_half, r[:SG_CHUNK, rows], r[SG_CHUNK:, rows]) + bsp_ref[j]
            ob_ref[0, rows, cols] = (u[c][:, cols] * sv).astype(BF16)
        if j == 0:
            fill()


def _attn_kernel(q1_ref, q2_ref, q3_ref, kv1_ref, kv2_ref, kv3_ref, kp1_ref, kp2_ref, kp3_ref,
                 bias_ref, hm_ref, oa_ref, pv_nat, m_nat, l_nat, pv_pad, m_pad, l_pad):
    lead = (pl.program_id(1) == 0).astype(jnp.int32)

    groups = ((q1_ref, kv1_ref, kp1_ref), (q2_ref, kv2_ref, kp2_ref), (q3_ref, kv3_ref, kp3_ref))
    for g, (q_ref, kv_ref, kvp_ref) in enumerate(groups):
        dil = DILATIONS[g]
        nblk = SUPER // dil // Q_BLOCK
        for r in range(dil):
            kc = slice(r * 2 * HW, r * 2 * HW + HW)
            vc = slice(r * 2 * HW + HW, (r + 1) * 2 * HW)
            qc = slice(r * HW, (r + 1) * HW)
            for i in range(nblk):
                rq = slice(i * Q_BLOCK, (i + 1) * Q_BLOCK)
                if i == 0:
                    k2 = jnp.concatenate([kvp_ref[0, :, kc], kv_ref[0, rq, kc]], axis=0)
                    v2 = jnp.concatenate([kvp_ref[0, :, vc], kv_ref[0, rq, vc]], axis=0)
                else:
                    rk = slice((i - 1) * Q_BLOCK, (i + 1) * Q_BLOCK)
                    k2 = kv_ref[0, rk, kc]
                    v2 = kv_ref[0, rk, vc]
                table = lead if i == 0 else 0
                pv, m, l = _packed_heads_attention(
                    q_ref[0, rq, qc], k2, v2, hm_ref,
                    lambda h, table=table: bias_ref[g, table, h * Q_BLOCK:(h + 1) * Q_BLOCK, :])
                for c in range(HW // LANES):
                    lanes = slice(c * LANES, (c + 1) * LANES)
                    if dil == PADDED_DIL:
                        at = (c, pl.ds(i * Q_BLOCK * PAD_PITCH + r, Q_BLOCK, stride=PAD_PITCH))
                        dst = (pv_pad, m_pad, l_pad)
                    else:
                        nat = pl.ds(i * Q_BLOCK * dil + r, Q_BLOCK, stride=dil) if dil > 1 else rq
                        at = (g, c, nat)
                        dst = (pv_nat, m_nat, l_nat)
                    for ref, val in zip(dst, (pv, m, l)):
                        ref[at + (slice(None),)] = val[:, lanes]

    def padded_rows(ref, c, t):
        runs = COMBINE_ROWS // PADDED_DIL
        return jnp.concatenate([ref[c, pl.ds((t * runs + k) * PAD_PITCH, PADDED_DIL), :]
                                for k in range(runs)], axis=0)

    for t in range(SUPER // COMBINE_ROWS):
        rows = slice(t * COMBINE_ROWS, (t + 1) * COMBINE_ROWS)
        for c in range(HW // LANES):
            m1, m2, m3 = m_nat[0, c, rows, :], m_nat[1, c, rows, :], padded_rows(m_pad, c, t)
            m = jnp.maximum(jnp.maximum(m1, m2), m3)
            e1, e2, e3 = jnp.exp2(m1 - m), jnp.exp2(m2 - m), jnp.exp2(m3 - m)
            num = (e1 * pv_nat[0, c, rows, :] + e2 * pv_nat[1, c, rows, :]
                   + e3 * padded_rows(pv_pad, c, t))
            den = (e1 * l_nat[0, c, rows, :] + e2 * l_nat[1, c, rows, :]
                   + e3 * padded_rows(l_pad, c, t))
            oa_ref[0, rows, c * LANES:(c + 1) * LANES] = (num * (1.0 / den)).astype(BF16)


def _merge_kernel(x_ref, oa_ref, ob_ref, oc_ref, wg_ref, bg_ref, wpa_ref, wpb_ref, wpc_ref,
                  wout_ref, lng_ref, lnb_ref, h_ref):
    x = x_ref[...]
    xb = x.astype(BF16)

    def gated(k, o_ref, wp_ref):
        cols = slice(k * D_MODEL, (k + 1) * D_MODEL)
        t = jnp.tanh(jnp.dot(xb, wg_ref[:, cols], preferred_element_type=F32) + bg_ref[:, cols])
        half_p = jnp.dot(o_ref[...], wp_ref[...], preferred_element_type=F32)
        return half_p + half_p * t

    merged = (gated(0, oa_ref, wpa_ref) + gated(1, ob_ref, wpb_ref)
              + gated(2, oc_ref, wpc_ref)).astype(BF16)
    for c in range(x_ref.shape[0] // TAIL_ROWS):
        rows = slice(c * TAIL_ROWS, (c + 1) * TAIL_ROWS)
        y = jnp.dot(merged[rows], wout_ref[...], preferred_element_type=F32)
        h_ref[rows, :] = _layer_norm(ALPHA * x_ref[rows, :] + y, lng_ref[...], lnb_ref[...])


def _ffn_kernel(h_ref, wup_ref, cw_ref, cb_ref, wdn_ref, lng_ref, lnb_ref, o_ref,
                a_buf, carry, g_buf, *, tiles_per_seq):
    tm = h_ref.shape[0]
    n_chunks = D_FF // FF_CHUNK

    @pl.when(pl.program_id(0) % tiles_per_seq == 0)
    def _():
        carry[...] = jnp.zeros(carry.shape, F32)

    h = h_ref[...]
    hb = h.astype(BF16)
    for c in range(n_chunks):
        cols = slice(c * FF_CHUNK, (c + 1) * FF_CHUNK)
        a = jnp.dot(hb, wup_ref[:, cols], preferred_element_type=F32)
        gate = jnp.dot(hb, wup_ref[:, D_FF + c * FF_CHUNK:D_FF + (c + 1) * FF_CHUNK],
                       preferred_element_type=F32)
        a_buf[:CARRY_ROWS] = carry[c]
        a_buf[CARRY_ROWS:] = a
        carry[c] = a[tm - CARRY_ROWS:]
        conv = (cw_ref[0:1, cols] * a_buf[CARRY_ROWS - 2:CARRY_ROWS - 2 + tm]
                + cw_ref[1:2, cols] * a_buf[CARRY_ROWS - 1:CARRY_ROWS - 1 + tm]
                + cw_ref[2:3, cols] * a + cb_ref[:, cols])
        g_buf[:, cols] = (_gelu(conv) * gate).astype(BF16)
    for c in range(tm // TAIL_ROWS):
        rows = slice(c * TAIL_ROWS, (c + 1) * TAIL_ROWS)
        y = jnp.dot(g_buf[rows, :], wdn_ref[...], preferred_element_type=F32)
        o_ref[rows, :] = _layer_norm(ALPHA * h_ref[rows, :] + y, lng_ref[...], lnb_ref[...])


def _t5_bucket(dist):
    max_exact = N_BUCKETS // 2
    n = jnp.maximum(dist, 1).astype(jnp.float32)
    large = max_exact + (jnp.log(n / max_exact) / math.log(MAX_DISTANCE / max_exact)
                         * (N_BUCKETS - max_exact)).astype(jnp.int32)
    large = jnp.minimum(large, N_BUCKETS - 1)
    return jnp.where(dist < max_exact, dist, large)


def _offset_bias(rel_bias):
    steps = WINDOW_STEPS - jnp.arange(2 * Q_BLOCK)
    rows = []
    for g, dil in enumerate(DILATIONS):
        bucket = _t5_bucket(jnp.maximum(steps, 0) * dil)
        bias = rel_bias[:, g * HEADS:(g + 1) * HEADS][bucket].astype(F32)
        rows.append(jnp.where(steps[:, None] >= 0, bias, NEG).T)
    return jnp.stack(rows, 0)[:, :, None, :]


def _side_cast_plan(rows, max_blocks):
    block = BF16_ROWS
    while rows % block or rows // block > max_blocks:
        block += BF16_ROWS
    return block, rows // block


def _cparams(*sem):
    return pltpu.CompilerParams(dimension_semantics=sem, vmem_limit_bytes=VMEM_LIMIT)


def _layer(x, mem, band_bias, hmask, w_in, w_mem_kv, sg_ln_g, sg_ln_b, w_spatial, b_spatial,
           w_proj_a, w_proj_b, w_proj_c, w_gate, b_gate, w_out, ln1_g, ln1_b,
           w_ffn_up, conv_w, conv_b, w_ffn_down, ln2_g, ln2_b):
    bsz, seq, _ = x.shape
    n_tok = bsz * seq
    tm = TOKEN_TILE
    assert seq % SUPER == 0 and SUPER % tm == 0 and tm % (SG_CHUNK * 2) == 0
    assert D_FF % FF_CHUNK == 0 and tm % (16 * DILATIONS[-1]) == 0
    assert seq % MERGE_TILE == 0 and seq % FFN_TILE == 0
    tiles_per_seq = seq // tm
    row = lambda v: v.reshape(1, -1)
    tok = lambda rows, width: pl.BlockSpec((rows, width), lambda i: (i, 0))
    tile3 = lambda rows, width: pl.BlockSpec((1, rows, width), lambda b, t: (b, t, 0))

    wsp = w_spatial.reshape(SG_W // SG_CHUNK, 2 * SG_CHUNK, SG_CHUNK)
    bsp = jnp.repeat(b_spatial.reshape(SG_W // SG_CHUNK, 2, SG_CHUNK).transpose(0, 2, 1),
                     HEAD_DIM, axis=-1)
    qkv_shapes = ([jax.ShapeDtypeStruct((bsz, seq // d, d * HW), BF16) for d in DILATIONS]
                  + [jax.ShapeDtypeStruct((bsz, seq // d, d * 2 * HW), BF16) for d in DILATIONS])
    qkv_specs = ([tile3(tm // d, d * HW) for d in DILATIONS]
                 + [tile3(tm // d, d * 2 * HW) for d in DILATIONS])
    side_w = ((w_gate, 0.5), (w_proj_a, 0.5), (w_proj_b, 0.5), (w_proj_c, 0.5), (w_out, 1.0),
              (w_ffn_up, 1.0), (w_ffn_down, 1.0))
    assert SIDE_CAST_BLOCKS <= bsz * tiles_per_seq
    plans = [_side_cast_plan(w.shape[0], SIDE_CAST_BLOCKS) for w, _ in side_w]

    def side_spec(w, plan):
        rows, n_blocks = plan
        return pl.BlockSpec((rows, w.shape[1]), lambda b, t: (
            jnp.minimum(b * tiles_per_seq + t, n_blocks - 1), 0))

    side_specs = [side_spec(w, plan) for (w, _), plan in zip(side_w, plans)]
    outs = pl.pallas_call(
        functools.partial(_proj_kernel,
                          side=tuple((scale, plan[1]) for (_, scale), plan in zip(side_w, plans))),
        grid=(bsz, tiles_per_seq),
        in_specs=[tile3(tm, D_MODEL), _resident((D_MODEL, IN_W)),
                  pl.BlockSpec((1, MEM_LEN, D_MODEL), lambda b, t: (b, 0, 0)),
                  _resident((D_MODEL, 2 * HW)), _resident((1, SG_W)), _resident((1, SG_W)),
                  _resident(wsp.shape), _resident(bsp.shape), _resident(hmask.shape)] + side_specs,
        out_specs=qkv_specs + [tile3(tm, SG_W), tile3(tm, HW)] + side_specs,
        out_shape=qkv_shapes + [jax.ShapeDtypeStruct((bsz, seq, SG_W), BF16),
                                jax.ShapeDtypeStruct((bsz, seq, HW), BF16)]
        + [jax.ShapeDtypeStruct(w.shape, BF16) for w, _ in side_w],
        scratch_shapes=[pltpu.VMEM((tm // PADDED_DIL * PAD_PITCH, LANES), F32),
                        pltpu.VMEM((MEM_LEN, 2 * HW), BF16)],
        compiler_params=_cparams("arbitrary", "arbitrary"),
        name="proj",
    )(x, w_in.astype(BF16), mem, w_mem_kv.astype(BF16), row(sg_ln_g), row(sg_ln_b), wsp, bsp,
      hmask, *[w for w, _ in side_w])
    q1, q2, q3, kv1, kv2, kv3, ob, oc = outs[:8]
    wg_b, wpa_b, wpb_b, wpc_b, wout_b, wup_b, wdn_b = outs[8:]

    n_super = seq // SUPER
    super3 = lambda d, width: pl.BlockSpec((1, SUPER // d, d * width), lambda b, j: (b, j, 0))

    def prev3(d):
        nblk = SUPER // d // Q_BLOCK
        return pl.BlockSpec((1, Q_BLOCK, d * 2 * HW),
                            lambda b, j: (b, jnp.maximum(j * nblk - 1, 0), 0))

    oa = pl.pallas_call(
        _attn_kernel,
        grid=(bsz, n_super),
        in_specs=[super3(d, HW) for d in DILATIONS] + [super3(d, 2 * HW) for d in DILATIONS]
        + [prev3(d) for d in DILATIONS] + [_resident(band_bias.shape), _resident(hmask.shape)],
        out_specs=pl.BlockSpec((1, SUPER, HW), lambda b, j: (b, j, 0)),
        out_shape=jax.ShapeDtypeStruct((bsz, seq, HW), BF16),
        scratch_shapes=[pltpu.VMEM((N_GROUPS - 1, HW // LANES, SUPER, LANES), F32)] * 3
        + [pltpu.VMEM((HW // LANES, SUPER // PADDED_DIL * PAD_PITCH, LANES), F32)] * 3,
        compiler_params=_cparams("parallel", "parallel"),
        name="attn",
    )(q1, q2, q3, kv1, kv2, kv3, kv1, kv2, kv3, band_bias, hmask)

    h1 = pl.pallas_call(
        _merge_kernel,
        grid=(n_tok // MERGE_TILE,),
        in_specs=[tok(MERGE_TILE, D_MODEL), tok(MERGE_TILE, HW), tok(MERGE_TILE, SG_W),
                  tok(MERGE_TILE, HW),
                  _resident((D_MODEL, 3 * D_MODEL)), _resident((1, 3 * D_MODEL)),
                  _resident((HW, D_MODEL)), _resident((SG_W, D_MODEL)), _resident((HW, D_MODEL)),
                  _resident((D_MODEL, D_MODEL)), _resident((1, D_MODEL)), _resident((1, D_MODEL))],
        out_specs=tok(MERGE_TILE, D_MODEL),
        out_shape=jax.ShapeDtypeStruct((n_tok, D_MODEL), F32),
        compiler_params=_cparams("parallel"),
        name="merge",
    )(x.reshape(n_tok, D_MODEL), oa.reshape(n_tok, HW), ob.reshape(n_tok, SG_W),
      oc.reshape(n_tok, HW), wg_b, row(0.5 * b_gate), wpa_b, wpb_b, wpc_b, wout_b,
      row(ln1_g), row(ln1_b))

    h2 = pl.pallas_call(
        functools.partial(_ffn_kernel, tiles_per_seq=seq // FFN_TILE),
        grid=(n_tok // FFN_TILE,),
        in_specs=[tok(FFN_TILE, D_MODEL), _resident((D_MODEL, 2 * D_FF)), _resident(conv_w.shape),
                  _resident((1, D_FF)), _resident((D_FF, D_MODEL)),
                  _resident((1, D_MODEL)), _resident((1, D_MODEL))],
        out_specs=tok(FFN_TILE, D_MODEL),
        out_shape=jax.ShapeDtypeStruct((n_tok, D_MODEL), F32),
        scratch_shapes=[pltpu.VMEM((CARRY_ROWS + FFN_TILE, FF_CHUNK), F32),
                        pltpu.VMEM((D_FF // FF_CHUNK, CARRY_ROWS, FF_CHUNK), F32),
                        pltpu.VMEM((FFN_TILE, D_FF), BF16)],
        compiler_params=_cparams("arbitrary"),
        name="ffn",
    )(h1, wup_b, conv_w, row(conv_b), wdn_b, row(ln2_g), row(ln2_b))
    return h2.reshape(bsz, seq, D_MODEL)


def kernel(x, mem, rel_bias, w_in, w_mem_kv, sg_ln_g, sg_ln_b, w_spatial, b_spatial, w_proj_a,
           w_proj_b, w_proj_c, w_gate, b_gate, w_out, ln1_g, ln1_b, w_ffn_up, conv_w, conv_b,
           w_ffn_down, ln2_g, ln2_b):
    band_bias = pl.pallas_call(
        _band_bias_kernel,
        grid=(N_GROUPS,),
        in_specs=[pl.BlockSpec((1, HEADS, 1, 2 * Q_BLOCK), lambda g: (g, 0, 0, 0))],
        out_specs=pl.BlockSpec((1, 2, HEADS * Q_BLOCK, 2 * Q_BLOCK), lambda g: (g, 0, 0, 0)),
        out_shape=jax.ShapeDtypeStruct((N_GROUPS, 2, HEADS * Q_BLOCK, 2 * Q_BLOCK), F32),
        name="band_bias",
    )(_offset_bias(rel_bias))
    lane = jnp.arange(HW)[None, None, :] // HEAD_DIM
    hmask = jnp.broadcast_to(lane == jnp.arange(HEADS)[:, None, None],
                             (HEADS, Q_BLOCK, HW)).astype(BF16)
    h = x
    for l in range(w_in.shape[0]):
        h = _layer(h, mem, band_bias, hmask, w_in[l], w_mem_kv[l], sg_ln_g[l], sg_ln_b[l],
                   w_spatial[l], b_spatial[l], w_proj_a[l], w_proj_b[l], w_proj_c[l], w_gate[l],
                   b_gate[l], w_out[l], ln1_g[l], ln1_b[l], w_ffn_up[l], conv_w[l], conv_b[l],
                   w_ffn_down[l], ln2_g[l], ln2_b[l])
    return h
```

```python
import functools
import math

import jax
import jax.numpy as jnp
from jax import lax
from jax.experimental import pallas as pl
from jax.experimental.pallas import tpu as pltpu

F32 = jnp.float32
BF16 = jnp.bfloat16

LANES = 128
BF16_ROWS = 16
D_MODEL = 1024
HEAD_DIM = 64
DILATIONS = (1, 4, 16)
N_GROUPS = 3
HEADS = 4
HW = HEADS * HEAD_DIM
ATTN_W = N_GROUPS * HW
WINDOW_STEPS = 128
Q_BLOCK = 128
SG_CHUNK = 128
SG_W = 512
MEM_LEN = 256
IN_W = 3 * ATTN_W + 2 * SG_W + HW
D_FF = 2816
N_BUCKETS = 32
MAX_DISTANCE = 2048
LN_EPS = 1e-5
DEPTH = 1
ALPHA = (2 * DEPTH) ** 0.25
ATTN_SCALE = HEAD_DIM ** -0.5
LOG2E = math.log2(math.e)
Q_SCALE = ATTN_SCALE * LOG2E
NEG = -1e30

TOKEN_TILE = 512
MERGE_TILE = 1024
FFN_TILE = 1024
SUPER = Q_BLOCK * DILATIONS[-1]
PADDED_DIL = DILATIONS[-1]
PAD_PITCH = 24
SIDE_CAST_BLOCKS = 8
TAIL_ROWS = 256
LAST_TAIL_ROWS = 128
COMBINE_ROWS = 256
FF_CHUNK = 256
CARRY_ROWS = 8
VMEM_LIMIT = 56 * 1024 * 1024

_NT = (((1,), (1,)), ((), ()))


def _gelu(x):
    return 0.5 * x * (1.0 + jnp.tanh(math.sqrt(2.0 / math.pi) * (x + 0.044715 * (x * x * x))))


def _layer_norm(x, g, b):
    mu = jnp.mean(x, axis=-1, keepdims=True)
    xc = x - mu
    var = jnp.mean(xc * xc, axis=-1, keepdims=True)
    return xc * lax.rsqrt(var + LN_EPS) * g + b


def _tail_chunks(rows):
    edges = list(range(0, rows - TAIL_ROWS, TAIL_ROWS)) + list(range(rows - TAIL_ROWS, rows + 1,
                                                                     LAST_TAIL_ROWS))
    return [slice(lo, hi) for lo, hi in zip(edges[:-1], edges[1:])]


def _resident(shape):
    zeros = (0,) * len(shape)
    return pl.BlockSpec(shape, lambda *_: zeros, pipeline_mode=pl.Buffered(1))


def _head_dense(per_head):
    low = lax.broadcasted_iota(jnp.int32, (per_head[0].shape[0], LANES), 1) < HEAD_DIM
    return jnp.concatenate([jnp.where(low, per_head[h], per_head[h + 1])
                            for h in range(0, HEADS, 2)], axis=1)


def _packed_heads_attention(q, k, v, hm_ref, bias_fn):
    m_rows = q.shape[0]
    qs = jnp.concatenate([q * hm_ref[h] for h in range(HEADS)], axis=0)
    s = lax.dot_general(qs, k, _NT, preferred_element_type=F32)
    ps, ms, ls = [], [], []
    for h in range(HEADS):
        sh = s[h * m_rows:(h + 1) * m_rows]
        bias = bias_fn(h)
        if bias is not None:
            sh = sh + bias
        m = jnp.max(sh, axis=-1, keepdims=True)
        p = jnp.exp2(sh - m)
        ps.append(p.astype(BF16))
        ms.append(m)
        ls.append(jnp.sum(p, axis=-1, keepdims=True))
    pv = jnp.dot(jnp.concatenate(ps, axis=0), v, preferred_element_type=F32)
    half = lambda h: slice((h // 2) * LANES, (h // 2 + 1) * LANES)
    pv = _head_dense([pv[h * m_rows:(h + 1) * m_rows, half(h)] for h in range(HEADS)])
    return pv, _head_dense(ms), _head_dense(ls)


def _band_bias_kernel(f_ref, o_ref):
    lane = lax.broadcasted_iota(jnp.int32, (Q_BLOCK, 2 * Q_BLOCK), 1)
    no_prev = jnp.where(lane < Q_BLOCK, NEG, 0.0)
    for h in range(HEADS):
        rep = jnp.broadcast_to(f_ref[0, h] * LOG2E, (Q_BLOCK, 2 * Q_BLOCK))
        band = pltpu.roll(rep, 0, 1, stride=1, stride_axis=0)
        o_ref[0, 0, h * Q_BLOCK:(h + 1) * Q_BLOCK] = band
        o_ref[0, 1, h * Q_BLOCK:(h + 1) * Q_BLOCK] = band + no_prev


def _proj_kernel(*refs, side):
    n_side = len(side)
    x_ref, win_ref, mem_ref, wkv_ref, lng_ref, lnb_ref, wsp_ref, bsp_ref, hm_ref = refs[:9]
    side_in = refs[9:9 + n_side]
    q1_ref, q2_ref, q3_ref, kv1_ref, kv2_ref, kv3_ref, ob_ref, oc_ref = refs[9 + n_side:17 + n_side]
    side_out = refs[17 + n_side:17 + 2 * n_side]
    de_buf, kvm = refs[17 + 2 * n_side:]
    xb = x_ref[0].astype(BF16)
    tm = xb.shape[0]

    step = pl.program_id(0) * pl.num_programs(1) + pl.program_id(1)
    for (scale, n_blocks), w_ref, o_ref in zip(side, side_in, side_out):
        @pl.when(step < n_blocks)
        def _(scale=scale, w_ref=w_ref, o_ref=o_ref):
            o_ref[...] = (w_ref[...] * scale).astype(BF16)

    @pl.when(pl.program_id(1) == 0)
    def _():
        kvm[...] = jnp.dot(mem_ref[0].astype(BF16), wkv_ref[...],
                           preferred_element_type=F32).astype(BF16)

    def proj(lo, width):
        return jnp.dot(xb, win_ref[:, lo:lo + width], preferred_element_type=F32)

    def put(ref, lane0, val, dil):
        if dil == 1:
            ref[0, :, lane0:lane0 + HW] = val.astype(BF16)
            return
        stride_cols = ref.shape[2] // dil
        pitch = PAD_PITCH if dil == PADDED_DIL else dil
        for c in range(HW // LANES):
            half = val[:, c * LANES:(c + 1) * LANES]
            if pitch == dil:
                de_buf[:tm] = half
            else:
                for i in range(tm // dil):
                    de_buf[i * pitch:i * pitch + dil] = half[i * dil:(i + 1) * dil]
            for r in range(dil):
                piece = de_buf[pl.ds(r, tm // dil, stride=pitch), :]
                col = r * stride_cols + lane0 + c * LANES
                ref[0, :, col:col + LANES] = piece.astype(BF16)

    def a_proj(g, part):
        q_ref, kv_ref = ((q1_ref, kv1_ref), (q2_ref, kv2_ref), (q3_ref, kv3_ref))[g]
        if part == 0:
            put(q_ref, 0, proj(g * HW, HW) * Q_SCALE, DILATIONS[g])
        else:
            put(kv_ref, (part - 1) * HW, proj(part * ATTN_W + g * HW, HW), DILATIONS[g])

    fillers = iter([(g, part) for g in range(N_GROUPS) for part in range(3)])
    fill = lambda: a_proj(*next(fillers))

    def proj_halves(lo):
        return jnp.concatenate([proj(lo, HW), proj(lo + HW, HW)], axis=1)

    qc = (proj(3 * ATTN_W + 2 * SG_W, HW) * Q_SCALE).astype(BF16)
    u_pre = proj_halves(3 * ATTN_W)
    v_pre = proj_halves(3 * ATTN_W + SG_W)

    n_chunks = tm // SG_CHUNK
    u, vn = [], []
    for c in range(n_chunks):
        rows = slice(c * SG_CHUNK, (c + 1) * SG_CHUNK)
        fill()
        u.append(_gelu(u_pre[rows]))
        vn.append(_layer_norm(_gelu(v_pre[rows]), lng_ref[...], lnb_ref[...]).astype(BF16))
        pv, _, l = _packed_heads_attention(qc[rows], kvm[:, :HW], kvm[:, HW:], hm_ref,
                                           lambda h: None)
        oc_ref[0, rows, :] = (pv * (1.0 / l)).astype(BF16)
        fill()

    row = lax.broadcasted_iota(jnp.int32, (2 * SG_CHUNK, SG_CHUNK), 0) & (SG_CHUNK - 1)
    col = lax.broadcasted_iota(jnp.int32, (2 * SG_CHUNK, SG_CHUNK), 1)
    low_half = lax.broadcasted_iota(jnp.int32, (SG_CHUNK, SG_CHUNK), 1) < HEAD_DIM
    for j in range(SG_W // SG_CHUNK):
        cols = slice(j * SG_CHUNK, (j + 1) * SG_CHUNK)
        w_pair = jnp.where(row >= col, wsp_ref[j], 0.0).astype(BF16)
        rhs = jnp.concatenate([vn[c][:, cols] for c in range(n_chunks)], axis=1)
        r = jnp.dot(w_pair, rhs, preferred_element_type=F32)
        for c in range(n_chunks):
            rows = slice(c * SG_CHUNK, (c + 1) * SG_CHUNK)
            sv = jnp.where(low_half, r[:SG_CHUNK, rows], r[SG_CHUNK:, rows]) + bsp_ref[j]
            ob_ref[0, rows, cols] = (u[c][:, cols] * sv).astype(BF16)
        if j == 0:
            fill()


def _attn_kernel(q1_ref, q2_ref, q3_ref, kv1_ref, kv2_ref, kv3_ref, kp1_ref, kp2_ref, kp3_ref,
                 bias_ref, hm_ref, oa_ref, pv_nat, m_nat, l_nat, pv_pad, m_pad, l_pad):
    lead = (pl.program_id(1) == 0).astype(jnp.int32)

    groups = ((q1_ref, kv1_ref, kp1_ref), (q2_ref, kv2_ref, kp2_ref), (q3_ref, kv3_ref, kp3_ref))
    for g, (q_ref, kv_ref, kvp_ref) in enumerate(groups):
        dil = DILATIONS[g]
        nblk = SUPER // dil // Q_BLOCK
        for r in range(dil):
            kc = slice(r * 2 * HW, r * 2 * HW + HW)
            vc = slice(r * 2 * HW + HW, (r + 1) * 2 * HW)
            qc = slice(r * HW, (r + 1) * HW)
            for i in range(nblk):
                rq = slice(i * Q_BLOCK, (i + 1) * Q_BLOCK)
                if i == 0:
                    k2 = jnp.concatenate([kvp_ref[0, :, kc], kv_ref[0, rq, kc]], axis=0)
                    v2 = jnp.concatenate([kvp_ref[0, :, vc], kv_ref[0, rq, vc]], axis=0)
                else:
                    rk = slice((i - 1) * Q_BLOCK, (i + 1) * Q_BLOCK)
                    k2 = kv_ref[0, rk, kc]
                    v2 = kv_ref[0, rk, vc]
                table = lead if i == 0 else 0
                pv, m, l = _packed_heads_attention(
                    q_ref[0, rq, qc], k2, v2, hm_ref,
                    lambda h, table=table: bias_ref[g, table, h * Q_BLOCK:(h + 1) * Q_BLOCK, :])
                for c in range(HW // LANES):
                    lanes = slice(c * LANES, (c + 1) * LANES)
                    if dil == PADDED_DIL:
                        at = (c, pl.ds(i * Q_BLOCK * PAD_PITCH + r, Q_BLOCK, stride=PAD_PITCH))
                        dst = (pv_pad, m_pad, l_pad)
                    else:
                        nat = pl.ds(i * Q_BLOCK * dil + r, Q_BLOCK, stride=dil) if dil > 1 else rq
                        at = (g, c, nat)
                        dst = (pv_nat, m_nat, l_nat)
                    for ref, val in zip(dst, (pv, m, l)):
                        ref[at + (slice(None),)] = val[:, lanes]

    def padded_rows(ref, c, t):
        runs = COMBINE_ROWS // PADDED_DIL
        return jnp.concatenate([ref[c, pl.ds((t * runs + k) * PAD_PITCH, PADDED_DIL), :]
                                for k in range(runs)], axis=0)

    for t in range(SUPER // COMBINE_ROWS):
        rows = slice(t * COMBINE_ROWS, (t + 1) * COMBINE_ROWS)
        for c in range(HW // LANES):
            m1, m2, m3 = m_nat[0, c, rows, :], m_nat[1, c, rows, :], padded_rows(m_pad, c, t)
            m = jnp.maximum(jnp.maximum(m1, m2), m3)
            e1, e2, e3 = jnp.exp2(m1 - m), jnp.exp2(m2 - m), jnp.exp2(m3 - m)
            num = (e1 * pv_nat[0, c, rows, :] + e2 * pv_nat[1, c, rows, :]
                   + e3 * padded_rows(pv_pad, c, t))
            den = (e1 * l_nat[0, c, rows, :] + e2 * l_nat[1, c, rows, :]
                   + e3 * padded_rows(l_pad, c, t))
            oa_ref[0, rows, c * LANES:(c + 1) * LANES] = (num * (1.0 / den)).astype(BF16)


def _merge_kernel(x_ref, oa_ref, ob_ref, oc_ref, wg_ref, bg_ref, wpa_ref, wpb_ref, wpc_ref,
                  wout_ref, lng_ref, lnb_ref, h_ref):
    x = x_ref[...]
    xb = x.astype(BF16)

    def gated(k, o_ref, wp_ref):
        cols = slice(k * D_MODEL, (k + 1) * D_MODEL)
        t = jnp.tanh(jnp.dot(xb, wg_ref[:, cols], preferred_element_type=F32) + bg_ref[:, cols])
        half_p = jnp.dot(o_ref[...], wp_ref[...], preferred_element_type=F32)
        return half_p + half_p * t

    merged = (gated(0, oa_ref, wpa_ref) + gated(1, ob_ref, wpb_ref)
              + gated(2, oc_ref, wpc_ref)).astype(BF16)
    for rows in _tail_chunks(x_ref.shape[0]):
        y = jnp.dot(merged[rows], wout_ref[...], preferred_element_type=F32)
        h_ref[rows, :] = _layer_norm(ALPHA * x_ref[rows, :] + y, lng_ref[...], lnb_ref[...])


def _ffn_kernel(h_ref, wup_ref, cw_ref, cb_ref, wdn_ref, lng_ref, lnb_ref, o_ref,
                a_buf, carry, g_buf, *, tiles_per_seq):
    tm = h_ref.shape[0]
    n_chunks = D_FF // FF_CHUNK

    @pl.when(pl.program_id(0) % tiles_per_seq == 0)
    def _():
        carry[...] = jnp.zeros(carry.shape, F32)

    h = h_ref[...]
    hb = h.astype(BF16)
    for c in range(n_chunks):
        cols = slice(c * FF_CHUNK, (c + 1) * FF_CHUNK)
        a = jnp.dot(hb, wup_ref[:, cols], preferred_element_type=F32)
        gate = jnp.dot(hb, wup_ref[:, D_FF + c * FF_CHUNK:D_FF + (c + 1) * FF_CHUNK],
                       preferred_element_type=F32)
        a_buf[:CARRY_ROWS] = carry[c]
        a_buf[CARRY_ROWS:] = a
        carry[c] = a[tm - CARRY_ROWS:]
        conv = (cw_ref[0:1, cols] * a_buf[CARRY_ROWS - 2:CARRY_ROWS - 2 + tm]
                + cw_ref[1:2, cols] * a_buf[CARRY_ROWS - 1:CARRY_ROWS - 1 + tm]
                + cw_ref[2:3, cols] * a + cb_ref[:, cols])
        g_buf[:, cols] = (_gelu(conv) * gate).astype(BF16)
    for rows in _tail_chunks(tm):
        y = jnp.dot(g_buf[rows, :], wdn_ref[...], preferred_element_type=F32)
        o_ref[rows, :] = _layer_norm(ALPHA * h_ref[rows, :] + y, lng_ref[...], lnb_ref[...])


def _t5_bucket(dist):
    max_exact = N_BUCKETS // 2
    n = jnp.maximum(dist, 1).astype(jnp.float32)
    large = max_exact + (jnp.log(n / max_exact) / math.log(MAX_DISTANCE / max_exact)
                         * (N_BUCKETS - max_exact)).astype(jnp.int32)
    large = jnp.minimum(large, N_BUCKETS - 1)
    return jnp.where(dist < max_exact, dist, large)


def _offset_bias(rel_bias):
    steps = WINDOW_STEPS - jnp.arange(2 * Q_BLOCK)
    rows = []
    for g, dil in enumerate(DILATIONS):
        bucket = _t5_bucket(jnp.maximum(steps, 0) * dil)
        bias = rel_bias[:, g * HEADS:(g + 1) * HEADS][bucket].astype(F32)
        rows.append(jnp.where(steps[:, None] >= 0, bias, NEG).T)
    return jnp.stack(rows, 0)[:, :, None, :]


def _side_cast_plan(rows, max_blocks):
    block = BF16_ROWS
    while rows % block or rows // block > max_blocks:
        block += BF16_ROWS
    return block, rows // block


def _cparams(*sem):
    return pltpu.CompilerParams(dimension_semantics=sem, vmem_limit_bytes=VMEM_LIMIT)


def _layer(x, mem, band_bias, hmask, w_in, w_mem_kv, sg_ln_g, sg_ln_b, w_spatial, b_spatial,
           w_proj_a, w_proj_b, w_proj_c, w_gate, b_gate, w_out, ln1_g, ln1_b,
           w_ffn_up, conv_w, conv_b, w_ffn_down, ln2_g, ln2_b):
    bsz, seq, _ = x.shape
    n_tok = bsz * seq
    tm = TOKEN_TILE
    assert seq % SUPER == 0 and SUPER % tm == 0 and tm % (SG_CHUNK * 2) == 0
    assert D_FF % FF_CHUNK == 0 and tm % (16 * DILATIONS[-1]) == 0
    assert seq % MERGE_TILE == 0 and seq % FFN_TILE == 0
    tiles_per_seq = seq // tm
    row = lambda v: v.reshape(1, -1)
    tok = lambda rows, width: pl.BlockSpec((rows, width), lambda i: (i, 0))
    tile3 = lambda rows, width: pl.BlockSpec((1, rows, width), lambda b, t: (b, t, 0))

    wsp = w_spatial.reshape(SG_W // SG_CHUNK, 2 * SG_CHUNK, SG_CHUNK)
    bsp = jnp.repeat(b_spatial.reshape(SG_W // SG_CHUNK, 2, SG_CHUNK).transpose(0, 2, 1),
                     HEAD_DIM, axis=-1)
    qkv_shapes = ([jax.ShapeDtypeStruct((bsz, seq // d, d * HW), BF16) for d in DILATIONS]
                  + [jax.ShapeDtypeStruct((bsz, seq // d, d * 2 * HW), BF16) for d in DILATIONS])
    qkv_specs = ([tile3(tm // d, d * HW) for d in DILATIONS]
                 + [tile3(tm // d, d * 2 * HW) for d in DILATIONS])
    side_w = ((w_gate, 0.5), (w_proj_a, 0.5), (w_proj_b, 0.5), (w_proj_c, 0.5), (w_out, 1.0),
              (w_ffn_up, 1.0), (w_ffn_down, 1.0))
    assert SIDE_CAST_BLOCKS <= bsz * tiles_per_seq
    plans = [_side_cast_plan(w.shape[0], SIDE_CAST_BLOCKS) for w, _ in side_w]

    def side_spec(w, plan):
        rows, n_blocks = plan
        return pl.BlockSpec((rows, w.shape[1]), lambda b, t: (
            jnp.minimum(b * tiles_per_seq + t, n_blocks - 1), 0))

    side_specs = [side_spec(w, plan) for (w, _), plan in zip(side_w, plans)]
    outs = pl.pallas_call(
        functools.partial(_proj_kernel,
                          side=tuple((scale, plan[1]) for (_, scale), plan in zip(side_w, plans))),
        grid=(bsz, tiles_per_seq),
        in_specs=[tile3(tm, D_MODEL), _resident((D_MODEL, IN_W)),
                  pl.BlockSpec((1, MEM_LEN, D_MODEL), lambda b, t: (b, 0, 0)),
                  _resident((D_MODEL, 2 * HW)), _resident((1, SG_W)), _resident((1, SG_W)),
                  _resident(wsp.shape), _resident(bsp.shape), _resident(hmask.shape)] + side_specs,
        out_specs=qkv_specs + [tile3(tm, SG_W), tile3(tm, HW)] + side_specs,
        out_shape=qkv_shapes + [jax.ShapeDtypeStruct((bsz, seq, SG_W), BF16),
                                jax.ShapeDtypeStruct((bsz, seq, HW), BF16)]
        + [jax.ShapeDtypeStruct(w.shape, BF16) for w, _ in side_w],
        scratch_shapes=[pltpu.VMEM((tm // PADDED_DIL * PAD_PITCH, LANES), F32),
                        pltpu.VMEM((MEM_LEN, 2 * HW), BF16)],
        compiler_params=_cparams("arbitrary", "arbitrary"),
        name="proj",
    )(x, w_in.astype(BF16), mem, w_mem_kv.astype(BF16), row(sg_ln_g), row(sg_ln_b), wsp, bsp,
      hmask, *[w for w, _ in side_w])
    q1, q2, q3, kv1, kv2, kv3, ob, oc = outs[:8]
    wg_b, wpa_b, wpb_b, wpc_b, wout_b, wup_b, wdn_b = outs[8:]

    n_super = seq // SUPER
    super3 = lambda d, width: pl.BlockSpec((1, SUPER // d, d * width), lambda b, j: (b, j, 0))

    def prev3(d):
        nblk = SUPER // d // Q_BLOCK
        return pl.BlockSpec((1, Q_BLOCK, d * 2 * HW),
                            lambda b, j: (b, jnp.maximum(j * nblk - 1, 0), 0))

    oa = pl.pallas_call(
        _attn_kernel,
        grid=(bsz, n_super),
        in_specs=[super3(d, HW) for d in DILATIONS] + [super3(d, 2 * HW) for d in DILATIONS]
        + [prev3(d) for d in DILATIONS] + [_resident(band_bias.shape), _resident(hmask.shape)],
        out_specs=pl.BlockSpec((1, SUPER, HW), lambda b, j: (b, j, 0)),
        out_shape=jax.ShapeDtypeStruct((bsz, seq, HW), BF16),
        scratch_shapes=[pltpu.VMEM((N_GROUPS - 1, HW // LANES, SUPER, LANES), F32)] * 3
        + [pltpu.VMEM((HW // LANES, SUPER // PADDED_DIL * PAD_PITCH, LANES), F32)] * 3,
        compiler_params=_cparams("parallel", "parallel"),
        name="attn",
    )(q1, q2, q3, kv1, kv2, kv3, kv1, kv2, kv3, band_bias, hmask)

    h1 = pl.pallas_call(
        _merge_kernel,
        grid=(n_tok // MERGE_TILE,),
        in_specs=[tok(MERGE_TILE, D_MODEL), tok(MERGE_TILE, HW), tok(MERGE_TILE, SG_W),
                  tok(MERGE_TILE, HW),
                  _resident((D_MODEL, 3 * D_MODEL)), _resident((1, 3 * D_MODEL)),
                  _resident((HW, D_MODEL)), _resident((SG_W, D_MODEL)), _resident((HW, D_MODEL)),
                  _resident((D_MODEL, D_MODEL)), _resident((1, D_MODEL)), _resident((1, D_MODEL))],
        out_specs=tok(MERGE_TILE, D_MODEL),
        out_shape=jax.ShapeDtypeStruct((n_tok, D_MODEL), F32),
        compiler_params=_cparams("parallel"),
        name="merge",
    )(x.reshape(n_tok, D_MODEL), oa.reshape(n_tok, HW), ob.reshape(n_tok, SG_W),
      oc.reshape(n_tok, HW), wg_b, row(0.5 * b_gate), wpa_b, wpb_b, wpc_b, wout_b,
      row(ln1_g), row(ln1_b))

    h2 = pl.pallas_call(
        functools.partial(_ffn_kernel, tiles_per_seq=seq // FFN_TILE),
        grid=(n_tok // FFN_TILE,),
        in_specs=[tok(FFN_TILE, D_MODEL), _resident((D_MODEL, 2 * D_FF)), _resident(conv_w.shape),
                  _resident((1, D_FF)), _resident((D_FF, D_MODEL)),
                  _resident((1, D_MODEL)), _resident((1, D_MODEL))],
        out_specs=tok(FFN_TILE, D_MODEL),
        out_shape=jax.ShapeDtypeStruct((n_tok, D_MODEL), F32),
        scratch_shapes=[pltpu.VMEM((CARRY_ROWS + FFN_TILE, FF_CHUNK), F32),
                        pltpu.VMEM((D_FF // FF_CHUNK, CARRY_ROWS, FF_CHUNK), F32),
                        pltpu.VMEM((FFN_TILE, D_FF), BF16)],
        compiler_params=_cparams("arbitrary"),
        name="ffn",
    )(h1, wup_b, conv_w, row(conv_b), wdn_b, row(ln2_g), row(ln2_b))
    return h2.reshape(bsz, seq, D_MODEL)


def kernel(x, mem, rel_bias, w_in, w_mem_kv, sg_ln_g, sg_ln_b, w_spatial, b_spatial, w_proj_a,
           w_proj_b, w_proj_c, w_gate, b_gate, w_out, ln1_g, ln1_b, w_ffn_up, conv_w, conv_b,
           w_ffn_down, ln2_g, ln2_b):
    band_bias = pl.pallas_call(
        _band_bias_kernel,
        grid=(N_GROUPS,),
        in_specs=[pl.BlockSpec((1, HEADS, 1, 2 * Q_BLOCK), lambda g: (g, 0, 0, 0))],
        out_specs=pl.BlockSpec((1, 2, HEADS * Q_BLOCK, 2 * Q_BLOCK), lambda g: (g, 0, 0, 0)),
        out_shape=jax.ShapeDtypeStruct((N_GROUPS, 2, HEADS * Q_BLOCK, 2 * Q_BLOCK), F32),
        name="band_bias",
    )(_offset_bias(rel_bias))
    lane = jnp.arange(HW)[None, None, :] // HEAD_DIM
    hmask = jnp.broadcast_to(lane == jnp.arange(HEADS)[:, None, None],
                             (HEADS, Q_BLOCK, HW)).astype(BF16)
    h = x
    for l in range(w_in.shape[0]):
        h = _layer(h, mem, band_bias, hmask, w_in[l], w_mem_kv[l], sg_ln_g[l], sg_ln_b[l],
                   w_spatial[l], b_spatial[l], w_proj_a[l], w_proj_b[l], w_proj_c[l], w_gate[l],
                   b_gate[l], w_out[l], ln1_g[l], ln1_b[l], w_ffn_up[l], conv_w[l], conv_b[l],
                   w_ffn_down[l], ln2_g[l], ln2_b[l])
    return h
```

```python
import functools
import math

import jax
import jax.numpy as jnp
from jax import lax
from jax.experimental import pallas as pl
from jax.experimental.pallas import tpu as pltpu

F32 = jnp.float32
BF16 = jnp.bfloat16

LANES = 128
BF16_ROWS = 16
D_MODEL = 1024
HEAD_DIM = 64
DILATIONS = (1, 4, 16)
N_GROUPS = 3
HEADS = 4
HW = HEADS * HEAD_DIM
ATTN_W = N_GROUPS * HW
WINDOW_STEPS = 128
Q_BLOCK = 128
SG_CHUNK = 128
SG_W = 512
MEM_LEN = 256
IN_W = 3 * ATTN_W + 2 * SG_W + HW
D_FF = 2816
N_BUCKETS = 32
MAX_DISTANCE = 2048
LN_EPS = 1e-5
DEPTH = 1
ALPHA = (2 * DEPTH) ** 0.25
ATTN_SCALE = HEAD_DIM ** -0.5
LOG2E = math.log2(math.e)
Q_SCALE = ATTN_SCALE * LOG2E
NEG = -1e30

TOKEN_TILE = 512
MERGE_TILE = 1024
FFN_TILE = 1024
SUPER = Q_BLOCK * DILATIONS[-1]
PADDED_DIL = DILATIONS[-1]
PAD_PITCH = 24
SIDE_CAST_BLOCKS = 8
TAIL_ROWS = 256
COMBINE_ROWS = 256
FF_CHUNK = 256
CARRY_ROWS = 8
VMEM_LIMIT = 56 * 1024 * 1024

_NT = (((1,), (1,)), ((), ()))


def _gelu(x):
    return 0.5 * x * (1.0 + jnp.tanh(math.sqrt(2.0 / math.pi) * (x + 0.044715 * (x * x * x))))


def _layer_norm(x, g, b):
    mu = jnp.mean(x, axis=-1, keepdims=True)
    xc = x - mu
    var = jnp.mean(xc * xc, axis=-1, keepdims=True)
    return xc * lax.rsqrt(var + LN_EPS) * g + b


def _resident(shape):
    zeros = (0,) * len(shape)
    return pl.BlockSpec(shape, lambda *_: zeros, pipeline_mode=pl.Buffered(1))


def _head_dense(per_head):
    low = lax.broadcasted_iota(jnp.int32, (per_head[0].shape[0], LANES), 1) < HEAD_DIM
    return jnp.concatenate([jnp.where(low, per_head[h], per_head[h + 1])
                            for h in range(0, HEADS, 2)], axis=1)


def _packed_heads_attention(q, k, v, hm_ref, bias_fn):
    m_rows = q.shape[0]
    qs = jnp.concatenate([q * hm_ref[h] for h in range(HEADS)], axis=0)
    s = lax.dot_general(qs, k, _NT, preferred_element_type=F32)
    ps, ms, ls = [], [], []
    for h in range(HEADS):
        sh = s[h * m_rows:(h + 1) * m_rows]
        bias = bias_fn(h)
        if bias is not None:
            sh = sh + bias
        m = jnp.max(sh, axis=-1, keepdims=True)
        p = jnp.exp2(sh - m)
        ps.append(p.astype(BF16))
        ms.append(m)
        ls.append(jnp.sum(p, axis=-1, keepdims=True))
    pv = jnp.dot(jnp.concatenate(ps, axis=0), v, preferred_element_type=F32)
    half = lambda h: slice((h // 2) * LANES, (h // 2 + 1) * LANES)
    pv = _head_dense([pv[h * m_rows:(h + 1) * m_rows, half(h)] for h in range(HEADS)])
    return pv, _head_dense(ms), _head_dense(ls)


def _band_bias_kernel(f_ref, o_ref):
    lane = lax.broadcasted_iota(jnp.int32, (Q_BLOCK, 2 * Q_BLOCK), 1)
    no_prev = jnp.where(lane < Q_BLOCK, NEG, 0.0)
    for h in range(HEADS):
        rep = jnp.broadcast_to(f_ref[0, h] * LOG2E, (Q_BLOCK, 2 * Q_BLOCK))
        band = pltpu.roll(rep, 0, 1, stride=1, stride_axis=0)
        o_ref[0, 0, h * Q_BLOCK:(h + 1) * Q_BLOCK] = band
        o_ref[0, 1, h * Q_BLOCK:(h + 1) * Q_BLOCK] = band + no_prev


def _proj_kernel(*refs, side):
    n_side = len(side)
    x_ref, win_ref, mem_ref, wkv_ref, lng_ref, lnb_ref, wsp_ref, bsp_ref, hm_ref = refs[:9]
    side_in = refs[9:9 + n_side]
    q1_ref, q2_ref, q3_ref, kv1_ref, kv2_ref, kv3_ref, ob_ref, oc_ref = refs[9 + n_side:17 + n_side]
    side_out = refs[17 + n_side:17 + 2 * n_side]
    de_buf, kvm = refs[17 + 2 * n_side:]
    xb = x_ref[0].astype(BF16)
    tm = xb.shape[0]

    step = pl.program_id(0) * pl.num_programs(1) + pl.program_id(1)
    for (scale, n_blocks), w_ref, o_ref in zip(side, side_in, side_out):
        @pl.when(step < n_blocks)
        def _(scale=scale, w_ref=w_ref, o_ref=o_ref):
            o_ref[...] = (w_ref[...] * scale).astype(BF16)

    @pl.when(pl.program_id(1) == 0)
    def _():
        kvm[...] = jnp.dot(mem_ref[0].astype(BF16), wkv_ref[...],
                           preferred_element_type=F32).astype(BF16)

    def proj(lo, width):
        return jnp.dot(xb, win_ref[:, lo:lo + width], preferred_element_type=F32)

    def put(ref, lane0, val, dil):
        if dil == 1:
            ref[0, :, lane0:lane0 + HW] = val.astype(BF16)
            return
        stride_cols = ref.shape[2] // dil
        pitch = PAD_PITCH if dil == PADDED_DIL else dil
        for c in range(HW // LANES):
            half = val[:, c * LANES:(c + 1) * LANES]
            if pitch == dil:
                de_buf[:tm] = half
            else:
                for i in range(tm // dil):
                    de_buf[i * pitch:i * pitch + dil] = half[i * dil:(i + 1) * dil]
            for r in range(dil):
                piece = de_buf[pl.ds(r, tm // dil, stride=pitch), :]
                col = r * stride_cols + lane0 + c * LANES
                ref[0, :, col:col + LANES] = piece.astype(BF16)

    def a_proj(g, part):
        q_ref, kv_ref = ((q1_ref, kv1_ref), (q2_ref, kv2_ref), (q3_ref, kv3_ref))[g]
        if part == 0:
            put(q_ref, 0, proj(g * HW, HW) * Q_SCALE, DILATIONS[g])
        else:
            put(kv_ref, (part - 1) * HW, proj(part * ATTN_W + g * HW, HW), DILATIONS[g])

    fillers = iter([(g, part) for g in reversed(range(N_GROUPS)) for part in range(3)])
    fill = lambda: a_proj(*next(fillers))

    def proj_halves(lo):
        return jnp.concatenate([proj(lo, HW), proj(lo + HW, HW)], axis=1)

    qc = (proj(3 * ATTN_W + 2 * SG_W, HW) * Q_SCALE).astype(BF16)
    u_pre = proj_halves(3 * ATTN_W)
    v_pre = proj_halves(3 * ATTN_W + SG_W)

    n_chunks = tm // SG_CHUNK
    u, vn = [], []
    for c in range(n_chunks):
        rows = slice(c * SG_CHUNK, (c + 1) * SG_CHUNK)
        fill()
        u.append(_gelu(u_pre[rows]))
        vn.append(_layer_norm(_gelu(v_pre[rows]), lng_ref[...], lnb_ref[...]).astype(BF16))
        pv, _, l = _packed_heads_attention(qc[rows], kvm[:, :HW], kvm[:, HW:], hm_ref,
                                           lambda h: None)
        oc_ref[0, rows, :] = (pv * (1.0 / l)).astype(BF16)
        fill()

    row = lax.broadcasted_iota(jnp.int32, (2 * SG_CHUNK, SG_CHUNK), 0) & (SG_CHUNK - 1)
    col = lax.broadcasted_iota(jnp.int32, (2 * SG_CHUNK, SG_CHUNK), 1)
    low_half = lax.broadcasted_iota(jnp.int32, (SG_CHUNK, SG_CHUNK), 1) < HEAD_DIM
    for j in range(SG_W // SG_CHUNK):
        cols = slice(j * SG_CHUNK, (j + 1) * SG_CHUNK)
        w_pair = jnp.where(row >= col, wsp_ref[j], 0.0).astype(BF16)
        rhs = jnp.concatenate([vn[c][:, cols] for c in range(n_chunks)], axis=1)
        r = jnp.dot(w_pair, rhs, preferred_element_type=F32)
        for c in range(n_chunks):
            rows = slice(c * SG_CHUNK, (c + 1) * SG_CHUNK)
            sv = jnp.where(low_half, r[:SG_CHUNK, rows], r[SG_CHUNK:, rows]) + bsp_ref[j]
            ob_ref[0, rows, cols] = (u[c][:, cols] * sv).astype(BF16)
        if j == 0:
            fill()


def _attn_kernel(q1_ref, q2_ref, q3_ref, kv1_ref, kv2_ref, kv3_ref, kp1_ref, kp2_ref, kp3_ref,
                 bias_ref, hm_ref, oa_ref, pv_nat, m_nat, l_nat, pv_pad, m_pad, l_pad):
    lead = (pl.program_id(1) == 0).astype(jnp.int32)

    groups = ((q1_ref, kv1_ref, kp1_ref), (q2_ref, kv2_ref, kp2_ref), (q3_ref, kv3_ref, kp3_ref))
    for g, (q_ref, kv_ref, kvp_ref) in enumerate(groups):
        dil = DILATIONS[g]
        nblk = SUPER // dil // Q_BLOCK
        for r in range(dil):
            kc = slice(r * 2 * HW, r * 2 * HW + HW)
            vc = slice(r * 2 * HW + HW, (r + 1) * 2 * HW)
            qc = slice(r * HW, (r + 1) * HW)
            for i in range(nblk):
                rq = slice(i * Q_BLOCK, (i + 1) * Q_BLOCK)
                if i == 0:
                    k2 = jnp.concatenate([kvp_ref[0, :, kc], kv_ref[0, rq, kc]], axis=0)
                    v2 = jnp.concatenate([kvp_ref[0, :, vc], kv_ref[0, rq, vc]], axis=0)
                else:
                    rk = slice((i - 1) * Q_BLOCK, (i + 1) * Q_BLOCK)
                    k2 = kv_ref[0, rk, kc]
                    v2 = kv_ref[0, rk, vc]
                table = lead if i == 0 else 0
                pv, m, l = _packed_heads_attention(
                    q_ref[0, rq, qc], k2, v2, hm_ref,
                    lambda h, table=table: bias_ref[g, table, h * Q_BLOCK:(h + 1) * Q_BLOCK, :])
                for c in range(HW // LANES):
                    lanes = slice(c * LANES, (c + 1) * LANES)
                    if dil == PADDED_DIL:
                        at = (c, pl.ds(i * Q_BLOCK * PAD_PITCH + r, Q_BLOCK, stride=PAD_PITCH))
                        dst = (pv_pad, m_pad, l_pad)
                    else:
                        nat = pl.ds(i * Q_BLOCK * dil + r, Q_BLOCK, stride=dil) if dil > 1 else rq
                        at = (g, c, nat)
                        dst = (pv_nat, m_nat, l_nat)
                    for ref, val in zip(dst, (pv, m, l)):
                        ref[at + (slice(None),)] = val[:, lanes]

    def padded_rows(ref, c, t):
        runs = COMBINE_ROWS // PADDED_DIL
        return jnp.concatenate([ref[c, pl.ds((t * runs + k) * PAD_PITCH, PADDED_DIL), :]
                                for k in range(runs)], axis=0)

    for t in range(SUPER // COMBINE_ROWS):
        rows = slice(t * COMBINE_ROWS, (t + 1) * COMBINE_ROWS)
        for c in range(HW // LANES):
            m1, m2, m3 = m_nat[0, c, rows, :], m_nat[1, c, rows, :], padded_rows(m_pad, c, t)
            m = jnp.maximum(jnp.maximum(m1, m2), m3)
            e1, e2, e3 = jnp.exp2(m1 - m), jnp.exp2(m2 - m), jnp.exp2(m3 - m)
            num = (e1 * pv_nat[0, c, rows, :] + e2 * pv_nat[1, c, rows, :]
                   + e3 * padded_rows(pv_pad, c, t))
            den = (e1 * l_nat[0, c, rows, :] + e2 * l_nat[1, c, rows, :]
                   + e3 * padded_rows(l_pad, c, t))
            oa_ref[0, rows, c * LANES:(c + 1) * LANES] = (num * (1.0 / den)).astype(BF16)


def _merge_kernel(x_ref, oa_ref, ob_ref, oc_ref, wg_ref, bg_ref, wpa_ref, wpb_ref, wpc_ref,
                  wout_ref, lng_ref, lnb_ref, h_ref):
    x = x_ref[...]
    xb = x.astype(BF16)

    def gated(k, o_ref, wp_ref):
        cols = slice(k * D_MODEL, (k + 1) * D_MODEL)
        t = jnp.tanh(jnp.dot(xb, wg_ref[:, cols], preferred_element_type=F32) + bg_ref[:, cols])
        half_p = jnp.dot(o_ref[...], wp_ref[...], preferred_element_type=F32)
        return half_p + half_p * t

    merged = (gated(0, oa_ref, wpa_ref) + gated(1, ob_ref, wpb_ref)
              + gated(2, oc_ref, wpc_ref)).astype(BF16)
    for c in range(x_ref.shape[0] // TAIL_ROWS):
        rows = slice(c * TAIL_ROWS, (c + 1) * TAIL_ROWS)
        y = jnp.dot(merged[rows], wout_ref[...], preferred_element_type=F32)
        h_ref[rows, :] = _layer_norm(ALPHA * x_ref[rows, :] + y, lng_ref[...], lnb_ref[...])


def _ffn_kernel(h_ref, wup_ref, cw_ref, cb_ref, wdn_ref, lng_ref, lnb_ref, o_ref,
                a_buf, carry, g_buf, *, tiles_per_seq):
    tm = h_ref.shape[0]
    n_chunks = D_FF // FF_CHUNK

    @pl.when(pl.program_id(0) % tiles_per_seq == 0)
    def _():
        carry[...] = jnp.zeros(carry.shape, F32)

    h = h_ref[...]
    hb = h.astype(BF16)
    for c in range(n_chunks):
        cols = slice(c * FF_CHUNK, (c + 1) * FF_CHUNK)
        a = jnp.dot(hb, wup_ref[:, cols], preferred_element_type=F32)
        gate = jnp.dot(hb, wup_ref[:, D_FF + c * FF_CHUNK:D_FF + (c + 1) * FF_CHUNK],
                       preferred_element_type=F32)
        a_buf[:CARRY_ROWS] = carry[c]
        a_buf[CARRY_ROWS:] = a
        carry[c] = a[tm - CARRY_ROWS:]
        conv = (cw_ref[0:1, cols] * a_buf[CARRY_ROWS - 2:CARRY_ROWS - 2 + tm]
                + cw_ref[1:2, cols] * a_buf[CARRY_ROWS - 1:CARRY_ROWS - 1 + tm]
                + cw_ref[2:3, cols] * a + cb_ref[:, cols])
        g_buf[:, cols] = (_gelu(conv) * gate).astype(BF16)
    for c in range(tm // TAIL_ROWS):
        rows = slice(c * TAIL_ROWS, (c + 1) * TAIL_ROWS)
        y = jnp.dot(g_buf[rows, :], wdn_ref[...], preferred_element_type=F32)
        o_ref[rows, :] = _layer_norm(ALPHA * h_ref[rows, :] + y, lng_ref[...], lnb_ref[...])


def _t5_bucket(dist):
    max_exact = N_BUCKETS // 2
    n = jnp.maximum(dist, 1).astype(jnp.float32)
    large = max_exact + (jnp.log(n / max_exact) / math.log(MAX_DISTANCE / max_exact)
                         * (N_BUCKETS - max_exact)).astype(jnp.int32)
    large = jnp.minimum(large, N_BUCKETS - 1)
    return jnp.where(dist < max_exact, dist, large)


def _offset_bias(rel_bias):
    steps = WINDOW_STEPS - jnp.arange(2 * Q_BLOCK)
    rows = []
    for g, dil in enumerate(DILATIONS):
        bucket = _t5_bucket(jnp.maximum(steps, 0) * dil)
        bias = rel_bias[:, g * HEADS:(g + 1) * HEADS][bucket].astype(F32)
        rows.append(jnp.where(steps[:, None] >= 0, bias, NEG).T)
    return jnp.stack(rows, 0)[:, :, None, :]


def _side_cast_plan(rows, max_blocks):
    block = BF16_ROWS
    while rows % block or rows // block > max_blocks:
        block += BF16_ROWS
    return block, rows // block


def _cparams(*sem):
    return pltpu.CompilerParams(dimension_semantics=sem, vmem_limit_bytes=VMEM_LIMIT)


def _layer(x, mem, band_bias, hmask, w_in, w_mem_kv, sg_ln_g, sg_ln_b, w_spatial, b_spatial,
           w_proj_a, w_proj_b, w_proj_c, w_gate, b_gate, w_out, ln1_g, ln1_b,
           w_ffn_up, conv_w, conv_b, w_ffn_down, ln2_g, ln2_b):
    bsz, seq, _ = x.shape
    n_tok = bsz * seq
    tm = TOKEN_TILE
    assert seq % SUPER == 0 and SUPER % tm == 0 and tm % (SG_CHUNK * 2) == 0
    assert D_FF % FF_CHUNK == 0 and tm % (16 * DILATIONS[-1]) == 0
    assert seq % MERGE_TILE == 0 and seq % FFN_TILE == 0
    tiles_per_seq = seq // tm
    row = lambda v: v.reshape(1, -1)
    tok = lambda rows, width: pl.BlockSpec((rows, width), lambda i: (i, 0))
    tile3 = lambda rows, width: pl.BlockSpec((1, rows, width), lambda b, t: (b, t, 0))

    wsp = w_spatial.reshape(SG_W // SG_CHUNK, 2 * SG_CHUNK, SG_CHUNK)
    bsp = jnp.repeat(b_spatial.reshape(SG_W // SG_CHUNK, 2, SG_CHUNK).transpose(0, 2, 1),
                     HEAD_DIM, axis=-1)
    qkv_shapes = ([jax.ShapeDtypeStruct((bsz, seq // d, d * HW), BF16) for d in DILATIONS]
                  + [jax.ShapeDtypeStruct((bsz, seq // d, d * 2 * HW), BF16) for d in DILATIONS])
    qkv_specs = ([tile3(tm // d, d * HW) for d in DILATIONS]
                 + [tile3(tm // d, d * 2 * HW) for d in DILATIONS])
    side_w = ((w_gate, 0.5), (w_proj_a, 0.5), (w_proj_b, 0.5), (w_proj_c, 0.5), (w_out, 1.0),
              (w_ffn_up, 1.0), (w_ffn_down, 1.0))
    assert SIDE_CAST_BLOCKS <= bsz * tiles_per_seq
    plans = [_side_cast_plan(w.shape[0], SIDE_CAST_BLOCKS) for w, _ in side_w]

    def side_spec(w, plan):
        rows, n_blocks = plan
        return pl.BlockSpec((rows, w.shape[1]), lambda b, t: (
            jnp.minimum(b * tiles_per_seq + t, n_blocks - 1), 0))

    side_specs = [side_spec(w, plan) for (w, _), plan in zip(side_w, plans)]
    outs = pl.pallas_call(
        functools.partial(_proj_kernel,
                          side=tuple((scale, plan[1]) for (_, scale), plan in zip(side_w, plans))),
        grid=(bsz, tiles_per_seq),
        in_specs=[tile3(tm, D_MODEL), _resident((D_MODEL, IN_W)),
                  pl.BlockSpec((1, MEM_LEN, D_MODEL), lambda b, t: (b, 0, 0)),
                  _resident((D_MODEL, 2 * HW)), _resident((1, SG_W)), _resident((1, SG_W)),
                  _resident(wsp.shape), _resident(bsp.shape), _resident(hmask.shape)] + side_specs,
        out_specs=qkv_specs + [tile3(tm, SG_W), tile3(tm, HW)] + side_specs,
        out_shape=qkv_shapes + [jax.ShapeDtypeStruct((bsz, seq, SG_W), BF16),
                                jax.ShapeDtypeStruct((bsz, seq, HW), BF16)]
        + [jax.ShapeDtypeStruct(w.shape, BF16) for w, _ in side_w],
        scratch_shapes=[pltpu.VMEM((tm // PADDED_DIL * PAD_PITCH, LANES), F32),
                        pltpu.VMEM((MEM_LEN, 2 * HW), BF16)],
        compiler_params=_cparams("arbitrary", "arbitrary"),
        name="proj",
    )(x, w_in.astype(BF16), mem, w_mem_kv.astype(BF16), row(sg_ln_g), row(sg_ln_b), wsp, bsp,
      hmask, *[w for w, _ in side_w])
    q1, q2, q3, kv1, kv2, kv3, ob, oc = outs[:8]
    wg_b, wpa_b, wpb_b, wpc_b, wout_b, wup_b, wdn_b = outs[8:]

    n_super = seq // SUPER
    super3 = lambda d, width: pl.BlockSpec((1, SUPER // d, d * width), lambda b, j: (b, j, 0))

    def prev3(d):
        nblk = SUPER // d // Q_BLOCK
        return pl.BlockSpec((1, Q_BLOCK, d * 2 * HW),
                            lambda b, j: (b, jnp.maximum(j * nblk - 1, 0), 0))

    oa = pl.pallas_call(
        _attn_kernel,
        grid=(bsz, n_super),
        in_specs=[super3(d, HW) for d in DILATIONS] + [super3(d, 2 * HW) for d in DILATIONS]
        + [prev3(d) for d in DILATIONS] + [_resident(band_bias.shape), _resident(hmask.shape)],
        out_specs=pl.BlockSpec((1, SUPER, HW), lambda b, j: (b, j, 0)),
        out_shape=jax.ShapeDtypeStruct((bsz, seq, HW), BF16),
        scratch_shapes=[pltpu.VMEM((N_GROUPS - 1, HW // LANES, SUPER, LANES), F32)] * 3
        + [pltpu.VMEM((HW // LANES, SUPER // PADDED_DIL * PAD_PITCH, LANES), F32)] * 3,
        compiler_params=_cparams("parallel", "parallel"),
        name="attn",
    )(q1, q2, q3, kv1, kv2, kv3, kv1, kv2, kv3, band_bias, hmask)

    h1 = pl.pallas_call(
        _merge_kernel,
        grid=(n_tok // MERGE_TILE,),
        in_specs=[tok(MERGE_TILE, D_MODEL), tok(MERGE_TILE, HW), tok(MERGE_TILE, SG_W),
                  tok(MERGE_TILE, HW),
                  _resident((D_MODEL, 3 * D_MODEL)), _resident((1, 3 * D_MODEL)),
                  _resident((HW, D_MODEL)), _resident((SG_W, D_MODEL)), _resident((HW, D_MODEL)),
                  _resident((D_MODEL, D_MODEL)), _resident((1, D_MODEL)), _resident((1, D_MODEL))],
        out_specs=tok(MERGE_TILE, D_MODEL),
        out_shape=jax.ShapeDtypeStruct((n_tok, D_MODEL), F32),
        compiler_params=_cparams("parallel"),
        name="merge",
    )(x.reshape(n_tok, D_MODEL), oa.reshape(n_tok, HW), ob.reshape(n_tok, SG_W),
      oc.reshape(n_tok, HW), wg_b, row(0.5 * b_gate), wpa_b, wpb_b, wpc_b, wout_b,
      row(ln1_g), row(ln1_b))

    h2 = pl.pallas_call(
        functools.partial(_ffn_kernel, tiles_per_seq=seq // FFN_TILE),
        grid=(n_tok // FFN_TILE,),
        in_specs=[tok(FFN_TILE, D_MODEL), _resident((D_MODEL, 2 * D_FF)), _resident(conv_w.shape),
                  _resident((1, D_FF)), _resident((D_FF, D_MODEL)),
                  _resident((1, D_MODEL)), _resident((1, D_MODEL))],
        out_specs=tok(FFN_TILE, D_MODEL),
        out_shape=jax.ShapeDtypeStruct((n_tok, D_MODEL), F32),
        scratch_shapes=[pltpu.VMEM((CARRY_ROWS + FFN_TILE, FF_CHUNK), F32),
                        pltpu.VMEM((D_FF // FF_CHUNK, CARRY_ROWS, FF_CHUNK), F32),
                        pltpu.VMEM((FFN_TILE, D_FF), BF16)],
        compiler_params=_cparams("arbitrary"),
        name="ffn",
    )(h1, wup_b, conv_w, row(conv_b), wdn_b, row(ln2_g), row(ln2_b))
    return h2.reshape(bsz, seq, D_MODEL)


def kernel(x, mem, rel_bias, w_in, w_mem_kv, sg_ln_g, sg_ln_b, w_spatial, b_spatial, w_proj_a,
           w_proj_b, w_proj_c, w_gate, b_gate, w_out, ln1_g, ln1_b, w_ffn_up, conv_w, conv_b,
           w_ffn_down, ln2_g, ln2_b):
    band_bias = pl.pallas_call(
        _band_bias_kernel,
        grid=(N_GROUPS,),
        in_specs=[pl.BlockSpec((1, HEADS, 1, 2 * Q_BLOCK), lambda g: (g, 0, 0, 0))],
        out_specs=pl.BlockSpec((1, 2, HEADS * Q_BLOCK, 2 * Q_BLOCK), lambda g: (g, 0, 0, 0)),
        out_shape=jax.ShapeDtypeStruct((N_GROUPS, 2, HEADS * Q_BLOCK, 2 * Q_BLOCK), F32),
        name="band_bias",
    )(_offset_bias(rel_bias))
    lane = jnp.arange(HW)[None, None, :] // HEAD_DIM
    hmask = jnp.broadcast_to(lane == jnp.arange(HEADS)[:, None, None],
                             (HEADS, Q_BLOCK, HW)).astype(BF16)
    h = x
    for l in range(w_in.shape[0]):
        h = _layer(h, mem, band_bias, hmask, w_in[l], w_mem_kv[l], sg_ln_g[l], sg_ln_b[l],
                   w_spatial[l], b_spatial[l], w_proj_a[l], w_proj_b[l], w_proj_c[l], w_gate[l],
                   b_gate[l], w_out[l], ln1_g[l], ln1_b[l], w_ffn_up[l], conv_w[l], conv_b[l],
                   w_ffn_down[l], ln2_g[l], ln2_b[l])
    return h
```

```python
import functools
import math

import jax
import jax.numpy as jnp
from jax import lax
from jax.experimental import pallas as pl
from jax.experimental.pallas import tpu as pltpu

F32 = jnp.float32
BF16 = jnp.bfloat16

LANES = 128
BF16_ROWS = 16
D_MODEL = 1024
HEAD_DIM = 64
DILATIONS = (1, 4, 16)
N_GROUPS = 3
HEADS = 4
HW = HEADS * HEAD_DIM
ATTN_W = N_GROUPS * HW
WINDOW_STEPS = 128
Q_BLOCK = 128
SG_CHUNK = 128
SG_W = 512
MEM_LEN = 256
IN_W = 3 * ATTN_W + 2 * SG_W + HW
D_FF = 2816
N_BUCKETS = 32
MAX_DISTANCE = 2048
LN_EPS = 1e-5
DEPTH = 1
ALPHA = (2 * DEPTH) ** 0.25
ATTN_SCALE = HEAD_DIM ** -0.5
LOG2E = math.log2(math.e)
Q_SCALE = ATTN_SCALE * LOG2E
NEG = -1e30

TOKEN_TILE = 512
MERGE_TILE = 1024
FFN_TILE = 1024
SUPER = Q_BLOCK * DILATIONS[-1]
PADDED_DIL = DILATIONS[-1]
PAD_PITCH = 24
SIDE_CAST_BLOCKS = 8
TAIL_ROWS = 512
COMBINE_ROWS = 256
FF_CHUNK = 256
CARRY_ROWS = 8
VMEM_LIMIT = 56 * 1024 * 1024

_NT = (((1,), (1,)), ((), ()))


def _gelu(x):
    return 0.5 * x * (1.0 + jnp.tanh(math.sqrt(2.0 / math.pi) * (x + 0.044715 * (x * x * x))))


def _layer_norm(x, g, b):
    mu = jnp.mean(x, axis=-1, keepdims=True)
    xc = x - mu
    var = jnp.mean(xc * xc, axis=-1, keepdims=True)
    return xc * lax.rsqrt(var + LN_EPS) * g + b


def _resident(shape):
    zeros = (0,) * len(shape)
    return pl.BlockSpec(shape, lambda *_: zeros, pipeline_mode=pl.Buffered(1))


def _head_dense(per_head):
    low = lax.broadcasted_iota(jnp.int32, (per_head[0].shape[0], LANES), 1) < HEAD_DIM
    return jnp.concatenate([jnp.where(low, per_head[h], per_head[h + 1])
                            for h in range(0, HEADS, 2)], axis=1)


def _packed_heads_attention(q, k, v, hm_ref, bias_fn):
    m_rows = q.shape[0]
    qs = jnp.concatenate([q * hm_ref[h] for h in range(HEADS)], axis=0)
    s = lax.dot_general(qs, k, _NT, preferred_element_type=F32)
    ps, ms, ls = [], [], []
    for h in range(HEADS):
        sh = s[h * m_rows:(h + 1) * m_rows]
        bias = bias_fn(h)
        if bias is not None:
            sh = sh + bias
        m = jnp.max(sh, axis=-1, keepdims=True)
        p = jnp.exp2(sh - m)
        ps.append(p.astype(BF16))
        ms.append(m)
        ls.append(jnp.sum(p, axis=-1, keepdims=True))
    pv = jnp.dot(jnp.concatenate(ps, axis=0), v, preferred_element_type=F32)
    half = lambda h: slice((h // 2) * LANES, (h // 2 + 1) * LANES)
    pv = _head_dense([pv[h * m_rows:(h + 1) * m_rows, half(h)] for h in range(HEADS)])
    return pv, _head_dense(ms), _head_dense(ls)


def _band_bias_kernel(f_ref, o_ref):
    lane = lax.broadcasted_iota(jnp.int32, (Q_BLOCK, 2 * Q_BLOCK), 1)
    no_prev = jnp.where(lane < Q_BLOCK, NEG, 0.0)
    for h in range(HEADS):
        rep = jnp.broadcast_to(f_ref[0, h] * LOG2E, (Q_BLOCK, 2 * Q_BLOCK))
        band = pltpu.roll(rep, 0, 1, stride=1, stride_axis=0)
        o_ref[0, 0, h * Q_BLOCK:(h + 1) * Q_BLOCK] = band
        o_ref[0, 1, h * Q_BLOCK:(h + 1) * Q_BLOCK] = band + no_prev


def _proj_kernel(*refs, side):
    n_side = len(side)
    x_ref, win_ref, mem_ref, wkv_ref, lng_ref, lnb_ref, wsp_ref, bsp_ref, hm_ref = refs[:9]
    side_in = refs[9:9 + n_side]
    q1_ref, q2_ref, q3_ref, kv1_ref, kv2_ref, kv3_ref, ob_ref, oc_ref = refs[9 + n_side:17 + n_side]
    side_out = refs[17 + n_side:17 + 2 * n_side]
    de_buf, kvm = refs[17 + 2 * n_side:]
    xb = x_ref[0].astype(BF16)
    tm = xb.shape[0]

    step = pl.program_id(0) * pl.num_programs(1) + pl.program_id(1)
    for (scale, n_blocks), w_ref, o_ref in zip(side, side_in, side_out):
        @pl.when(step < n_blocks)
        def _(scale=scale, w_ref=w_ref, o_ref=o_ref):
            o_ref[...] = (w_ref[...] * scale).astype(BF16)

    @pl.when(pl.program_id(1) == 0)
    def _():
        kvm[...] = jnp.dot(mem_ref[0].astype(BF16), wkv_ref[...],
                           preferred_element_type=F32).astype(BF16)

    def proj(lo, width):
        return jnp.dot(xb, win_ref[:, lo:lo + width], preferred_element_type=F32)

    def put(ref, lane0, val, dil):
        if dil == 1:
            ref[0, :, lane0:lane0 + HW] = val.astype(BF16)
            return
        stride_cols = ref.shape[2] // dil
        pitch = PAD_PITCH if dil == PADDED_DIL else dil
        for c in range(HW // LANES):
            half = val[:, c * LANES:(c + 1) * LANES]
            if pitch == dil:
                de_buf[:tm] = half
            else:
                for i in range(tm // dil):
                    de_buf[i * pitch:i * pitch + dil] = half[i * dil:(i + 1) * dil]
            for r in range(dil):
                piece = de_buf[pl.ds(r, tm // dil, stride=pitch), :]
                col = r * stride_cols + lane0 + c * LANES
                ref[0, :, col:col + LANES] = piece.astype(BF16)

    def a_proj(g, part):
        q_ref, kv_ref = ((q1_ref, kv1_ref), (q2_ref, kv2_ref), (q3_ref, kv3_ref))[g]
        if part == 0:
            put(q_ref, 0, proj(g * HW, HW) * Q_SCALE, DILATIONS[g])
        else:
            put(kv_ref, (part - 1) * HW, proj(part * ATTN_W + g * HW, HW), DILATIONS[g])

    fillers = iter([(g, part) for g in reversed(range(N_GROUPS)) for part in range(3)])
    fill = lambda: a_proj(*next(fillers))

    def proj_halves(lo):
        return jnp.concatenate([proj(lo, HW), proj(lo + HW, HW)], axis=1)

    qc = (proj(3 * ATTN_W + 2 * SG_W, HW) * Q_SCALE).astype(BF16)
    u_pre = proj_halves(3 * ATTN_W)
    v_pre = proj_halves(3 * ATTN_W + SG_W)

    n_chunks = tm // SG_CHUNK
    u, vn = [], []
    for c in range(n_chunks):
        rows = slice(c * SG_CHUNK, (c + 1) * SG_CHUNK)
        fill()
        u.append(_gelu(u_pre[rows]))
        vn.append(_layer_norm(_gelu(v_pre[rows]), lng_ref[...], lnb_ref[...]).astype(BF16))
        pv, _, l = _packed_heads_attention(qc[rows], kvm[:, :HW], kvm[:, HW:], hm_ref,
                                           lambda h: None)
        oc_ref[0, rows, :] = (pv * (1.0 / l)).astype(BF16)
        fill()

    row = lax.broadcasted_iota(jnp.int32, (2 * SG_CHUNK, SG_CHUNK), 0) & (SG_CHUNK - 1)
    col = lax.broadcasted_iota(jnp.int32, (2 * SG_CHUNK, SG_CHUNK), 1)
    low_half = lax.broadcasted_iota(jnp.int32, (SG_CHUNK, SG_CHUNK), 1) < HEAD_DIM
    for j in range(SG_W // SG_CHUNK):
        cols = slice(j * SG_CHUNK, (j + 1) * SG_CHUNK)
        w_pair = jnp.where(row >= col, wsp_ref[j], 0.0).astype(BF16)
        rhs = jnp.concatenate([vn[c][:, cols] for c in range(n_chunks)], axis=1)
        r = jnp.dot(w_pair, rhs, preferred_element_type=F32)
        for c in range(n_chunks):
            rows = slice(c * SG_CHUNK, (c + 1) * SG_CHUNK)
            sv = jnp.where(low_half, r[:SG_CHUNK, rows], r[SG_CHUNK:, rows]) + bsp_ref[j]
            ob_ref[0, rows, cols] = (u[c][:, cols] * sv).astype(BF16)
        if j == 0:
            fill()


def _attn_kernel(q1_ref, q2_ref, q3_ref, kv1_ref, kv2_ref, kv3_ref, kp1_ref, kp2_ref, kp3_ref,
                 bias_ref, hm_ref, oa_ref, pv_nat, m_nat, l_nat, pv_pad, m_pad, l_pad):
    lead = (pl.program_id(1) == 0).astype(jnp.int32)

    groups = ((q1_ref, kv1_ref, kp1_ref), (q2_ref, kv2_ref, kp2_ref), (q3_ref, kv3_ref, kp3_ref))
    for g, (q_ref, kv_ref, kvp_ref) in enumerate(groups):
        dil = DILATIONS[g]
        nblk = SUPER // dil // Q_BLOCK
        for r in range(dil):
            kc = slice(r * 2 * HW, r * 2 * HW + HW)
            vc = slice(r * 2 * HW + HW, (r + 1) * 2 * HW)
            qc = slice(r * HW, (r + 1) * HW)
            for i in range(nblk):
                rq = slice(i * Q_BLOCK, (i + 1) * Q_BLOCK)
                if i == 0:
                    k2 = jnp.concatenate([kvp_ref[0, :, kc], kv_ref[0, rq, kc]], axis=0)
                    v2 = jnp.concatenate([kvp_ref[0, :, vc], kv_ref[0, rq, vc]], axis=0)
                else:
                    rk = slice((i - 1) * Q_BLOCK, (i + 1) * Q_BLOCK)
                    k2 = kv_ref[0, rk, kc]
                    v2 = kv_ref[0, rk, vc]
                table = lead if i == 0 else 0
                pv, m, l = _packed_heads_attention(
                    q_ref[0, rq, qc], k2, v2, hm_ref,
                    lambda h, table=table: bias_ref[g, table, h * Q_BLOCK:(h + 1) * Q_BLOCK, :])
                for c in range(HW // LANES):
                    lanes = slice(c * LANES, (c + 1) * LANES)
                    if dil == PADDED_DIL:
                        at = (c, pl.ds(i * Q_BLOCK * PAD_PITCH + r, Q_BLOCK, stride=PAD_PITCH))
                        dst = (pv_pad, m_pad, l_pad)
                    else:
                        nat = pl.ds(i * Q_BLOCK * dil + r, Q_BLOCK, stride=dil) if dil > 1 else rq
                        at = (g, c, nat)
                        dst = (pv_nat, m_nat, l_nat)
                    for ref, val in zip(dst, (pv, m, l)):
                        ref[at + (slice(None),)] = val[:, lanes]

    def padded_rows(ref, c, t):
        runs = COMBINE_ROWS // PADDED_DIL
        return jnp.concatenate([ref[c, pl.ds((t * runs + k) * PAD_PITCH, PADDED_DIL), :]
                                for k in range(runs)], axis=0)

    for t in range(SUPER // COMBINE_ROWS):
        rows = slice(t * COMBINE_ROWS, (t + 1) * COMBINE_ROWS)
        for c in range(HW // LANES):
            m1, m2, m3 = m_nat[0, c, rows, :], m_nat[1, c, rows, :], padded_rows(m_pad, c, t)
            m = jnp.maximum(jnp.maximum(m1, m2), m3)
            e1, e2, e3 = jnp.exp2(m1 - m), jnp.exp2(m2 - m), jnp.exp2(m3 - m)
            num = (e1 * pv_nat[0, c, rows, :] + e2 * pv_nat[1, c, rows, :]
                   + e3 * padded_rows(pv_pad, c, t))
            den = (e1 * l_nat[0, c, rows, :] + e2 * l_nat[1, c, rows, :]
                   + e3 * padded_rows(l_pad, c, t))
            oa_ref[0, rows, c * LANES:(c + 1) * LANES] = (num * (1.0 / den)).astype(BF16)


def _merge_kernel(x_ref, oa_ref, ob_ref, oc_ref, wg_ref, bg_ref, wpa_ref, wpb_ref, wpc_ref,
                  wout_ref, lng_ref, lnb_ref, h_ref):
    x = x_ref[...]
    xb = x.astype(BF16)

    def gated(k, o_ref, wp_ref):
        cols = slice(k * D_MODEL, (k + 1) * D_MODEL)
        t = jnp.tanh(jnp.dot(xb, wg_ref[:, cols], preferred_element_type=F32) + bg_ref[:, cols])
        half_p = jnp.dot(o_ref[...], wp_ref[...], preferred_element_type=F32)
        return half_p + half_p * t

    merged = (gated(0, oa_ref, wpa_ref) + gated(1, ob_ref, wpb_ref)
              + gated(2, oc_ref, wpc_ref)).astype(BF16)
    for c in range(x_ref.shape[0] // TAIL_ROWS):
        rows = slice(c * TAIL_ROWS, (c + 1) * TAIL_ROWS)
        y = jnp.dot(merged[rows], wout_ref[...], preferred_element_type=F32)
        h_ref[rows, :] = _layer_norm(ALPHA * x_ref[rows, :] + y, lng_ref[...], lnb_ref[...])


def _ffn_kernel(h_ref, wup_ref, cw_ref, cb_ref, wdn_ref, lng_ref, lnb_ref, o_ref,
                a_buf, carry, g_buf, *, tiles_per_seq):
    tm = h_ref.shape[0]
    n_chunks = D_FF // FF_CHUNK

    @pl.when(pl.program_id(0) % tiles_per_seq == 0)
    def _():
        carry[...] = jnp.zeros(carry.shape, F32)

    h = h_ref[...]
    hb = h.astype(BF16)
    for c in range(n_chunks):
        cols = slice(c * FF_CHUNK, (c + 1) * FF_CHUNK)
        a = jnp.dot(hb, wup_ref[:, cols], preferred_element_type=F32)
        gate = jnp.dot(hb, wup_ref[:, D_FF + c * FF_CHUNK:D_FF + (c + 1) * FF_CHUNK],
                       preferred_element_type=F32)
        a_buf[:CARRY_ROWS] = carry[c]
        a_buf[CARRY_ROWS:] = a
        carry[c] = a[tm - CARRY_ROWS:]
        conv = (cw_ref[0:1, cols] * a_buf[CARRY_ROWS - 2:CARRY_ROWS - 2 + tm]
                + cw_ref[1:2, cols] * a_buf[CARRY_ROWS - 1:CARRY_ROWS - 1 + tm]
                + cw_ref[2:3, cols] * a + cb_ref[:, cols])
        g_buf[:, cols] = (_gelu(conv) * gate).astype(BF16)
    for c in range(tm // TAIL_ROWS):
        rows = slice(c * TAIL_ROWS, (c + 1) * TAIL_ROWS)
        y = jnp.dot(g_buf[rows, :], wdn_ref[...], preferred_element_type=F32)
        o_ref[rows, :] = _layer_norm(ALPHA * h_ref[rows, :] + y, lng_ref[...], lnb_ref[...])


def _t5_bucket(dist):
    max_exact = N_BUCKETS // 2
    n = jnp.maximum(dist, 1).astype(jnp.float32)
    large = max_exact + (jnp.log(n / max_exact) / math.log(MAX_DISTANCE / max_exact)
                         * (N_BUCKETS - max_exact)).astype(jnp.int32)
    large = jnp.minimum(large, N_BUCKETS - 1)
    return jnp.where(dist < max_exact, dist, large)


def _offset_bias(rel_bias):
    steps = WINDOW_STEPS - jnp.arange(2 * Q_BLOCK)
    rows = []
    for g, dil in enumerate(DILATIONS):
        bucket = _t5_bucket(jnp.maximum(steps, 0) * dil)
        bias = rel_bias[:, g * HEADS:(g + 1) * HEADS][bucket].astype(F32)
        rows.append(jnp.where(steps[:, None] >= 0, bias, NEG).T)
    return jnp.stack(rows, 0)[:, :, None, :]


def _side_cast_plan(rows, max_blocks):
    block = BF16_ROWS
    while rows % block or rows // block > max_blocks:
        block += BF16_ROWS
    return block, rows // block


def _cparams(*sem):
    return pltpu.CompilerParams(dimension_semantics=sem, vmem_limit_bytes=VMEM_LIMIT)


def _layer(x, mem, band_bias, hmask, w_in, w_mem_kv, sg_ln_g, sg_ln_b, w_spatial, b_spatial,
           w_proj_a, w_proj_b, w_proj_c, w_gate, b_gate, w_out, ln1_g, ln1_b,
           w_ffn_up, conv_w, conv_b, w_ffn_down, ln2_g, ln2_b):
    bsz, seq, _ = x.shape
    n_tok = bsz * seq
    tm = TOKEN_TILE
    assert seq % SUPER == 0 and SUPER % tm == 0 and tm % (SG_CHUNK * 2) == 0
    assert D_FF % FF_CHUNK == 0 and tm % (16 * DILATIONS[-1]) == 0
    assert seq % MERGE_TILE == 0 and seq % FFN_TILE == 0
    tiles_per_seq = seq // tm
    row = lambda v: v.reshape(1, -1)
    tok = lambda rows, width: pl.BlockSpec((rows, width), lambda i: (i, 0))
    tile3 = lambda rows, width: pl.BlockSpec((1, rows, width), lambda b, t: (b, t, 0))

    wsp = w_spatial.reshape(SG_W // SG_CHUNK, 2 * SG_CHUNK, SG_CHUNK)
    bsp = jnp.repeat(b_spatial.reshape(SG_W // SG_CHUNK, 2, SG_CHUNK).transpose(0, 2, 1),
                     HEAD_DIM, axis=-1)
    qkv_shapes = ([jax.ShapeDtypeStruct((bsz, seq // d, d * HW), BF16) for d in DILATIONS]
                  + [jax.ShapeDtypeStruct((bsz, seq // d, d * 2 * HW), BF16) for d in DILATIONS])
    qkv_specs = ([tile3(tm // d, d * HW) for d in DILATIONS]
                 + [tile3(tm // d, d * 2 * HW) for d in DILATIONS])
    side_w = ((w_gate, 0.5), (w_proj_a, 0.5), (w_proj_b, 0.5), (w_proj_c, 0.5), (w_out, 1.0),
              (w_ffn_up, 1.0), (w_ffn_down, 1.0))
    assert SIDE_CAST_BLOCKS <= bsz * tiles_per_seq
    plans = [_side_cast_plan(w.shape[0], SIDE_CAST_BLOCKS) for w, _ in side_w]

    def side_spec(w, plan):
        rows, n_blocks = plan
        return pl.BlockSpec((rows, w.shape[1]), lambda b, t: (
            jnp.minimum(b * tiles_per_seq + t, n_blocks - 1), 0))

    side_specs = [side_spec(w, plan) for (w, _), plan in zip(side_w, plans)]
    outs = pl.pallas_call(
        functools.partial(_proj_kernel,
                          side=tuple((scale, plan[1]) for (_, scale), plan in zip(side_w, plans))),
        grid=(bsz, tiles_per_seq),
        in_specs=[tile3(tm, D_MODEL), _resident((D_MODEL, IN_W)),
                  pl.BlockSpec((1, MEM_LEN, D_MODEL), lambda b, t: (b, 0, 0)),
                  _resident((D_MODEL, 2 * HW)), _resident((1, SG_W)), _resident((1, SG_W)),
                  _resident(wsp.shape), _resident(bsp.shape), _resident(hmask.shape)] + side_specs,
        out_specs=qkv_specs + [tile3(tm, SG_W), tile3(tm, HW)] + side_specs,
        out_shape=qkv_shapes + [jax.ShapeDtypeStruct((bsz, seq, SG_W), BF16),
                                jax.ShapeDtypeStruct((bsz, seq, HW), BF16)]
        + [jax.ShapeDtypeStruct(w.shape, BF16) for w, _ in side_w],
        scratch_shapes=[pltpu.VMEM((tm // PADDED_DIL * PAD_PITCH, LANES), F32),
                        pltpu.VMEM((MEM_LEN, 2 * HW), BF16)],
        compiler_params=_cparams("arbitrary", "arbitrary"),
        name="proj",
    )(x, w_in.astype(BF16), mem, w_mem_kv.astype(BF16), row(sg_ln_g), row(sg_ln_b), wsp, bsp,
      hmask, *[w for w, _ in side_w])
    q1, q2, q3, kv1, kv2, kv3, ob, oc = outs[:8]
    wg_b, wpa_b, wpb_b, wpc_b, wout_b, wup_b, wdn_b = outs[8:]

    n_super = seq // SUPER
    super3 = lambda d, width: pl.BlockSpec((1, SUPER // d, d * width), lambda b, j: (b, j, 0))

    def prev3(d):
        nblk = SUPER // d // Q_BLOCK
        return pl.BlockSpec((1, Q_BLOCK, d * 2 * HW),
                            lambda b, j: (b, jnp.maximum(j * nblk - 1, 0), 0))

    oa = pl.pallas_call(
        _attn_kernel,
        grid=(bsz, n_super),
        in_specs=[super3(d, HW) for d in DILATIONS] + [super3(d, 2 * HW) for d in DILATIONS]
        + [prev3(d) for d in DILATIONS] + [_resident(band_bias.shape), _resident(hmask.shape)],
        out_specs=pl.BlockSpec((1, SUPER, HW), lambda b, j: (b, j, 0)),
        out_shape=jax.ShapeDtypeStruct((bsz, seq, HW), BF16),
        scratch_shapes=[pltpu.VMEM((N_GROUPS - 1, HW // LANES, SUPER, LANES), F32)] * 3
        + [pltpu.VMEM((HW // LANES, SUPER // PADDED_DIL * PAD_PITCH, LANES), F32)] * 3,
        compiler_params=_cparams("parallel", "parallel"),
        name="attn",
    )(q1, q2, q3, kv1, kv2, kv3, kv1, kv2, kv3, band_bias, hmask)

    h1 = pl.pallas_call(
        _merge_kernel,
        grid=(n_tok // MERGE_TILE,),
        in_specs=[tok(MERGE_TILE, D_MODEL), tok(MERGE_TILE, HW), tok(MERGE_TILE, SG_W),
                  tok(MERGE_TILE, HW),
                  _resident((D_MODEL, 3 * D_MODEL)), _resident((1, 3 * D_MODEL)),
                  _resident((HW, D_MODEL)), _resident((SG_W, D_MODEL)), _resident((HW, D_MODEL)),
                  _resident((D_MODEL, D_MODEL)), _resident((1, D_MODEL)), _resident((1, D_MODEL))],
        out_specs=tok(MERGE_TILE, D_MODEL),
        out_shape=jax.ShapeDtypeStruct((n_tok, D_MODEL), F32),
        compiler_params=_cparams("parallel"),
        name="merge",
    )(x.reshape(n_tok, D_MODEL), oa.reshape(n_tok, HW), ob.reshape(n_tok, SG_W),
      oc.reshape(n_tok, HW), wg_b, row(0.5 * b_gate), wpa_b, wpb_b, wpc_b, wout_b,
      row(ln1_g), row(ln1_b))

    h2 = pl.pallas_call(
        functools.partial(_ffn_kernel, tiles_per_seq=seq // FFN_TILE),
        grid=(n_tok // FFN_TILE,),
        in_specs=[tok(FFN_TILE, D_MODEL), _resident((D_MODEL, 2 * D_FF)), _resident(conv_w.shape),
                  _resident((1, D_FF)), _resident((D_FF, D_MODEL)),
                  _resident((1, D_MODEL)), _resident((1, D_MODEL))],
        out_specs=tok(FFN_TILE, D_MODEL),
        out_shape=jax.ShapeDtypeStruct((n_tok, D_MODEL), F32),
        scratch_shapes=[pltpu.VMEM((CARRY_ROWS + FFN_TILE, FF_CHUNK), F32),
                        pltpu.VMEM((D_FF // FF_CHUNK, CARRY_ROWS, FF_CHUNK), F32),
                        pltpu.VMEM((FFN_TILE, D_FF), BF16)],
        compiler_params=_cparams("arbitrary"),
        name="ffn",
    )(h1, wup_b, conv_w, row(conv_b), wdn_b, row(ln2_g), row(ln2_b))
    return h2.reshape(bsz, seq, D_MODEL)


def kernel(x, mem, rel_bias, w_in, w_mem_kv, sg_ln_g, sg_ln_b, w_spatial, b_spatial, w_proj_a,
           w_proj_b, w_proj_c, w_gate, b_gate, w_out, ln1_g, ln1_b, w_ffn_up, conv_w, conv_b,
           w_ffn_down, ln2_g, ln2_b):
    band_bias = pl.pallas_call(
        _band_bias_kernel,
        grid=(N_GROUPS,),
        in_specs=[pl.BlockSpec((1, HEADS, 1, 2 * Q_BLOCK), lambda g: (g, 0, 0, 0))],
        out_specs=pl.BlockSpec((1, 2, HEADS * Q_BLOCK, 2 * Q_BLOCK), lambda g: (g, 0, 0, 0)),
        out_shape=jax.ShapeDtypeStruct((N_GROUPS, 2, HEADS * Q_BLOCK, 2 * Q_BLOCK), F32),
        name="band_bias",
    )(_offset_bias(rel_bias))
    lane = jnp.arange(HW)[None, None, :] // HEAD_DIM
    hmask = jnp.broadcast_to(lane == jnp.arange(HEADS)[:, None, None],
                             (HEADS, Q_BLOCK, HW)).astype(BF16)
    h = x
    for l in range(w_in.shape[0]):
        h = _layer(h, mem, band_bias, hmask, w_in[l], w_mem_kv[l], sg_ln_g[l], sg_ln_b[l],
                   w_spatial[l], b_spatial[l], w_proj_a[l], w_proj_b[l], w_proj_c[l], w_gate[l],
                   b_gate[l], w_out[l], ln1_g[l], ln1_b[l], w_ffn_up[l], conv_w[l], conv_b[l],
                   w_ffn_down[l], ln2_g[l], ln2_b[l])
    return h
```

```python
import functools
import math

import jax
import jax.numpy as jnp
from jax import lax
from jax.experimental import pallas as pl
from jax.experimental.pallas import tpu as pltpu

F32 = jnp.float32
BF16 = jnp.bfloat16

LANES = 128
BF16_ROWS = 16
D_MODEL = 1024
HEAD_DIM = 64
DILATIONS = (1, 4, 16)
N_GROUPS = 3
HEADS = 4
HW = HEADS * HEAD_DIM
ATTN_W = N_GROUPS * HW
WINDOW_STEPS = 128
Q_BLOCK = 128
SG_CHUNK = 128
SG_W = 512
MEM_LEN = 256
IN_W = 3 * ATTN_W + 2 * SG_W + HW
D_FF = 2816
N_BUCKETS = 32
MAX_DISTANCE = 2048
LN_EPS = 1e-5
DEPTH = 1
ALPHA = (2 * DEPTH) ** 0.25
ATTN_SCALE = HEAD_DIM ** -0.5
LOG2E = math.log2(math.e)
Q_SCALE = ATTN_SCALE * LOG2E
NEG = -1e30

TOKEN_TILE = 512
MERGE_TILE = 1024
FFN_TILE = 1024
SUPER = Q_BLOCK * DILATIONS[-1]
PADDED_DIL = DILATIONS[-1]
PAD_PITCH = 24
SPLIT = 4
SIDE_CAST_BLOCKS = 8
TAIL_ROWS = 256
COMBINE_ROWS = 256
FF_CHUNK = 256
CARRY_ROWS = 8
VMEM_LIMIT = 56 * 1024 * 1024

_NT = (((1,), (1,)), ((), ()))


def _gelu(x):
    return 0.5 * x * (1.0 + jnp.tanh(math.sqrt(2.0 / math.pi) * (x + 0.044715 * (x * x * x))))


def _layer_norm(x, g, b):
    mu = jnp.mean(x, axis=-1, keepdims=True)
    xc = x - mu
    var = jnp.mean(xc * xc, axis=-1, keepdims=True)
    return xc * lax.rsqrt(var + LN_EPS) * g + b


def _resident(shape):
    zeros = (0,) * len(shape)
    return pl.BlockSpec(shape, lambda *_: zeros, pipeline_mode=pl.Buffered(1))


def _head_dense(per_head):
    low = lax.broadcasted_iota(jnp.int32, (per_head[0].shape[0], LANES), 1) < HEAD_DIM
    return jnp.concatenate([jnp.where(low, per_head[h], per_head[h + 1])
                            for h in range(0, HEADS, 2)], axis=1)


def _packed_heads_attention(q, k, v, hm_ref, bias_fn):
    m_rows = q.shape[0]
    qs = jnp.concatenate([q * hm_ref[h] for h in range(HEADS)], axis=0)
    s = lax.dot_general(qs, k, _NT, preferred_element_type=F32)
    ps, ms, ls = [], [], []
    for h in range(HEADS):
        sh = s[h * m_rows:(h + 1) * m_rows]
        bias = bias_fn(h)
        if bias is not None:
            sh = sh + bias
        m = jnp.max(sh, axis=-1, keepdims=True)
        p = jnp.exp2(sh - m)
        ps.append(p.astype(BF16))
        ms.append(m)
        ls.append(jnp.sum(p, axis=-1, keepdims=True))
    pv = jnp.dot(jnp.concatenate(ps, axis=0), v, preferred_element_type=F32)
    half = lambda h: slice((h // 2) * LANES, (h // 2 + 1) * LANES)
    pv = _head_dense([pv[h * m_rows:(h + 1) * m_rows, half(h)] for h in range(HEADS)])
    return pv, _head_dense(ms), _head_dense(ls)


def _band_bias_kernel(f_ref, o_ref):
    lane = lax.broadcasted_iota(jnp.int32, (Q_BLOCK, 2 * Q_BLOCK), 1)
    no_prev = jnp.where(lane < Q_BLOCK, NEG, 0.0)
    for h in range(HEADS):
        rep = jnp.broadcast_to(f_ref[0, h] * LOG2E, (Q_BLOCK, 2 * Q_BLOCK))
        band = pltpu.roll(rep, 0, 1, stride=1, stride_axis=0)
        o_ref[0, 0, h * Q_BLOCK:(h + 1) * Q_BLOCK] = band
        o_ref[0, 1, h * Q_BLOCK:(h + 1) * Q_BLOCK] = band + no_prev


def _proj_kernel(*refs, side):
    n_side = len(side)
    x_ref, win_ref, mem_ref, wkv_ref, lng_ref, lnb_ref, wsp_ref, bsp_ref, hm_ref = refs[:9]
    side_in = refs[9:9 + n_side]
    q1_ref, q2_ref, q3_ref, kv1_ref, kv2_ref, kv3_ref, ob_ref, oc_ref = refs[9 + n_side:17 + n_side]
    side_out = refs[17 + n_side:17 + 2 * n_side]
    de_buf, mid_buf, kvm = refs[17 + 2 * n_side:]
    xb = x_ref[0].astype(BF16)
    tm = xb.shape[0]

    step = pl.program_id(0) * pl.num_programs(1) + pl.program_id(1)
    for (scale, n_blocks), w_ref, o_ref in zip(side, side_in, side_out):
        @pl.when(step < n_blocks)
        def _(scale=scale, w_ref=w_ref, o_ref=o_ref):
            o_ref[...] = (w_ref[...] * scale).astype(BF16)

    @pl.when(pl.program_id(1) == 0)
    def _():
        kvm[...] = jnp.dot(mem_ref[0].astype(BF16), wkv_ref[...],
                           preferred_element_type=F32).astype(BF16)

    def proj(lo, width):
        return jnp.dot(xb, win_ref[:, lo:lo + width], preferred_element_type=F32)

    def put(ref, lane0, val, dil):
        if dil == 1:
            ref[0, :, lane0:lane0 + HW] = val.astype(BF16)
            return
        stride_cols = ref.shape[2] // dil
        for c in range(HW // LANES):
            de_buf[...] = val[:, c * LANES:(c + 1) * LANES]
            if dil == PADDED_DIL:
                for b in range(SPLIT):
                    mid_buf[b] = de_buf[pl.ds(b, tm // SPLIT, stride=SPLIT), :]
                pieces = [(SPLIT * a + b, mid_buf[b, pl.ds(a, tm // dil, stride=SPLIT), :])
                          for a in range(SPLIT) for b in range(SPLIT)]
            else:
                pieces = [(r, de_buf[pl.ds(r, tm // dil, stride=dil), :]) for r in range(dil)]
            for r, piece in pieces:
                col = r * stride_cols + lane0 + c * LANES
                ref[0, :, col:col + LANES] = piece.astype(BF16)

    def a_proj(g, part):
        q_ref, kv_ref = ((q1_ref, kv1_ref), (q2_ref, kv2_ref), (q3_ref, kv3_ref))[g]
        if part == 0:
            put(q_ref, 0, proj(g * HW, HW) * Q_SCALE, DILATIONS[g])
        else:
            put(kv_ref, (part - 1) * HW, proj(part * ATTN_W + g * HW, HW), DILATIONS[g])

    fillers = iter([(g, part) for g in reversed(range(N_GROUPS)) for part in range(3)])
    fill = lambda: a_proj(*next(fillers))

    def proj_halves(lo):
        return jnp.concatenate([proj(lo, HW), proj(lo + HW, HW)], axis=1)

    qc = (proj(3 * ATTN_W + 2 * SG_W, HW) * Q_SCALE).astype(BF16)
    u_pre = proj_halves(3 * ATTN_W)
    v_pre = proj_halves(3 * ATTN_W + SG_W)

    n_chunks = tm // SG_CHUNK
    u, vn = [], []
    for c in range(n_chunks):
        rows = slice(c * SG_CHUNK, (c + 1) * SG_CHUNK)
        fill()
        u.append(_gelu(u_pre[rows]))
        vn.append(_layer_norm(_gelu(v_pre[rows]), lng_ref[...], lnb_ref[...]).astype(BF16))
        pv, _, l = _packed_heads_attention(qc[rows], kvm[:, :HW], kvm[:, HW:], hm_ref,
                                           lambda h: None)
        oc_ref[0, rows, :] = (pv * (1.0 / l)).astype(BF16)
        fill()

    row = lax.broadcasted_iota(jnp.int32, (2 * SG_CHUNK, SG_CHUNK), 0) & (SG_CHUNK - 1)
    col = lax.broadcasted_iota(jnp.int32, (2 * SG_CHUNK, SG_CHUNK), 1)
    low_half = lax.broadcasted_iota(jnp.int32, (SG_CHUNK, SG_CHUNK), 1) < HEAD_DIM
    for j in range(SG_W // SG_CHUNK):
        cols = slice(j * SG_CHUNK, (j + 1) * SG_CHUNK)
        w_pair = jnp.where(row >= col, wsp_ref[j], 0.0).astype(BF16)
        rhs = jnp.concatenate([vn[c][:, cols] for c in range(n_chunks)], axis=1)
        r = jnp.dot(w_pair, rhs, preferred_element_type=F32)
        for c in range(n_chunks):
            rows = slice(c * SG_CHUNK, (c + 1) * SG_CHUNK)
            sv = jnp.where(low_half, r[:SG_CHUNK, rows], r[SG_CHUNK:, rows]) + bsp_ref[j]
            ob_ref[0, rows, cols] = (u[c][:, cols] * sv).astype(BF16)
        if j == 0:
            fill()


def _attn_kernel(q1_ref, q2_ref, q3_ref, kv1_ref, kv2_ref, kv3_ref, kp1_ref, kp2_ref, kp3_ref,
                 bias_ref, hm_ref, oa_ref, pv_nat, m_nat, l_nat, pv_pad, m_pad, l_pad):
    lead = (pl.program_id(1) == 0).astype(jnp.int32)

    groups = ((q1_ref, kv1_ref, kp1_ref), (q2_ref, kv2_ref, kp2_ref), (q3_ref, kv3_ref, kp3_ref))
    for g, (q_ref, kv_ref, kvp_ref) in enumerate(groups):
        dil = DILATIONS[g]
        nblk = SUPER // dil // Q_BLOCK
        for r in range(dil):
            kc = slice(r * 2 * HW, r * 2 * HW + HW)
            vc = slice(r * 2 * HW + HW, (r + 1) * 2 * HW)
            qc = slice(r * HW, (r + 1) * HW)
            for i in range(nblk):
                rq = slice(i * Q_BLOCK, (i + 1) * Q_BLOCK)
                if i == 0:
                    k2 = jnp.concatenate([kvp_ref[0, :, kc], kv_ref[0, rq, kc]], axis=0)
                    v2 = jnp.concatenate([kvp_ref[0, :, vc], kv_ref[0, rq, vc]], axis=0)
                else:
                    rk = slice((i - 1) * Q_BLOCK, (i + 1) * Q_BLOCK)
                    k2 = kv_ref[0, rk, kc]
                    v2 = kv_ref[0, rk, vc]
                table = lead if i == 0 else 0
                pv, m, l = _packed_heads_attention(
                    q_ref[0, rq, qc], k2, v2, hm_ref,
                    lambda h, table=table: bias_ref[g, table, h * Q_BLOCK:(h + 1) * Q_BLOCK, :])
                for c in range(HW // LANES):
                    lanes = slice(c * LANES, (c + 1) * LANES)
                    if dil == PADDED_DIL:
                        at = (c, pl.ds(i * Q_BLOCK * PAD_PITCH + r, Q_BLOCK, stride=PAD_PITCH))
                        dst = (pv_pad, m_pad, l_pad)
                    else:
                        nat = pl.ds(i * Q_BLOCK * dil + r, Q_BLOCK, stride=dil) if dil > 1 else rq
                        at = (g, c, nat)
                        dst = (pv_nat, m_nat, l_nat)
                    for ref, val in zip(dst, (pv, m, l)):
                        ref[at + (slice(None),)] = val[:, lanes]

    def padded_rows(ref, c, t):
        runs = COMBINE_ROWS // PADDED_DIL
        return jnp.concatenate([ref[c, pl.ds((t * runs + k) * PAD_PITCH, PADDED_DIL), :]
                                for k in range(runs)], axis=0)

    for t in range(SUPER // COMBINE_ROWS):
        rows = slice(t * COMBINE_ROWS, (t + 1) * COMBINE_ROWS)
        for c in range(HW // LANES):
            m1, m2, m3 = m_nat[0, c, rows, :], m_nat[1, c, rows, :], padded_rows(m_pad, c, t)
            m = jnp.maximum(jnp.maximum(m1, m2), m3)
            e1, e2, e3 = jnp.exp2(m1 - m), jnp.exp2(m2 - m), jnp.exp2(m3 - m)
            num = (e1 * pv_nat[0, c, rows, :] + e2 * pv_nat[1, c, rows, :]
                   + e3 * padded_rows(pv_pad, c, t))
            den = (e1 * l_nat[0, c, rows, :] + e2 * l_nat[1, c, rows, :]
                   + e3 * padded_rows(l_pad, c, t))
            oa_ref[0, rows, c * LANES:(c + 1) * LANES] = (num * (1.0 / den)).astype(BF16)


def _merge_kernel(x_ref, oa_ref, ob_ref, oc_ref, wg_ref, bg_ref, wpa_ref, wpb_ref, wpc_ref,
                  wout_ref, lng_ref, lnb_ref, h_ref):
    x = x_ref[...]
    xb = x.astype(BF16)

    def gated(k, o_ref, wp_ref):
        cols = slice(k * D_MODEL, (k + 1) * D_MODEL)
        t = jnp.tanh(jnp.dot(xb, wg_ref[:, cols], preferred_element_type=F32) + bg_ref[:, cols])
        half_p = jnp.dot(o_ref[...], wp_ref[...], preferred_element_type=F32)
        return half_p + half_p * t

    merged = (gated(0, oa_ref, wpa_ref) + gated(1, ob_ref, wpb_ref)
              + gated(2, oc_ref, wpc_ref)).astype(BF16)
    for c in range(x_ref.shape[0] // TAIL_ROWS):
        rows = slice(c * TAIL_ROWS, (c + 1) * TAIL_ROWS)
        y = jnp.dot(merged[rows], wout_ref[...], preferred_element_type=F32)
        h_ref[rows, :] = _layer_norm(ALPHA * x_ref[rows, :] + y, lng_ref[...], lnb_ref[...])


def _ffn_kernel(h_ref, wup_ref, cw_ref, cb_ref, wdn_ref, lng_ref, lnb_ref, o_ref,
                a_buf, carry, g_buf, *, tiles_per_seq):
    tm = h_ref.shape[0]
    n_chunks = D_FF // FF_CHUNK

    @pl.when(pl.program_id(0) % tiles_per_seq == 0)
    def _():
        carry[...] = jnp.zeros(carry.shape, F32)

    h = h_ref[...]
    hb = h.astype(BF16)
    for c in range(n_chunks):
        cols = slice(c * FF_CHUNK, (c + 1) * FF_CHUNK)
        a = jnp.dot(hb, wup_ref[:, cols], preferred_element_type=F32)
        gate = jnp.dot(hb, wup_ref[:, D_FF + c * FF_CHUNK:D_FF + (c + 1) * FF_CHUNK],
                       preferred_element_type=F32)
        a_buf[:CARRY_ROWS] = carry[c]
        a_buf[CARRY_ROWS:] = a
        carry[c] = a[tm - CARRY_ROWS:]
        conv = (cw_ref[0:1, cols] * a_buf[CARRY_ROWS - 2:CARRY_ROWS - 2 + tm]
                + cw_ref[1:2, cols] * a_buf[CARRY_ROWS - 1:CARRY_ROWS - 1 + tm]
                + cw_ref[2:3, cols] * a + cb_ref[:, cols])
        g_buf[:, cols] = (_gelu(conv) * gate).astype(BF16)
    for c in range(tm // TAIL_ROWS):
        rows = slice(c * TAIL_ROWS, (c + 1) * TAIL_ROWS)
        y = jnp.dot(g_buf[rows, :], wdn_ref[...], preferred_element_type=F32)
        o_ref[rows, :] = _layer_norm(ALPHA * h_ref[rows, :] + y, lng_ref[...], lnb_ref[...])


def _t5_bucket(dist):
    max_exact = N_BUCKETS // 2
    n = jnp.maximum(dist, 1).astype(jnp.float32)
    large = max_exact + (jnp.log(n / max_exact) / math.log(MAX_DISTANCE / max_exact)
                         * (N_BUCKETS - max_exact)).astype(jnp.int32)
    large = jnp.minimum(large, N_BUCKETS - 1)
    return jnp.where(dist < max_exact, dist, large)


def _offset_bias(rel_bias):
    steps = WINDOW_STEPS - jnp.arange(2 * Q_BLOCK)
    rows = []
    for g, dil in enumerate(DILATIONS):
        bucket = _t5_bucket(jnp.maximum(steps, 0) * dil)
        bias = rel_bias[:, g * HEADS:(g + 1) * HEADS][bucket].astype(F32)
        rows.append(jnp.where(steps[:, None] >= 0, bias, NEG).T)
    return jnp.stack(rows, 0)[:, :, None, :]


def _side_cast_plan(rows, max_blocks):
    block = BF16_ROWS
    while rows % block or rows // block > max_blocks:
        block += BF16_ROWS
    return block, rows // block


def _cparams(*sem):
    return pltpu.CompilerParams(dimension_semantics=sem, vmem_limit_bytes=VMEM_LIMIT)


def _layer(x, mem, band_bias, hmask, w_in, w_mem_kv, sg_ln_g, sg_ln_b, w_spatial, b_spatial,
           w_proj_a, w_proj_b, w_proj_c, w_gate, b_gate, w_out, ln1_g, ln1_b,
           w_ffn_up, conv_w, conv_b, w_ffn_down, ln2_g, ln2_b):
    bsz, seq, _ = x.shape
    n_tok = bsz * seq
    tm = TOKEN_TILE
    assert seq % SUPER == 0 and SUPER % tm == 0 and tm % (SG_CHUNK * 2) == 0
    assert D_FF % FF_CHUNK == 0 and tm % (16 * DILATIONS[-1]) == 0
    assert seq % MERGE_TILE == 0 and seq % FFN_TILE == 0 and SPLIT * SPLIT == PADDED_DIL
    tiles_per_seq = seq // tm
    row = lambda v: v.reshape(1, -1)
    tok = lambda rows, width: pl.BlockSpec((rows, width), lambda i: (i, 0))
    tile3 = lambda rows, width: pl.BlockSpec((1, rows, width), lambda b, t: (b, t, 0))

    wsp = w_spatial.reshape(SG_W // SG_CHUNK, 2 * SG_CHUNK, SG_CHUNK)
    bsp = jnp.repeat(b_spatial.reshape(SG_W // SG_CHUNK, 2, SG_CHUNK).transpose(0, 2, 1),
                     HEAD_DIM, axis=-1)
    qkv_shapes = ([jax.ShapeDtypeStruct((bsz, seq // d, d * HW), BF16) for d in DILATIONS]
                  + [jax.ShapeDtypeStruct((bsz, seq // d, d * 2 * HW), BF16) for d in DILATIONS])
    qkv_specs = ([tile3(tm // d, d * HW) for d in DILATIONS]
                 + [tile3(tm // d, d * 2 * HW) for d in DILATIONS])
    side_w = ((w_gate, 0.5), (w_proj_a, 0.5), (w_proj_b, 0.5), (w_proj_c, 0.5), (w_out, 1.0),
              (w_ffn_up, 1.0), (w_ffn_down, 1.0))
    assert SIDE_CAST_BLOCKS <= bsz * tiles_per_seq
    plans = [_side_cast_plan(w.shape[0], SIDE_CAST_BLOCKS) for w, _ in side_w]

    def side_spec(w, plan):
        rows, n_blocks = plan
        return pl.BlockSpec((rows, w.shape[1]), lambda b, t: (
            jnp.minimum(b * tiles_per_seq + t, n_blocks - 1), 0))

    side_specs = [side_spec(w, plan) for (w, _), plan in zip(side_w, plans)]
    outs = pl.pallas_call(
        functools.partial(_proj_kernel,
                          side=tuple((scale, plan[1]) for (_, scale), plan in zip(side_w, plans))),
        grid=(bsz, tiles_per_seq),
        in_specs=[tile3(tm, D_MODEL), _resident((D_MODEL, IN_W)),
                  pl.BlockSpec((1, MEM_LEN, D_MODEL), lambda b, t: (b, 0, 0)),
                  _resident((D_MODEL, 2 * HW)), _resident((1, SG_W)), _resident((1, SG_W)),
                  _resident(wsp.shape), _resident(bsp.shape), _resident(hmask.shape)] + side_specs,
        out_specs=qkv_specs + [tile3(tm, SG_W), tile3(tm, HW)] + side_specs,
        out_shape=qkv_shapes + [jax.ShapeDtypeStruct((bsz, seq, SG_W), BF16),
                                jax.ShapeDtypeStruct((bsz, seq, HW), BF16)]
        + [jax.ShapeDtypeStruct(w.shape, BF16) for w, _ in side_w],
        scratch_shapes=[pltpu.VMEM((tm, LANES), F32), pltpu.VMEM((SPLIT, tm // SPLIT, LANES), F32),
                        pltpu.VMEM((MEM_LEN, 2 * HW), BF16)],
        compiler_params=_cparams("arbitrary", "arbitrary"),
        name="proj",
    )(x, w_in.astype(BF16), mem, w_mem_kv.astype(BF16), row(sg_ln_g), row(sg_ln_b), wsp, bsp,
      hmask, *[w for w, _ in side_w])
    q1, q2, q3, kv1, kv2, kv3, ob, oc = outs[:8]
    wg_b, wpa_b, wpb_b, wpc_b, wout_b, wup_b, wdn_b = outs[8:]

    n_super = seq // SUPER
    super3 = lambda d, width: pl.BlockSpec((1, SUPER // d, d * width), lambda b, j: (b, j, 0))

    def prev3(d):
        nblk = SUPER // d // Q_BLOCK
        return pl.BlockSpec((1, Q_BLOCK, d * 2 * HW),
                            lambda b, j: (b, jnp.maximum(j * nblk - 1, 0), 0))

    oa = pl.pallas_call(
        _attn_kernel,
        grid=(bsz, n_super),
        in_specs=[super3(d, HW) for d in DILATIONS] + [super3(d, 2 * HW) for d in DILATIONS]
        + [prev3(d) for d in DILATIONS] + [_resident(band_bias.shape), _resident(hmask.shape)],
        out_specs=pl.BlockSpec((1, SUPER, HW), lambda b, j: (b, j, 0)),
        out_shape=jax.ShapeDtypeStruct((bsz, seq, HW), BF16),
        scratch_shapes=[pltpu.VMEM((N_GROUPS - 1, HW // LANES, SUPER, LANES), F32)] * 3
        + [pltpu.VMEM((HW // LANES, SUPER // PADDED_DIL * PAD_PITCH, LANES), F32)] * 3,
        compiler_params=_cparams("parallel", "parallel"),
        name="attn",
    )(q1, q2, q3, kv1, kv2, kv3, kv1, kv2, kv3, band_bias, hmask)

    h1 = pl.pallas_call(
        _merge_kernel,
        grid=(n_tok // MERGE_TILE,),
        in_specs=[tok(MERGE_TILE, D_MODEL), tok(MERGE_TILE, HW), tok(MERGE_TILE, SG_W),
                  tok(MERGE_TILE, HW),
                  _resident((D_MODEL, 3 * D_MODEL)), _resident((1, 3 * D_MODEL)),
                  _resident((HW, D_MODEL)), _resident((SG_W, D_MODEL)), _resident((HW, D_MODEL)),
                  _resident((D_MODEL, D_MODEL)), _resident((1, D_MODEL)), _resident((1, D_MODEL))],
        out_specs=tok(MERGE_TILE, D_MODEL),
        out_shape=jax.ShapeDtypeStruct((n_tok, D_MODEL), F32),
        compiler_params=_cparams("parallel"),
        name="merge",
    )(x.reshape(n_tok, D_MODEL), oa.reshape(n_tok, HW), ob.reshape(n_tok, SG_W),
      oc.reshape(n_tok, HW), wg_b, row(0.5 * b_gate), wpa_b, wpb_b, wpc_b, wout_b,
      row(ln1_g), row(ln1_b))

    h2 = pl.pallas_call(
        functools.partial(_ffn_kernel, tiles_per_seq=seq // FFN_TILE),
        grid=(n_tok // FFN_TILE,),
        in_specs=[tok(FFN_TILE, D_MODEL), _resident((D_MODEL, 2 * D_FF)), _resident(conv_w.shape),
                  _resident((1, D_FF)), _resident((D_FF, D_MODEL)),
                  _resident((1, D_MODEL)), _resident((1, D_MODEL))],
        out_specs=tok(FFN_TILE, D_MODEL),
        out_shape=jax.ShapeDtypeStruct((n_tok, D_MODEL), F32),
        scratch_shapes=[pltpu.VMEM((CARRY_ROWS + FFN_TILE, FF_CHUNK), F32),
                        pltpu.VMEM((D_FF // FF_CHUNK, CARRY_ROWS, FF_CHUNK), F32),
                        pltpu.VMEM((FFN_TILE, D_FF), BF16)],
        compiler_params=_cparams("arbitrary"),
        name="ffn",
    )(h1, wup_b, conv_w, row(conv_b), wdn_b, row(ln2_g), row(ln2_b))
    return h2.reshape(bsz, seq, D_MODEL)


def kernel(x, mem, rel_bias, w_in, w_mem_kv, sg_ln_g, sg_ln_b, w_spatial, b_spatial, w_proj_a,
           w_proj_b, w_proj_c, w_gate, b_gate, w_out, ln1_g, ln1_b, w_ffn_up, conv_w, conv_b,
           w_ffn_down, ln2_g, ln2_b):
    band_bias = pl.pallas_call(
        _band_bias_kernel,
        grid=(N_GROUPS,),
        in_specs=[pl.BlockSpec((1, HEADS, 1, 2 * Q_BLOCK), lambda g: (g, 0, 0, 0))],
        out_specs=pl.BlockSpec((1, 2, HEADS * Q_BLOCK, 2 * Q_BLOCK), lambda g: (g, 0, 0, 0)),
        out_shape=jax.ShapeDtypeStruct((N_GROUPS, 2, HEADS * Q_BLOCK, 2 * Q_BLOCK), F32),
        name="band_bias",
    )(_offset_bias(rel_bias))
    lane = jnp.arange(HW)[None, None, :] // HEAD_DIM
    hmask = jnp.broadcast_to(lane == jnp.arange(HEADS)[:, None, None],
                             (HEADS, Q_BLOCK, HW)).astype(BF16)
    h = x
    for l in range(w_in.shape[0]):
        h = _layer(h, mem, band_bias, hmask, w_in[l], w_mem_kv[l], sg_ln_g[l], sg_ln_b[l],
                   w_spatial[l], b_spatial[l], w_proj_a[l], w_proj_b[l], w_proj_c[l], w_gate[l],
                   b_gate[l], w_out[l], ln1_g[l], ln1_b[l], w_ffn_up[l], conv_w[l], conv_b[l],
                   w_ffn_down[l], ln2_g[l], ln2_b[l])
    return h
```

```python
import functools
import math

import jax
import jax.numpy as jnp
from jax import lax
from jax.experimental import pallas as pl
from jax.experimental.pallas import tpu as pltpu

F32 = jnp.float32
BF16 = jnp.bfloat16

LANES = 128
BF16_ROWS = 16
D_MODEL = 1024
HEAD_DIM = 64
DILATIONS = (1, 4, 16)
N_GROUPS = 3
HEADS = 4
HW = HEADS * HEAD_DIM
ATTN_W = N_GROUPS * HW
WINDOW_STEPS = 128
Q_BLOCK = 128
SG_CHUNK = 128
SG_W = 512
MEM_LEN = 256
IN_W = 3 * ATTN_W + 2 * SG_W + HW
D_FF = 2816
N_BUCKETS = 32
MAX_DISTANCE = 2048
LN_EPS = 1e-5
DEPTH = 1
ALPHA = (2 * DEPTH) ** 0.25
ATTN_SCALE = HEAD_DIM ** -0.5
LOG2E = math.log2(math.e)
Q_SCALE = ATTN_SCALE * LOG2E
NEG = -1e30

TOKEN_TILE = 512
MERGE_TILE = 1024
FFN_TILE = 1024
SUPER = Q_BLOCK * DILATIONS[-1]
PADDED_DIL = DILATIONS[-1]
PAD_PITCH = 24
SIDE_CAST_BLOCKS = 8
GATE_COLS = 512
TAIL_ROWS = 256
COMBINE_ROWS = 256
FF_CHUNK = 256
CARRY_ROWS = 8
VMEM_LIMIT = 56 * 1024 * 1024

_NT = (((1,), (1,)), ((), ()))


def _gelu(x):
    return 0.5 * x * (1.0 + jnp.tanh(math.sqrt(2.0 / math.pi) * (x + 0.044715 * (x * x * x))))


def _layer_norm(x, g, b):
    mu = jnp.mean(x, axis=-1, keepdims=True)
    xc = x - mu
    var = jnp.mean(xc * xc, axis=-1, keepdims=True)
    return xc * lax.rsqrt(var + LN_EPS) * g + b


def _resident(shape):
    zeros = (0,) * len(shape)
    return pl.BlockSpec(shape, lambda *_: zeros, pipeline_mode=pl.Buffered(1))


def _head_dense(per_head):
    low = lax.broadcasted_iota(jnp.int32, (per_head[0].shape[0], LANES), 1) < HEAD_DIM
    return jnp.concatenate([jnp.where(low, per_head[h], per_head[h + 1])
                            for h in range(0, HEADS, 2)], axis=1)


def _packed_heads_attention(q, k, v, hm_ref, bias_fn):
    m_rows = q.shape[0]
    qs = jnp.concatenate([q * hm_ref[h] for h in range(HEADS)], axis=0)
    s = lax.dot_general(qs, k, _NT, preferred_element_type=F32)
    ps, ms, ls = [], [], []
    for h in range(HEADS):
        sh = s[h * m_rows:(h + 1) * m_rows]
        bias = bias_fn(h)
        if bias is not None:
            sh = sh + bias
        m = jnp.max(sh, axis=-1, keepdims=True)
        p = jnp.exp2(sh - m)
        ps.append(p.astype(BF16))
        ms.append(m)
        ls.append(jnp.sum(p, axis=-1, keepdims=True))
    pv = jnp.dot(jnp.concatenate(ps, axis=0), v, preferred_element_type=F32)
    half = lambda h: slice((h // 2) * LANES, (h // 2 + 1) * LANES)
    pv = _head_dense([pv[h * m_rows:(h + 1) * m_rows, half(h)] for h in range(HEADS)])
    return pv, _head_dense(ms), _head_dense(ls)


def _band_bias_kernel(f_ref, o_ref):
    lane = lax.broadcasted_iota(jnp.int32, (Q_BLOCK, 2 * Q_BLOCK), 1)
    no_prev = jnp.where(lane < Q_BLOCK, NEG, 0.0)
    for h in range(HEADS):
        rep = jnp.broadcast_to(f_ref[0, h] * LOG2E, (Q_BLOCK, 2 * Q_BLOCK))
        band = pltpu.roll(rep, 0, 1, stride=1, stride_axis=0)
        o_ref[0, 0, h * Q_BLOCK:(h + 1) * Q_BLOCK] = band
        o_ref[0, 1, h * Q_BLOCK:(h + 1) * Q_BLOCK] = band + no_prev


def _proj_kernel(*refs, side):
    n_side = len(side)
    x_ref, win_ref, mem_ref, wkv_ref, lng_ref, lnb_ref, wsp_ref, bsp_ref, hm_ref = refs[:9]
    side_in = refs[9:9 + n_side]
    q1_ref, q2_ref, q3_ref, kv1_ref, kv2_ref, kv3_ref, ob_ref, oc_ref = refs[9 + n_side:17 + n_side]
    side_out = refs[17 + n_side:17 + 2 * n_side]
    de_buf, kvm = refs[17 + 2 * n_side:]
    xb = x_ref[0].astype(BF16)
    tm = xb.shape[0]

    step = pl.program_id(0) * pl.num_programs(1) + pl.program_id(1)
    for (scale, n_blocks), w_ref, o_ref in zip(side, side_in, side_out):
        @pl.when(step < n_blocks)
        def _(scale=scale, w_ref=w_ref, o_ref=o_ref):
            o_ref[...] = (w_ref[...] * scale).astype(BF16)

    @pl.when(pl.program_id(1) == 0)
    def _():
        kvm[...] = jnp.dot(mem_ref[0].astype(BF16), wkv_ref[...],
                           preferred_element_type=F32).astype(BF16)

    def proj(lo, width):
        return jnp.dot(xb, win_ref[:, lo:lo + width], preferred_element_type=F32)

    def put(ref, lane0, val, dil):
        if dil == 1:
            ref[0, :, lane0:lane0 + HW] = val.astype(BF16)
            return
        stride_cols = ref.shape[2] // dil
        pitch = PAD_PITCH if dil == PADDED_DIL else dil
        for c in range(HW // LANES):
            half = val[:, c * LANES:(c + 1) * LANES]
            if pitch == dil:
                de_buf[:tm] = half
            else:
                for i in range(tm // dil):
                    de_buf[i * pitch:i * pitch + dil] = half[i * dil:(i + 1) * dil]
            for r in range(dil):
                piece = de_buf[pl.ds(r, tm // dil, stride=pitch), :]
                col = r * stride_cols + lane0 + c * LANES
                ref[0, :, col:col + LANES] = piece.astype(BF16)

    def a_proj(g, part):
        q_ref, kv_ref = ((q1_ref, kv1_ref), (q2_ref, kv2_ref), (q3_ref, kv3_ref))[g]
        if part == 0:
            put(q_ref, 0, proj(g * HW, HW) * Q_SCALE, DILATIONS[g])
        else:
            put(kv_ref, (part - 1) * HW, proj(part * ATTN_W + g * HW, HW), DILATIONS[g])

    fillers = iter([(g, part) for g in reversed(range(N_GROUPS)) for part in range(3)])
    fill = lambda: a_proj(*next(fillers))

    def proj_halves(lo):
        return jnp.concatenate([proj(lo, HW), proj(lo + HW, HW)], axis=1)

    qc = (proj(3 * ATTN_W + 2 * SG_W, HW) * Q_SCALE).astype(BF16)
    u_pre = proj_halves(3 * ATTN_W)
    v_pre = proj_halves(3 * ATTN_W + SG_W)

    n_chunks = tm // SG_CHUNK
    u, vn = [], []
    for c in range(n_chunks):
        rows = slice(c * SG_CHUNK, (c + 1) * SG_CHUNK)
        fill()
        u.append(_gelu(u_pre[rows]))
        vn.append(_layer_norm(_gelu(v_pre[rows]), lng_ref[...], lnb_ref[...]).astype(BF16))
        pv, _, l = _packed_heads_attention(qc[rows], kvm[:, :HW], kvm[:, HW:], hm_ref,
                                           lambda h: None)
        oc_ref[0, rows, :] = (pv * (1.0 / l)).astype(BF16)
        fill()

    row = lax.broadcasted_iota(jnp.int32, (2 * SG_CHUNK, SG_CHUNK), 0) & (SG_CHUNK - 1)
    col = lax.broadcasted_iota(jnp.int32, (2 * SG_CHUNK, SG_CHUNK), 1)
    low_half = lax.broadcasted_iota(jnp.int32, (SG_CHUNK, SG_CHUNK), 1) < HEAD_DIM
    for j in range(SG_W // SG_CHUNK):
        cols = slice(j * SG_CHUNK, (j + 1) * SG_CHUNK)
        w_pair = jnp.where(row >= col, wsp_ref[j], 0.0).astype(BF16)
        rhs = jnp.concatenate([vn[c][:, cols] for c in range(n_chunks)], axis=1)
        r = jnp.dot(w_pair, rhs, preferred_element_type=F32)
        for c in range(n_chunks):
            rows = slice(c * SG_CHUNK, (c + 1) * SG_CHUNK)
            sv = jnp.where(low_half, r[:SG_CHUNK, rows], r[SG_CHUNK:, rows]) + bsp_ref[j]
            ob_ref[0, rows, cols] = (u[c][:, cols] * sv).astype(BF16)
        if j == 0:
            fill()


def _attn_kernel(q1_ref, q2_ref, q3_ref, kv1_ref, kv2_ref, kv3_ref, kp1_ref, kp2_ref, kp3_ref,
                 bias_ref, hm_ref, oa_ref, pv_nat, m_nat, l_nat, pv_pad, m_pad, l_pad):
    lead = (pl.program_id(1) == 0).astype(jnp.int32)

    groups = ((q1_ref, kv1_ref, kp1_ref), (q2_ref, kv2_ref, kp2_ref), (q3_ref, kv3_ref, kp3_ref))
    for g, (q_ref, kv_ref, kvp_ref) in enumerate(groups):
        dil = DILATIONS[g]
        nblk = SUPER // dil // Q_BLOCK
        for r in range(dil):
            kc = slice(r * 2 * HW, r * 2 * HW + HW)
            vc = slice(r * 2 * HW + HW, (r + 1) * 2 * HW)
            qc = slice(r * HW, (r + 1) * HW)
            for i in range(nblk):
                rq = slice(i * Q_BLOCK, (i + 1) * Q_BLOCK)
                if i == 0:
                    k2 = jnp.concatenate([kvp_ref[0, :, kc], kv_ref[0, rq, kc]], axis=0)
                    v2 = jnp.concatenate([kvp_ref[0, :, vc], kv_ref[0, rq, vc]], axis=0)
                else:
                    rk = slice((i - 1) * Q_BLOCK, (i + 1) * Q_BLOCK)
                    k2 = kv_ref[0, rk, kc]
                    v2 = kv_ref[0, rk, vc]
                table = lead if i == 0 else 0
                pv, m, l = _packed_heads_attention(
                    q_ref[0, rq, qc], k2, v2, hm_ref,
                    lambda h, table=table: bias_ref[g, table, h * Q_BLOCK:(h + 1) * Q_BLOCK, :])
                for c in range(HW // LANES):
                    lanes = slice(c * LANES, (c + 1) * LANES)
                    if dil == PADDED_DIL:
                        at = (c, pl.ds(i * Q_BLOCK * PAD_PITCH + r, Q_BLOCK, stride=PAD_PITCH))
                        dst = (pv_pad, m_pad, l_pad)
                    else:
                        nat = pl.ds(i * Q_BLOCK * dil + r, Q_BLOCK, stride=dil) if dil > 1 else rq
                        at = (g, c, nat)
                        dst = (pv_nat, m_nat, l_nat)
                    for ref, val in zip(dst, (pv, m, l)):
                        ref[at + (slice(None),)] = val[:, lanes]

    def padded_rows(ref, c, t):
        runs = COMBINE_ROWS // PADDED_DIL
        return jnp.concatenate([ref[c, pl.ds((t * runs + k) * PAD_PITCH, PADDED_DIL), :]
                                for k in range(runs)], axis=0)

    for t in range(SUPER // COMBINE_ROWS):
        rows = slice(t * COMBINE_ROWS, (t + 1) * COMBINE_ROWS)
        for c in range(HW // LANES):
            m1, m2, m3 = m_nat[0, c, rows, :], m_nat[1, c, rows, :], padded_rows(m_pad, c, t)
            m = jnp.maximum(jnp.maximum(m1, m2), m3)
            e1, e2, e3 = jnp.exp2(m1 - m), jnp.exp2(m2 - m), jnp.exp2(m3 - m)
            num = (e1 * pv_nat[0, c, rows, :] + e2 * pv_nat[1, c, rows, :]
                   + e3 * padded_rows(pv_pad, c, t))
            den = (e1 * l_nat[0, c, rows, :] + e2 * l_nat[1, c, rows, :]
                   + e3 * padded_rows(l_pad, c, t))
            oa_ref[0, rows, c * LANES:(c + 1) * LANES] = (num * (1.0 / den)).astype(BF16)


def _merge_kernel(x_ref, oa_ref, ob_ref, oc_ref, wg_ref, bg_ref, wpa_ref, wpb_ref, wpc_ref,
                  wout_ref, lng_ref, lnb_ref, h_ref):
    x = x_ref[...]
    xb = x.astype(BF16)

    def gated(k, o_ref, wp_ref, lo):
        cols = slice(k * D_MODEL + lo, k * D_MODEL + lo + GATE_COLS)
        t = jnp.tanh(jnp.dot(xb, wg_ref[:, cols], preferred_element_type=F32) + bg_ref[:, cols])
        half_p = jnp.dot(o_ref[...], wp_ref[:, lo:lo + GATE_COLS], preferred_element_type=F32)
        return half_p + half_p * t

    merged = jnp.concatenate(
        [(gated(0, oa_ref, wpa_ref, lo) + gated(1, ob_ref, wpb_ref, lo)
          + gated(2, oc_ref, wpc_ref, lo)).astype(BF16) for lo in range(0, D_MODEL, GATE_COLS)],
        axis=1)
    for c in range(x_ref.shape[0] // TAIL_ROWS):
        rows = slice(c * TAIL_ROWS, (c + 1) * TAIL_ROWS)
        y = jnp.dot(merged[rows], wout_ref[...], preferred_element_type=F32)
        h_ref[rows, :] = _layer_norm(ALPHA * x_ref[rows, :] + y, lng_ref[...], lnb_ref[...])


def _ffn_kernel(h_ref, wup_ref, cw_ref, cb_ref, wdn_ref, lng_ref, lnb_ref, o_ref,
                a_buf, carry, g_buf, *, tiles_per_seq):
    tm = h_ref.shape[0]
    n_chunks = D_FF // FF_CHUNK

    @pl.when(pl.program_id(0) % tiles_per_seq == 0)
    def _():
        carry[...] = jnp.zeros(carry.shape, F32)

    h = h_ref[...]
    hb = h.astype(BF16)
    for c in range(n_chunks):
        cols = slice(c * FF_CHUNK, (c + 1) * FF_CHUNK)
        a = jnp.dot(hb, wup_ref[:, cols], preferred_element_type=F32)
        gate = jnp.dot(hb, wup_ref[:, D_FF + c * FF_CHUNK:D_FF + (c + 1) * FF_CHUNK],
                       preferred_element_type=F32)
        a_buf[:CARRY_ROWS] = carry[c]
        a_buf[CARRY_ROWS:] = a
        carry[c] = a[tm - CARRY_ROWS:]
        conv = (cw_ref[0:1, cols] * a_buf[CARRY_ROWS - 2:CARRY_ROWS - 2 + tm]
                + cw_ref[1:2, cols] * a_buf[CARRY_ROWS - 1:CARRY_ROWS - 1 + tm]
                + cw_ref[2:3, cols] * a + cb_ref[:, cols])
        g_buf[:, cols] = (_gelu(conv) * gate).astype(BF16)
    for c in range(tm // TAIL_ROWS):
        rows = slice(c * TAIL_ROWS, (c + 1) * TAIL_ROWS)
        y = jnp.dot(g_buf[rows, :], wdn_ref[...], preferred_element_type=F32)
        o_ref[rows, :] = _layer_norm(ALPHA * h_ref[rows, :] + y, lng_ref[...], lnb_ref[...])


def _t5_bucket(dist):
    max_exact = N_BUCKETS // 2
    n = jnp.maximum(dist, 1).astype(jnp.float32)
    large = max_exact + (jnp.log(n / max_exact) / math.log(MAX_DISTANCE / max_exact)
                         * (N_BUCKETS - max_exact)).astype(jnp.int32)
    large = jnp.minimum(large, N_BUCKETS - 1)
    return jnp.where(dist < max_exact, dist, large)


def _offset_bias(rel_bias):
    steps = WINDOW_STEPS - jnp.arange(2 * Q_BLOCK)
    rows = []
    for g, dil in enumerate(DILATIONS):
        bucket = _t5_bucket(jnp.maximum(steps, 0) * dil)
        bias = rel_bias[:, g * HEADS:(g + 1) * HEADS][bucket].astype(F32)
        rows.append(jnp.where(steps[:, None] >= 0, bias, NEG).T)
    return jnp.stack(rows, 0)[:, :, None, :]


def _side_cast_plan(rows, max_blocks):
    block = BF16_ROWS
    while rows % block or rows // block > max_blocks:
        block += BF16_ROWS
    return block, rows // block


def _cparams(*sem):
    return pltpu.CompilerParams(dimension_semantics=sem, vmem_limit_bytes=VMEM_LIMIT)


def _layer(x, mem, band_bias, hmask, w_in, w_mem_kv, sg_ln_g, sg_ln_b, w_spatial, b_spatial,
           w_proj_a, w_proj_b, w_proj_c, w_gate, b_gate, w_out, ln1_g, ln1_b,
           w_ffn_up, conv_w, conv_b, w_ffn_down, ln2_g, ln2_b):
    bsz, seq, _ = x.shape
    n_tok = bsz * seq
    tm = TOKEN_TILE
    assert seq % SUPER == 0 and SUPER % tm == 0 and tm % (SG_CHUNK * 2) == 0
    assert D_FF % FF_CHUNK == 0 and tm % (16 * DILATIONS[-1]) == 0
    assert seq % MERGE_TILE == 0 and seq % FFN_TILE == 0
    tiles_per_seq = seq // tm
    row = lambda v: v.reshape(1, -1)
    tok = lambda rows, width: pl.BlockSpec((rows, width), lambda i: (i, 0))
    tile3 = lambda rows, width: pl.BlockSpec((1, rows, width), lambda b, t: (b, t, 0))

    wsp = w_spatial.reshape(SG_W // SG_CHUNK, 2 * SG_CHUNK, SG_CHUNK)
    bsp = jnp.repeat(b_spatial.reshape(SG_W // SG_CHUNK, 2, SG_CHUNK).transpose(0, 2, 1),
                     HEAD_DIM, axis=-1)
    qkv_shapes = ([jax.ShapeDtypeStruct((bsz, seq // d, d * HW), BF16) for d in DILATIONS]
                  + [jax.ShapeDtypeStruct((bsz, seq // d, d * 2 * HW), BF16) for d in DILATIONS])
    qkv_specs = ([tile3(tm // d, d * HW) for d in DILATIONS]
                 + [tile3(tm // d, d * 2 * HW) for d in DILATIONS])
    side_w = ((w_gate, 0.5), (w_proj_a, 0.5), (w_proj_b, 0.5), (w_proj_c, 0.5), (w_out, 1.0),
              (w_ffn_up, 1.0), (w_ffn_down, 1.0))
    assert SIDE_CAST_BLOCKS <= bsz * tiles_per_seq
    plans = [_side_cast_plan(w.shape[0], SIDE_CAST_BLOCKS) for w, _ in side_w]

    def side_spec(w, plan):
        rows, n_blocks = plan
        return pl.BlockSpec((rows, w.shape[1]), lambda b, t: (
            jnp.minimum(b * tiles_per_seq + t, n_blocks - 1), 0))

    side_specs = [side_spec(w, plan) for (w, _), plan in zip(side_w, plans)]
    outs = pl.pallas_call(
        functools.partial(_proj_kernel,
                          side=tuple((scale, plan[1]) for (_, scale), plan in zip(side_w, plans))),
        grid=(bsz, tiles_per_seq),
        in_specs=[tile3(tm, D_MODEL), _resident((D_MODEL, IN_W)),
                  pl.BlockSpec((1, MEM_LEN, D_MODEL), lambda b, t: (b, 0, 0)),
                  _resident((D_MODEL, 2 * HW)), _resident((1, SG_W)), _resident((1, SG_W)),
                  _resident(wsp.shape), _resident(bsp.shape), _resident(hmask.shape)] + side_specs,
        out_specs=qkv_specs + [tile3(tm, SG_W), tile3(tm, HW)] + side_specs,
        out_shape=qkv_shapes + [jax.ShapeDtypeStruct((bsz, seq, SG_W), BF16),
                                jax.ShapeDtypeStruct((bsz, seq, HW), BF16)]
        + [jax.ShapeDtypeStruct(w.shape, BF16) for w, _ in side_w],
        scratch_shapes=[pltpu.VMEM((tm // PADDED_DIL * PAD_PITCH, LANES), F32),
                        pltpu.VMEM((MEM_LEN, 2 * HW), BF16)],
        compiler_params=_cparams("arbitrary", "arbitrary"),
        name="proj",
    )(x, w_in.astype(BF16), mem, w_mem_kv.astype(BF16), row(sg_ln_g), row(sg_ln_b), wsp, bsp,
      hmask, *[w for w, _ in side_w])
    q1, q2, q3, kv1, kv2, kv3, ob, oc = outs[:8]
    wg_b, wpa_b, wpb_b, wpc_b, wout_b, wup_b, wdn_b = outs[8:]

    n_super = seq // SUPER
    super3 = lambda d, width: pl.BlockSpec((1, SUPER // d, d * width), lambda b, j: (b, j, 0))

    def prev3(d):
        nblk = SUPER // d // Q_BLOCK
        return pl.BlockSpec((1, Q_BLOCK, d * 2 * HW),
                            lambda b, j: (b, jnp.maximum(j * nblk - 1, 0), 0))

    oa = pl.pallas_call(
        _attn_kernel,
        grid=(bsz, n_super),
        in_specs=[super3(d, HW) for d in DILATIONS] + [super3(d, 2 * HW) for d in DILATIONS]
        + [prev3(d) for d in DILATIONS] + [_resident(band_bias.shape), _resident(hmask.shape)],
        out_specs=pl.BlockSpec((1, SUPER, HW), lambda b, j: (b, j, 0)),
        out_shape=jax.ShapeDtypeStruct((bsz, seq, HW), BF16),
        scratch_shapes=[pltpu.VMEM((N_GROUPS - 1, HW // LANES, SUPER, LANES), F32)] * 3
        + [pltpu.VMEM((HW // LANES, SUPER // PADDED_DIL * PAD_PITCH, LANES), F32)] * 3,
        compiler_params=_cparams("parallel", "parallel"),
        name="attn",
    )(q1, q2, q3, kv1, kv2, kv3, kv1, kv2, kv3, band_bias, hmask)

    h1 = pl.pallas_call(
        _merge_kernel,
        grid=(n_tok // MERGE_TILE,),
        in_specs=[tok(MERGE_TILE, D_MODEL), tok(MERGE_TILE, HW), tok(MERGE_TILE, SG_W),
                  tok(MERGE_TILE, HW),
                  _resident((D_MODEL, 3 * D_MODEL)), _resident((1, 3 * D_MODEL)),
                  _resident((HW, D_MODEL)), _resident((SG_W, D_MODEL)), _resident((HW, D_MODEL)),
                  _resident((D_MODEL, D_MODEL)), _resident((1, D_MODEL)), _resident((1, D_MODEL))],
        out_specs=tok(MERGE_TILE, D_MODEL),
        out_shape=jax.ShapeDtypeStruct((n_tok, D_MODEL), F32),
        compiler_params=_cparams("parallel"),
        name="merge",
    )(x.reshape(n_tok, D_MODEL), oa.reshape(n_tok, HW), ob.reshape(n_tok, SG_W),
      oc.reshape(n_tok, HW), wg_b, row(0.5 * b_gate), wpa_b, wpb_b, wpc_b, wout_b,
      row(ln1_g), row(ln1_b))

    h2 = pl.pallas_call(
        functools.partial(_ffn_kernel, tiles_per_seq=seq // FFN_TILE),
        grid=(n_tok // FFN_TILE,),
        in_specs=[tok(FFN_TILE, D_MODEL), _resident((D_MODEL, 2 * D_FF)), _resident(conv_w.shape),
                  _resident((1, D_FF)), _resident((D_FF, D_MODEL)),
                  _resident((1, D_MODEL)), _resident((1, D_MODEL))],
        out_specs=tok(FFN_TILE, D_MODEL),
        out_shape=jax.ShapeDtypeStruct((n_tok, D_MODEL), F32),
        scratch_shapes=[pltpu.VMEM((CARRY_ROWS + FFN_TILE, FF_CHUNK), F32),
                        pltpu.VMEM((D_FF // FF_CHUNK, CARRY_ROWS, FF_CHUNK), F32),
                        pltpu.VMEM((FFN_TILE, D_FF), BF16)],
        compiler_params=_cparams("arbitrary"),
        name="ffn",
    )(h1, wup_b, conv_w, row(conv_b), wdn_b, row(ln2_g), row(ln2_b))
    return h2.reshape(bsz, seq, D_MODEL)


def kernel(x, mem, rel_bias, w_in, w_mem_kv, sg_ln_g, sg_ln_b, w_spatial, b_spatial, w_proj_a,
           w_proj_b, w_proj_c, w_gate, b_gate, w_out, ln1_g, ln1_b, w_ffn_up, conv_w, conv_b,
           w_ffn_down, ln2_g, ln2_b):
    band_bias = pl.pallas_call(
        _band_bias_kernel,
        grid=(N_GROUPS,),
        in_specs=[pl.BlockSpec((1, HEADS, 1, 2 * Q_BLOCK), lambda g: (g, 0, 0, 0))],
        out_specs=pl.BlockSpec((1, 2, HEADS * Q_BLOCK, 2 * Q_BLOCK), lambda g: (g, 0, 0, 0)),
        out_shape=jax.ShapeDtypeStruct((N_GROUPS, 2, HEADS * Q_BLOCK, 2 * Q_BLOCK), F32),
        name="band_bias",
    )(_offset_bias(rel_bias))
    lane = jnp.arange(HW)[None, None, :] // HEAD_DIM
    hmask = jnp.broadcast_to(lane == jnp.arange(HEADS)[:, None, None],
                             (HEADS, Q_BLOCK, HW)).astype(BF16)
    h = x
    for l in range(w_in.shape[0]):
        h = _layer(h, mem, band_bias, hmask, w_in[l], w_mem_kv[l], sg_ln_g[l], sg_ln_b[l],
                   w_spatial[l], b_spatial[l], w_proj_a[l], w_proj_b[l], w_proj_c[l], w_gate[l],
                   b_gate[l], w_out[l], ln1_g[l], ln1_b[l], w_ffn_up[l], conv_w[l], conv_b[l],
                   w_ffn_down[l], ln2_g[l], ln2_b[l])
    return h
```
